```python
import math
import jax
import jax.numpy as jnp
from jax import lax
import numpy as np

D_MODEL = 1024
BATCH = 16
SEQ = 4096
DEPTH = 2

GRID_W = 64
CTX_LEN = 256
MIX_W = D_MODEL
HEAD_DIM = 64
ATTN_W = MIX_W // 2
N_ATTN_HEADS = ATTN_W // HEAD_DIM
QK_DIM = HEAD_DIM // 2
CONV_W = MIX_W // 4
CONV_K = 3
POOL_W = MIX_W - ATTN_W - CONV_W
POOL_WINDOWS = (2, 4, 8, 16)
POOL_GC = POOL_W // len(POOL_WINDOWS)
Q_OFF = 0
K_OFF = ATTN_W
V_OFF = 2 * ATTN_W
CB_OFF = 3 * ATTN_W
CC_OFF = CB_OFF + CONV_W
CU_OFF = CC_OFF + CONV_W
POOL_OFF = CU_OFF + CONV_W
IN_W = POOL_OFF + POOL_W
ROPE_BASE = 10000.0
ROPE_PAIRS = QK_DIM // 4
N_EXPERTS = 32
TOP_K = 4
D_FF = D_MODEL
SWIGLU_ALPHA = 1.702
SWIGLU_LIMIT = 7.0
EXPERT_BLOCK = 256
Q_BLOCK = 128
N_MOD = 6
EPS = 1e-6

kernel_name = "hymba_diffattn_conv_pool_moe_dit"


def rms_norm(t, g):
    tf = t.astype(jnp.float32)
    y = tf * lax.rsqrt(jnp.mean(tf * tf, axis=-1, keepdims=True) + EPS)
    return y.astype(t.dtype) * g


def modulate(t, shift, scale):
    return t * (1 + scale) + shift


def split_heads(t, dh):
    b, l, _ = t.shape
    return t.reshape(b, l, -1, dh).transpose(0, 2, 1, 3)


def axial_rope(L):
    rows = L // GRID_W
    row = jnp.broadcast_to(jnp.arange(rows, dtype=jnp.float32)[:, None], (rows, GRID_W)).reshape(L)
    col = jnp.broadcast_to(jnp.arange(GRID_W, dtype=jnp.float32)[None, :], (rows, GRID_W)).reshape(L)
    inv = ROPE_BASE ** (-jnp.arange(ROPE_PAIRS, dtype=jnp.float32) / ROPE_PAIRS)
    ang = jnp.stack([row, col], axis=-1)[..., None] * inv
    return jnp.cos(ang), jnp.sin(ang)


def apply_rope(t, cos, sin):
    ts = t.reshape(*t.shape[:-1], 2, 2, ROPE_PAIRS)
    t1, t2 = ts[..., 0, :], ts[..., 1, :]
    cos = cos.astype(t.dtype)
    sin = sin.astype(t.dtype)
    out = jnp.stack([t1 * cos - t2 * sin, t2 * cos + t1 * sin], axis=-2)
    return out.reshape(t.shape)


def qk_pair(t, g, rope):
    t = split_heads(t, HEAD_DIM)
    t1 = rms_norm(t[..., :QK_DIM], g)
    t2 = rms_norm(t[..., QK_DIM:], g)
    if rope is not None:
        t1 = apply_rope(t1, *rope)
        t2 = apply_rope(t2, *rope)
    return t1, t2


def keys_values(pkv, g, rope):
    k1, k2 = qk_pair(pkv[..., :ATTN_W], g, rope)
    v = split_heads(pkv[..., ATTN_W:], HEAD_DIM)
    return k1, k2, v


def diff_attention(q1, q2, k1, k2, v, lam):
    b, h, lq, dq = q1.shape
    nb = lq // Q_BLOCK
    scale = dq ** -0.5

    def to_blocks(q):
        return jnp.moveaxis(q.reshape(b, h, nb, Q_BLOCK, dq), 2, 0)

    def block(qs):
        a1, a2 = qs
        s1 = jnp.einsum('bhqd,bhkd->bhqk', a1, k1).astype(jnp.float32) * scale
        s2 = jnp.einsum('bhqd,bhkd->bhqk', a2, k2).astype(jnp.float32) * scale
        pmap = jax.nn.softmax(s1, axis=-1) - lam * jax.nn.softmax(s2, axis=-1)
        return jnp.einsum('bhqk,bhkd->bhqd', pmap.astype(v.dtype), v)

    o = lax.map(block, (to_blocks(q1), to_blocks(q2)))
    return jnp.moveaxis(o, 0, 2).reshape(b, h, lq, v.shape[-1])


def short_conv(gate_b, gate_c, u, w):
    z = gate_c * u
    zp = jnp.pad(z, ((0, 0), (1, 1), (0, 0)))
    conv = w[0] * zp[:, :-2] + w[1] * zp[:, 1:-1] + w[2] * zp[:, 2:]
    return gate_b * conv


def multiscale_pool(u, w_pool, scale):
    bn, L, C = u.shape
    cs = jnp.cumsum(u.astype(jnp.float32), axis=1)
    S = jnp.concatenate([jnp.zeros((bn, 1, C), jnp.float32), cs], axis=1)
    t = jnp.arange(L)
    groups = []
    for g, win in enumerate(POOL_WINDOWS):
        half = win // 2
        lo = jnp.clip(t - half, 0, L)
        hi = jnp.clip(t + half, 0, L)
        Sg = S[..., g * POOL_GC:(g + 1) * POOL_GC]
        mean = (Sg[:, hi] - Sg[:, lo]) / (hi - lo).astype(jnp.float32)[:, None]
        groups.append(mean - u[..., g * POOL_GC:(g + 1) * POOL_GC].astype(jnp.float32))
    d = jnp.stack(groups, axis=2).astype(u.dtype)
    y = jnp.einsum('blgc,gcd->blgd', d, w_pool).reshape(bn, L, POOL_W)
    return y * scale


def mix_tokens(p, keys, rope, lam, lam_init, q_g, subln_g, conv_w, pool_w, pool_scale, w_out):
    bn, L, _ = p.shape
    q1, q2 = qk_pair(p[..., Q_OFF:K_OFF], q_g, rope)
    k1, k2, v = keys
    o = diff_attention(q1, q2, k1, k2, v, lam)
    o = rms_norm(o, subln_g) * (1.0 - lam_init)
    o_att = o.transpose(0, 2, 1, 3).reshape(bn, L, ATTN_W)
    o_conv = short_conv(p[..., CB_OFF:CC_OFF], p[..., CC_OFF:CU_OFF], p[..., CU_OFF:POOL_OFF], conv_w)
    o_pool = multiscale_pool(p[..., POOL_OFF:IN_W], pool_w, pool_scale)
    return jnp.concatenate([o_att, o_conv, o_pool], axis=-1) @ w_out


def moe_ffn(h, w_r, b_r, w_gu, b_gu, w_dn, b_dn):
    T, D = h.shape
    logits = (h @ w_r).astype(jnp.float32) + b_r.astype(jnp.float32)
    top_logit, top_e = lax.top_k(logits, TOP_K)
    gate_w = jax.nn.softmax(top_logit, axis=-1)
    n_assign = T * TOP_K
    flat_e = top_e.reshape(-1)
    order = jnp.argsort(flat_e)
    e_sorted = flat_e[order]
    tok_sorted = (order // TOP_K).astype(jnp.int32)
    w_sorted = gate_w.reshape(-1)[order]
    counts = jax.ops.segment_sum(jnp.ones_like(flat_e), flat_e, num_segments=N_EXPERTS)
    padded = (counts + EXPERT_BLOCK - 1) // EXPERT_BLOCK * EXPERT_BLOCK
    pad_end = jnp.cumsum(padded)
    pad_start = pad_end - padded
    start = jnp.cumsum(counts) - counts
    dest = pad_start[e_sorted] + (jnp.arange(n_assign) - start[e_sorted])
    n_blocks = -(-(n_assign + N_EXPERTS * (EXPERT_BLOCK - 1)) // EXPERT_BLOCK)
    n_rows = n_blocks * EXPERT_BLOCK
    row_tok = jnp.full((n_rows,), T, jnp.int32).at[dest].set(tok_sorted)
    row_w = jnp.zeros((n_rows,), jnp.float32).at[dest].set(w_sorted)
    block_e = jnp.minimum(jnp.searchsorted(pad_end, jnp.arange(n_blocks) * EXPERT_BLOCK, side='right'),
                          N_EXPERTS - 1)
    h_pad = jnp.concatenate([h, jnp.zeros((1, D), h.dtype)], axis=0)

    def expert_block(args):
        toks, wts, e = args
        xb = h_pad[toks]
        gu = xb @ w_gu[e] + b_gu[e]
        glu = jnp.minimum(gu[:, :D_FF], SWIGLU_LIMIT)
        lin = jnp.clip(gu[:, D_FF:], -SWIGLU_LIMIT, SWIGLU_LIMIT)
        act = glu * jax.nn.sigmoid(SWIGLU_ALPHA * glu) * (lin + 1)
        y = act @ w_dn[e] + b_dn[e]
        return y * wts[:, None].astype(y.dtype)

    y = lax.map(expert_block, (row_tok.reshape(n_blocks, EXPERT_BLOCK),
                               row_w.reshape(n_blocks, EXPERT_BLOCK), block_e))
    return jax.ops.segment_sum(y.reshape(n_rows, D), row_tok, num_segments=T + 1)[:T]


def setup_inputs(seed: int = 0) -> dict:
    key = jax.random.key(seed)
    ks = jax.random.split(key, 26)
    f32 = jnp.float32

    def nrm(k, shape, s):
        return jax.random.normal(k, shape, f32) * s

    return {
        "x": nrm(ks[0], (BATCH, SEQ, D_MODEL), 1.0),
        "c": nrm(ks[1], (BATCH, D_MODEL), 1.0),
        "ctx": nrm(ks[2], (BATCH, CTX_LEN, D_MODEL), 1.0),
        "c_ctx": nrm(ks[3], (D_MODEL,), 1.0),
        "w_mod": nrm(ks[4], (DEPTH, D_MODEL, N_MOD * D_MODEL), 0.3 * D_MODEL ** -0.5),
        "b_mod": nrm(ks[5], (DEPTH, N_MOD * D_MODEL), 0.02),
        "norm1_g": 1.0 + nrm(ks[6], (DEPTH, D_MODEL), 0.05),
        "norm2_g": 1.0 + nrm(ks[7], (DEPTH, D_MODEL), 0.05),
        "w_in": nrm(ks[8], (DEPTH, D_MODEL, IN_W), D_MODEL ** -0.5),
        "q_norm_g": 1.0 + nrm(ks[9], (DEPTH, QK_DIM), 0.05),
        "k_norm_g": 1.0 + nrm(ks[10], (DEPTH, QK_DIM), 0.05),
        "lambda_q1": nrm(ks[11], (DEPTH, QK_DIM), 0.1),
        "lambda_k1": nrm(ks[12], (DEPTH, QK_DIM), 0.1),
        "lambda_q2": nrm(ks[13], (DEPTH, QK_DIM), 0.1),
        "lambda_k2": nrm(ks[14], (DEPTH, QK_DIM), 0.1),
        "subln_g": 1.0 + nrm(ks[15], (DEPTH, HEAD_DIM), 0.05),
        "conv_w": nrm(ks[16], (DEPTH, CONV_K, CONV_W), CONV_K ** -0.5),
        "pool_w": nrm(ks[17], (DEPTH, len(POOL_WINDOWS), POOL_GC, POOL_GC), POOL_GC ** -0.5),
        "pool_scale": 1.0 + nrm(ks[18], (DEPTH, POOL_W), 0.1),
        "w_out": nrm(ks[19], (DEPTH, MIX_W, D_MODEL), MIX_W ** -0.5),
        "router_w": nrm(ks[20], (DEPTH, D_MODEL, N_EXPERTS), D_MODEL ** -0.5),
        "router_b": nrm(ks[21], (DEPTH, N_EXPERTS), 0.01),
        "w_gate_up": nrm(ks[22], (DEPTH, N_EXPERTS, D_MODEL, 2 * D_FF), D_MODEL ** -0.5),
        "b_gate_up": nrm(ks[23], (DEPTH, N_EXPERTS, 2 * D_FF), 0.02),
        "w_down": nrm(ks[24], (DEPTH, N_EXPERTS, D_FF, D_MODEL), D_FF ** -0.5),
        "b_down": nrm(ks[25], (DEPTH, N_EXPERTS, D_MODEL), 0.02),
    }


def reference(x, c, ctx, c_ctx, w_mod, b_mod, norm1_g, norm2_g, w_in, q_norm_g, k_norm_g,
              lambda_q1, lambda_k1, lambda_q2, lambda_k2, subln_g, conv_w, pool_w, pool_scale, w_out,
              router_w, router_b, w_gate_up, b_gate_up, w_down, b_down):
    B, L, D = x.shape
    rope = axial_rope(L)
    f32 = jnp.float32
    for l in range(DEPTH):
        has_next = l + 1 < DEPTH
        lam_init = 0.8 - 0.6 * math.exp(-0.3 * l)
        lam = (jnp.exp(jnp.sum(lambda_q1[l].astype(f32) * lambda_k1[l].astype(f32)))
               - jnp.exp(jnp.sum(lambda_q2[l].astype(f32) * lambda_k2[l].astype(f32))) + lam_init)
        mod = (jax.nn.silu(c) @ w_mod[l] + b_mod[l]).reshape(B, 1, N_MOD, D)
        mod_c = (jax.nn.silu(c_ctx) @ w_mod[l] + b_mod[l]).reshape(N_MOD, D)

        h = modulate(rms_norm(x, norm1_g[l]), mod[:, :, 0], mod[:, :, 1])
        hc = modulate(rms_norm(ctx, norm1_g[l]), mod_c[0], mod_c[1])
        p = h @ w_in[l]
        if has_next:
            pc = hc @ w_in[l]
            pc_kv = pc[..., K_OFF:CB_OFF]
        else:
            pc_kv = hc @ w_in[l][:, K_OFF:CB_OFF]
        ck1, ck2, cv = keys_values(pc_kv, k_norm_g[l], None)
        k1, k2, v = keys_values(p[..., K_OFF:CB_OFF], k_norm_g[l], rope)
        lat_keys = (jnp.concatenate([ck1, k1], axis=2),
                    jnp.concatenate([ck2, k2], axis=2),
                    jnp.concatenate([cv, v], axis=2))
        layer_w = (q_norm_g[l], subln_g[l], conv_w[l], pool_w[l], pool_scale[l], w_out[l])
        x = x + mod[:, :, 2] * mix_tokens(p, lat_keys, rope, lam, lam_init, *layer_w)
        if has_next:
            ctx = ctx + mod_c[2] * mix_tokens(pc, (ck1, ck2, cv), None, lam, lam_init, *layer_w)

        moe_w = (router_w[l], router_b[l], w_gate_up[l], b_gate_up[l], w_down[l], b_down[l])
        h2 = modulate(rms_norm(x, norm2_g[l]), mod[:, :, 3], mod[:, :, 4])
        if has_next:
            h2c = modulate(rms_norm(ctx, norm2_g[l]), mod_c[3], mod_c[4])
            n_c = h2c.shape[0] * h2c.shape[1]
            y = moe_ffn(jnp.concatenate([h2c.reshape(-1, D), h2.reshape(-1, D)], axis=0), *moe_w)
            ctx = ctx + mod_c[5] * y[:n_c].reshape(ctx.shape)
            x = x + mod[:, :, 5] * y[n_c:].reshape(x.shape)
        else:
            x = x + mod[:, :, 5] * moe_ffn(h2.reshape(-1, D), *moe_w).reshape(x.shape)
    return x
```

```python
import functools
import math

import jax
import jax.numpy as jnp
from jax import lax
from jax.experimental import pallas as pl
from jax.experimental.pallas import tpu as pltpu

F32 = jnp.float32
BF16 = jnp.bfloat16
I32 = jnp.int32
U32 = jnp.uint32

GRID_W = 64
CTX_LEN = 256
HEAD_DIM = 64
QK_DIM = 32
N_HEADS = 8
ATTN_W = 512
CONV_W = 256
POOL_W = 256
POOL_WINDOWS = (2, 4, 8, 16)
POOL_GC = 64
ROPE_BASE = 10000.0
ROPE_PAIRS = 8
N_EXPERTS = 32
TOP_K = 4
N_MOD = 6
EPS = 1e-6
SWIGLU_ALPHA = 1.702
SWIGLU_LIMIT = 7.0
LOG2E = 1.4426950408889634

LANES = 128
SUBLANES = 8
ROW_TILE = 256
N_PAIRS = N_HEADS // 2
POOL_HALO = 8
EXPERT_ROWS = 256
RANK_TILE = 512
VMEM_LIMIT = 48 * 1024 * 1024
NEG_BIG = -1e30


def _split_bf16(a):
    hi = a.astype(BF16)
    lo = (a - hi.astype(F32)).astype(BF16)
    return hi, lo


def _dot(a, b):
    return jnp.dot(a, b, preferred_element_type=F32)


def _mod_kernel(c_ref, w_ref, b_ref, o_ref):
    cv = c_ref[...]
    s = cv * jax.nn.sigmoid(cv)
    s_hi, s_lo = _split_bf16(s)
    w_hi, w_lo = _split_bf16(w_ref[0])
    o_ref[0] = _dot(s_hi, w_hi) + _dot(s_lo, w_hi) + _dot(s_hi, w_lo) + b_ref[0]


def _mod_call(c_all, w_mod, b_mod):
    depth, d, n = w_mod.shape
    r = c_all.shape[0]
    tn = 1536
    return pl.pallas_call(
        _mod_kernel,
        grid=(depth, n // tn),
        in_specs=[
            pl.BlockSpec((r, d), lambda l, j: (0, 0)),
            pl.BlockSpec((1, d, tn), lambda l, j: (l, 0, j)),
            pl.BlockSpec((1, 1, tn), lambda l, j: (l, 0, j)),
        ],
        out_specs=pl.BlockSpec((1, r, tn), lambda l, j: (l, 0, j)),
        out_shape=jax.ShapeDtypeStruct((depth, r, n), F32),
        compiler_params=pltpu.CompilerParams(vmem_limit_bytes=VMEM_LIMIT),
        name="mod",
    )(c_all, w_mod, b_mod.reshape(depth, 1, n))


def _rms_mod(x, g, shift, scale):
    y = x * lax.rsqrt(jnp.mean(x * x, axis=-1, keepdims=True) + EPS) * g
    return y * (1.0 + scale) + shift


def _in_kernel(x_ref, mod_ref, g1_ref, wm_ref, wqk_ref, gain_ref, cos_ref, sin_ref,
               q_ref, kt_ref, v_ref, cp_ref):
    tm = x_ref.shape[1]
    h = _rms_mod(x_ref[0], g1_ref[...], mod_ref[0, 0:1, :], mod_ref[0, 1:2, :])
    pm = _dot(h.astype(BF16), wm_ref[...])
    for p in range(N_PAIRS):
        v_ref[0, p] = pm[:, p * LANES:(p + 1) * LANES].astype(BF16)
    o = ATTN_W
    cp_ref[0, :, 0:CONV_W] = pm[:, o:o + CONV_W]
    cp_ref[0, :, CONV_W:2 * CONV_W] = pm[:, o + CONV_W:o + 2 * CONV_W] * pm[:, o + 2 * CONV_W:o + 3 * CONV_W]
    cp_ref[0, :, 2 * CONV_W:] = pm[:, o + 3 * CONV_W:]

    ht = h.T.astype(BF16)
    qkt = _dot(wqk_ref[...], ht)
    ng = 2 * ATTN_W // QK_DIM
    t = qkt.reshape(ng, 4, ROPE_PAIRS, tm)
    ss = jnp.sum(jnp.sum(t * t, axis=2, keepdims=True), axis=1, keepdims=True)
    tn = t * lax.rsqrt(ss * (1.0 / QK_DIM) + EPS) * gain_ref[...].reshape(ng, 4, ROPE_PAIRS, tm)
    cs = cos_ref[...].reshape(2, ROPE_PAIRS, tm)
    sn = sin_ref[...].reshape(2, ROPE_PAIRS, tm)
    parts = []
    for a in range(2):
        t1 = tn[:, 2 * a]
        t2 = tn[:, 2 * a + 1]
        parts.append(t1 * cs[a] - t2 * sn[a])
        parts.append(t2 * cs[a] + t1 * sn[a])
    rot = jnp.stack(parts, axis=1).reshape(2 * ATTN_W, tm)
    qt = rot[:ATTN_W].T
    for p in range(N_PAIRS):
        q_ref[0, p] = qt[:, p * LANES:(p + 1) * LANES].astype(BF16)
        kt_ref[0, 0, p] = rot[ATTN_W + p * LANES:ATTN_W + (p + 1) * LANES].astype(BF16)


def _in_call(s, mod, g1, wm, wqk, gain, cos_t, sin_t):
    b, la, d = s.shape
    nt = la // ROW_TILE
    nmain = wm.shape[1]

    def mod_idx(i, j):
        return (jnp.where(j == 0, b, i), 0, 0)

    return pl.pallas_call(
        _in_kernel,
        grid=(b, nt),
        in_specs=[
            pl.BlockSpec((1, ROW_TILE, d), lambda i, j: (i, j, 0)),
            pl.BlockSpec((1, N_MOD, d), mod_idx),
            pl.BlockSpec((1, d), lambda i, j: (0, 0)),
            pl.BlockSpec((d, nmain), lambda i, j: (0, 0)),
            pl.BlockSpec((2 * ATTN_W, d), lambda i, j: (0, 0)),
            pl.BlockSpec((2 * ATTN_W, ROW_TILE), lambda i, j: (0, 0)),
            pl.BlockSpec((2 * ROPE_PAIRS, ROW_TILE), lambda i, j: (0, j)),
            pl.BlockSpec((2 * ROPE_PAIRS, ROW_TILE), lambda i, j: (0, j)),
        ],
        out_specs=[
            pl.BlockSpec((1, N_PAIRS, ROW_TILE, LANES), lambda i, j: (i, 0, j, 0)),
            pl.BlockSpec((1, 1, N_PAIRS, LANES, ROW_TILE), lambda i, j: (i, j, 0, 0, 0)),
            pl.BlockSpec((1, N_PAIRS, ROW_TILE, LANES), lambda i, j: (i, 0, j, 0)),
            pl.BlockSpec((1, ROW_TILE, 3 * CONV_W), lambda i, j: (i, j, 0)),
        ],
        out_shape=[
            jax.ShapeDtypeStruct((b, N_PAIRS, la, LANES), BF16),
            jax.ShapeDtypeStruct((b, nt, N_PAIRS, LANES, ROW_TILE), BF16),
            jax.ShapeDtypeStruct((b, N_PAIRS, la, LANES), BF16),
            jax.ShapeDtypeStruct((b, la, 3 * CONV_W), F32),
        ],
        compiler_params=pltpu.CompilerParams(vmem_limit_bytes=VMEM_LIMIT),
        name="inproj",
    )(s, mod, g1, wm, wqk, gain, cos_t, sin_t)


def _attn_kernel(lam_ref, q_ref, kt_ref, v_ref, sg_ref, o_ref, s_scr, *, lam_init):
    tq = q_ref.shape[2]
    nkc_all = kt_ref.shape[1]
    j = pl.program_id(1)
    nkc = jnp.where(j == 0, 1, nkc_all)
    lp = lam_ref[...]
    lam = (jnp.exp(jnp.sum(lp[0:1] * lp[1:2], axis=-1, keepdims=True))
           - jnp.exp(jnp.sum(lp[2:3] * lp[3:4], axis=-1, keepdims=True)) + lam_init)
    lane = lax.broadcasted_iota(I32, (1, LANES), 1)
    low = lane < HEAD_DIM

    for p in range(N_PAIRS):
        qp = q_ref[0, p]
        heads = []
        for hh in range(2):
            comps = []
            for c in range(2):
                lo = hh * HEAD_DIM + c * QK_DIM
                qm = jnp.where((lane >= lo) & (lane < lo + QK_DIM), qp, jnp.zeros_like(qp))

                def sweep_max(kc, mrun, qm=qm, p=p):
                    sc = _dot(qm, kt_ref[0, kc, p])
                    s_scr[kc] = sc
                    return jnp.maximum(mrun, jnp.maximum(sc[:, :LANES], sc[:, LANES:]))

                mrun = lax.fori_loop(0, nkc, sweep_max, jnp.full((tq, LANES), -jnp.inf, F32))
                m = jnp.max(mrun, axis=-1, keepdims=True)

                def sweep_pv(kc, carry, m=m, p=p):
                    lrun, acc = carry
                    e = jnp.exp2(s_scr[kc] - m)
                    lrun = lrun + (e[:, :LANES] + e[:, LANES:])
                    r0 = pl.multiple_of(kc * ROW_TILE, ROW_TILE)
                    acc = acc + _dot(e.astype(BF16), v_ref[0, p, pl.ds(r0, ROW_TILE), :])
                    return lrun, acc

                zero = jnp.zeros((tq, LANES), F32)
                lrun, acc = lax.fori_loop(0, nkc, sweep_pv, (zero, zero))
                comps.append(acc / jnp.sum(lrun, axis=-1, keepdims=True))
            heads.append(comps[0] - lam * comps[1])
        o = jnp.where(low, heads[0], heads[1])
        o2 = o * o
        ss_lo = jnp.sum(jnp.where(low, o2, 0.0), axis=-1, keepdims=True)
        ss_hi = jnp.sum(jnp.where(low, 0.0, o2), axis=-1, keepdims=True)
        inv = lax.rsqrt(jnp.where(low, ss_lo, ss_hi) * (1.0 / HEAD_DIM) + EPS)
        o_ref[0, p] = (o * inv * sg_ref[...] * (1.0 - lam_init)).astype(BF16)


def _attn_call(lam_p, q, kt, v, sg, lam_init):
    b, _, la, _ = q.shape
    nt = la // ROW_TILE
    return pl.pallas_call(
        functools.partial(_attn_kernel, lam_init=lam_init),
        grid=(b, nt),
        in_specs=[
            pl.BlockSpec((SUBLANES, LANES), lambda i, j: (0, 0)),
            pl.BlockSpec((1, N_PAIRS, ROW_TILE, LANES), lambda i, j: (i, 0, j, 0)),
            pl.BlockSpec((1, nt, N_PAIRS, LANES, ROW_TILE), lambda i, j: (i, 0, 0, 0, 0)),
            pl.BlockSpec((1, N_PAIRS, la, LANES), lambda i, j: (i, 0, 0, 0)),
            pl.BlockSpec((1, LANES), lambda i, j: (0, 0)),
        ],
        out_specs=pl.BlockSpec((1, N_PAIRS, ROW_TILE, LANES), lambda i, j: (i, 0, j, 0)),
        out_shape=jax.ShapeDtypeStruct((b, N_PAIRS, la, LANES), BF16),
        scratch_shapes=[pltpu.VMEM((nt, ROW_TILE, ROW_TILE), F32)],
        compiler_params=pltpu.CompilerParams(vmem_limit_bytes=VMEM_LIMIT),
        name="attn",
    )(lam_p, q, kt, v, sg)


def _shift_rows(a, k):
    return pltpu.roll(a, k % a.shape[0], axis=0)


def _out_kernel(x_ref, oa_ref, cp_ref, cpp_ref, cpn_ref, cw_ref, pw_ref, ps_ref, wo_ref,
                mod_ref, g2_ref, wrh_ref, wrl_ref, br_ref,
                xo_ref, hu_ref, te_ref, tw_ref, *, seq_len):
    tm = x_ref.shape[1]
    d = x_ref.shape[2]
    j = pl.program_id(1)
    nt = pl.num_programs(1)
    halo = POOL_HALO
    has_prev = j >= 2
    has_next = (j >= 1) & (j < nt - 1)
    prev = jnp.where(has_prev, cpp_ref[0], 0.0)
    nxt = jnp.where(has_next, cpn_ref[0], 0.0)
    ext = jnp.concatenate([prev, cp_ref[0], nxt], axis=0)

    z = ext[:, CONV_W:2 * CONV_W]
    conv = (cw_ref[0:1, :] * _shift_rows(z, 1) + cw_ref[1:2, :] * z + cw_ref[2:3, :] * _shift_rows(z, -1))
    o_conv = ext[halo:halo + tm, 0:CONV_W] * conv[halo:halo + tm]

    u = ext[:, 2 * CONV_W:]
    a2 = u + _shift_rows(u, 1)
    a4 = _shift_rows(a2, -1) + _shift_rows(a2, 1)
    a8 = _shift_rows(a4, -2) + _shift_rows(a4, 2)
    a16 = _shift_rows(a8, -4) + _shift_rows(a8, 4)
    lane = lax.broadcasted_iota(I32, (tm, POOL_W), 1)
    grp = lane // POOL_GC
    wsum = jnp.where(grp == 0, a2[halo:halo + tm],
                     jnp.where(grp == 1, a4[halo:halo + tm],
                               jnp.where(grp == 2, a8[halo:halo + tm], a16[halo:halo + tm])))
    half = jnp.where(grp == 0, 1, jnp.where(grp == 1, 2, jnp.where(grp == 2, 4, 8)))
    row = lax.broadcasted_iota(I32, (tm, POOL_W), 0)
    pos = jnp.where(j == 0, row, (j - 1) * tm + row)
    n_seq = jnp.where(j == 0, CTX_LEN, seq_len)
    cnt = jnp.minimum(pos + half, n_seq) - jnp.maximum(pos - half, 0)
    dlt = wsum / cnt.astype(F32) - u[halo:halo + tm]
    o_pool = _dot(dlt.astype(BF16), pw_ref[...]) * ps_ref[...]

    mix = _dot(o_conv.astype(BF16), wo_ref[ATTN_W:ATTN_W + CONV_W, :])
    mix = mix + _dot(o_pool.astype(BF16), wo_ref[ATTN_W + CONV_W:, :])
    for p in range(N_PAIRS):
        mix = mix + _dot(oa_ref[0, p], wo_ref[p * LANES:(p + 1) * LANES, :])
    x = x_ref[0] + mod_ref[0, 2:3, :] * mix
    xo_ref[0] = x

    h2 = _rms_mod(x, g2_ref[...], mod_ref[0, 3:4, :], mod_ref[0, 4:5, :])
    hb = h2.astype(BF16).astype(F32)
    lo_bits = lax.shift_right_logical(pltpu.bitcast(hb[:, :d // 2], U32), jnp.uint32(16))
    hi_bits = pltpu.bitcast(hb[:, d // 2:], U32)
    hu_ref[0] = lo_bits | hi_bits

    h_hi, h_lo = _split_bf16(h2)
    logits = (_dot(h_hi, wrh_ref[...]) + _dot(h_lo, wrh_ref[...]) + _dot(h_hi, wrl_ref[...])
              + br_ref[...])
    lanef = lax.broadcasted_iota(I32, (tm, LANES), 1).astype(F32)
    work = logits
    tops = []
    for k in range(TOP_K):
        mk = jnp.max(work, axis=-1, keepdims=True)
        ik = jnp.min(jnp.where(work == mk, lanef, float(LANES)), axis=-1, keepdims=True)
        te_ref[0, :, k:k + 1] = ik.astype(I32)
        work = jnp.where(lanef == ik, -jnp.inf, work)
        tops.append(mk)
    es = [jnp.exp(mk - tops[0]) for mk in tops]
    den = es[0] + es[1] + es[2] + es[3]
    for k in range(TOP_K):
        tw_ref[0, :, k:k + 1] = es[k] / den


def _out_call(s, oa, cp, cw, pw, ps, wo, mod, g2, wrh, wrl, br, seq_len):
    b, la, d = s.shape
    nt = la // ROW_TILE
    hb = ROW_TILE // POOL_HALO
    nhb = la // POOL_HALO

    def mod_idx(i, j):
        return (jnp.where(j == 0, b, i), 0, 0)

    const2 = lambda i, j: (0, 0)
    return pl.pallas_call(
        functools.partial(_out_kernel, seq_len=seq_len),
        grid=(b, nt),
        in_specs=[
            pl.BlockSpec((1, ROW_TILE, d), lambda i, j: (i, j, 0)),
            pl.BlockSpec((1, N_PAIRS, ROW_TILE, LANES), lambda i, j: (i, 0, j, 0)),
            pl.BlockSpec((1, ROW_TILE, 3 * CONV_W), lambda i, j: (i, j, 0)),
            pl.BlockSpec((1, POOL_HALO, 3 * CONV_W), lambda i, j: (i, jnp.maximum(j * hb - 1, 0), 0)),
            pl.BlockSpec((1, POOL_HALO, 3 * CONV_W), lambda i, j: (i, jnp.minimum((j + 1) * hb, nhb - 1), 0)),
            pl.BlockSpec((3, CONV_W), const2),
            pl.BlockSpec((POOL_W, POOL_W), const2),
            pl.BlockSpec((1, POOL_W), const2),
            pl.BlockSpec((d, d), const2),
            pl.BlockSpec((1, N_MOD, d), mod_idx),
            pl.BlockSpec((1, d), const2),
            pl.BlockSpec((d, LANES), const2),
            pl.BlockSpec((d, LANES), const2),
            pl.BlockSpec((1, LANES), const2),
        ],
        out_specs=[
            pl.BlockSpec((1, ROW_TILE, d), lambda i, j: (i, j, 0)),
            pl.BlockSpec((1, ROW_TILE, d // 2), lambda i, j: (i, j, 0)),
            pl.BlockSpec((1, ROW_TILE, TOP_K), lambda i, j: (i, j, 0)),
            pl.BlockSpec((1, ROW_TILE, TOP_K), lambda i, j: (i, j, 0)),
        ],
        out_shape=[
            jax.ShapeDtypeStruct((b, la, d), F32),
            jax.ShapeDtypeStruct((b, la, d // 2), U32),
            jax.ShapeDtypeStruct((b, la, TOP_K), I32),
            jax.ShapeDtypeStruct((b, la, TOP_K), F32),
        ],
        compiler_params=pltpu.CompilerParams(vmem_limit_bytes=VMEM_LIMIT),
        name="outproj",
    )(s, oa, cp, cp, cp, cw, pw, ps, wo, mod, g2, wrh, wrl, br)


def _rank_kernel(te_ref, tri_ref, dest_ref, cnt_ref, carry):
    ph = pl.program_id(0)
    i = pl.program_id(1)
    tr = te_ref.shape[0]
    lane1 = lax.broadcasted_iota(I32, (SUBLANES, LANES), 1)

    @pl.when((ph == 0) & (i == 0))
    def _():
        carry[...] = jnp.zeros_like(carry)

    @pl.when((ph == 1) & (i == 0))
    def _():
        cnt = carry[...]
        cnt_ref[...] = cnt.astype(I32)
        padded = jnp.ceil(cnt * (1.0 / EXPERT_ROWS)) * EXPERT_ROWS
        incl = padded
        for sh in (1, 2, 4, 8, 16):
            incl = incl + jnp.where(lane1 >= sh, pltpu.roll(incl, sh, axis=1), 0.0)
        carry[...] = incl - padded

    e = te_ref[...]
    lane = lax.broadcasted_iota(I32, (tr, LANES), 1)
    ohs = [(e[:, k:k + 1] == lane).astype(F32) for k in range(TOP_K)]
    m = ohs[0] + ohs[1] + ohs[2] + ohs[3]
    base = _dot(tri_ref[...], m.astype(BF16)) + carry[0:1, :]
    for k in range(TOP_K):
        dest_ref[:, k:k + 1] = jnp.sum(ohs[k] * base, axis=-1, keepdims=True).astype(I32)
    carry[...] = carry[...] + jnp.sum(m, axis=0, keepdims=True)


def _rank_call(te):
    t = te.shape[0]
    nt = t // RANK_TILE
    r = lax.broadcasted_iota(I32, (RANK_TILE, RANK_TILE), 0)
    c = lax.broadcasted_iota(I32, (RANK_TILE, RANK_TILE), 1)
    tri = (c < r).astype(BF16)
    return pl.pallas_call(
        _rank_kernel,
        grid=(2, nt),
        in_specs=[
            pl.BlockSpec((RANK_TILE, TOP_K), lambda ph, i: (i, 0)),
            pl.BlockSpec((RANK_TILE, RANK_TILE), lambda ph, i: (0, 0)),
        ],
        out_specs=[
            pl.BlockSpec((RANK_TILE, TOP_K), lambda ph, i: (ph * i, 0)),
            pl.BlockSpec((SUBLANES, LANES), lambda ph, i: (0, 0)),
        ],
        out_shape=[
            jax.ShapeDtypeStruct((t, TOP_K), I32),
            jax.ShapeDtypeStruct((SUBLANES, LANES), I32),
        ],
        scratch_shapes=[pltpu.VMEM((SUBLANES, LANES), F32)],
        compiler_params=pltpu.CompilerParams(dimension_semantics=("arbitrary", "arbitrary")),
        name="rank",
    )(te, tri)


def _disp_kernel(dest_ref, hu_ref, xs_in_ref, xs_ref, sem):
    del xs_in_ref
    t0 = pl.program_id(0) * ROW_TILE

    def row_copy(r, k):
        return pltpu.make_async_copy(hu_ref.at[pl.ds(t0 + r, 1)],
                                     xs_ref.at[pl.ds(dest_ref[r * TOP_K + k], 1)], sem)

    def start(r, _):
        for k in range(TOP_K):
            row_copy(r, k).start()
        return 0

    def wait(r, _):
        for k in range(TOP_K):
            row_copy(r, k).wait()
        return 0

    lax.fori_loop(0, ROW_TILE, start, 0)
    lax.fori_loop(0, ROW_TILE, wait, 0)


def _disp_call(dest_flat, hu, n_rows):
    t, w = hu.shape
    xs0 = jnp.zeros((n_rows, w), U32)
    return pl.pallas_call(
        _disp_kernel,
        grid=(t // ROW_TILE,),
        in_specs=[
            pl.BlockSpec((ROW_TILE * TOP_K,), lambda i: (i,), memory_space=pltpu.SMEM),
            pl.BlockSpec(memory_space=pl.ANY),
            pl.BlockSpec(memory_space=pl.ANY),
        ],
        out_specs=pl.BlockSpec(memory_space=pl.ANY),
        out_shape=jax.ShapeDtypeStruct((n_rows, w), U32),
        scratch_shapes=[pltpu.SemaphoreType.DMA(())],
        input_output_aliases={2: 0},
        compiler_params=pltpu.CompilerParams(dimension_semantics=("arbitrary",)),
        name="dispatch",
    )(dest_flat, hu, xs0)


def _exp_kernel(be_ref, nu_ref, xs_ref, wgu_ref, bgu_ref, wdn_ref, bdn_ref, ys_ref):
    del be_ref
    i = pl.program_id(0)
    ff = wdn_ref.shape[1]
    half = xs_ref.shape[1]

    @pl.when(i < nu_ref[0])
    def _():
        xu = xs_ref[...]
        x_lo = pltpu.bitcast(lax.shift_left(xu, jnp.uint32(16)), F32).astype(BF16)
        x_hi = pltpu.bitcast(xu & jnp.uint32(0xFFFF0000), F32).astype(BF16)
        gu = _dot(x_lo, wgu_ref[0, :half, :]) + _dot(x_hi, wgu_ref[0, half:, :]) + bgu_ref[0]
        glu = jnp.minimum(gu[:, :ff], SWIGLU_LIMIT)
        lin = jnp.clip(gu[:, ff:], -SWIGLU_LIMIT, SWIGLU_LIMIT)
        act = glu * jax.nn.sigmoid(SWIGLU_ALPHA * glu) * (lin + 1.0)
        ys_ref[...] = _dot(act.astype(BF16), wdn_ref[0]) + bdn_ref[0]

    @pl.when(i >= nu_ref[0])
    def _():
        ys_ref[...] = jnp.zeros_like(ys_ref)


def _exp_call(block_e, n_used, xs, wgu, bgu, wdn, bdn):
    n_rows, half = xs.shape
    _, d, ff2 = wgu.shape
    ff = ff2 // 2
    nb = n_rows // EXPERT_ROWS
    grid_spec = pltpu.PrefetchScalarGridSpec(
        num_scalar_prefetch=2,
        grid=(nb,),
        in_specs=[
            pl.BlockSpec((EXPERT_ROWS, half), lambda i, be, nu: (i, 0)),
            pl.BlockSpec((1, d, ff2), lambda i, be, nu: (be[i], 0, 0)),
            pl.BlockSpec((1, 1, ff2), lambda i, be, nu: (be[i], 0, 0)),
            pl.BlockSpec((1, ff, d), lambda i, be, nu: (be[i], 0, 0)),
            pl.BlockSpec((1, 1, d), lambda i, be, nu: (be[i], 0, 0)),
        ],
        out_specs=pl.BlockSpec((EXPERT_ROWS, d), lambda i, be, nu: (i, 0)),
    )
    return pl.pallas_call(
        _exp_kernel,
        grid_spec=grid_spec,
        out_shape=jax.ShapeDtypeStruct((n_rows, d), F32),
        compiler_params=pltpu.CompilerParams(dimension_semantics=("arbitrary",),
                                             vmem_limit_bytes=VMEM_LIMIT),
        name="experts",
    )(block_e, n_used, xs, wgu, bgu.reshape(-1, 1, ff2), wdn, bdn.reshape(-1, 1, d))


def _comb_kernel(dest_ref, x_ref, mod_ref, tw_ref, ys_ref, o_ref, buf, sem):
    def row_copy(r, k):
        return pltpu.make_async_copy(ys_ref.at[pl.ds(dest_ref[r * TOP_K + k], 1)],
                                     buf.at[k, pl.ds(r, 1)], sem)

    def start(r, _):
        for k in range(TOP_K):
            row_copy(r, k).start()
        return 0

    def wait(r, _):
        for k in range(TOP_K):
            row_copy(r, k).wait()
        return 0

    lax.fori_loop(0, ROW_TILE, start, 0)
    lax.fori_loop(0, ROW_TILE, wait, 0)
    acc = buf[0] * tw_ref[0, :, 0:1]
    for k in range(1, TOP_K):
        acc = acc + buf[k] * tw_ref[0, :, k:k + 1]
    o_ref[0] = x_ref[0] + mod_ref[0, 5:6, :] * acc


def _comb_call(dest_flat, s, mod, tw, ys, latent_only):
    b, la, d = s.shape
    nt = la // ROW_TILE
    off = 1 if latent_only else 0

    def mod_idx(i, j):
        return (jnp.where(j + off == 0, b, i), 0, 0)

    return pl.pallas_call(
        _comb_kernel,
        grid=(b, nt - off),
        in_specs=[
            pl.BlockSpec((ROW_TILE * TOP_K,), lambda i, j: (i * nt + j + off,), memory_space=pltpu.SMEM),
            pl.BlockSpec((1, ROW_TILE, d), lambda i, j: (i, j + off, 0)),
            pl.BlockSpec((1, N_MOD, d), mod_idx),
            pl.BlockSpec((1, ROW_TILE, TOP_K), lambda i, j: (i, j + off, 0)),
            pl.BlockSpec(memory_space=pl.ANY),
        ],
        out_specs=pl.BlockSpec((1, ROW_TILE, d), lambda i, j: (i, j, 0)),
        out_shape=jax.ShapeDtypeStruct((b, la - off * ROW_TILE, d), F32),
        scratch_shapes=[pltpu.VMEM((TOP_K, ROW_TILE, d), F32), pltpu.SemaphoreType.DMA(())],
        compiler_params=pltpu.CompilerParams(dimension_semantics=("arbitrary", "arbitrary"),
                                             vmem_limit_bytes=VMEM_LIMIT),
        name="combine",
    )(dest_flat, s, mod, tw, ys)


def _rope_tables(seq_len):
    rows = seq_len // GRID_W
    row = jnp.broadcast_to(jnp.arange(rows, dtype=F32)[:, None], (rows, GRID_W)).reshape(seq_len)
    col = jnp.broadcast_to(jnp.arange(GRID_W, dtype=F32)[None, :], (rows, GRID_W)).reshape(seq_len)
    inv = ROPE_BASE ** (-jnp.arange(ROPE_PAIRS, dtype=F32) / ROPE_PAIRS)
    ang = jnp.stack([row, col], axis=0)[:, None, :] * inv[None, :, None]
    cos_l = jnp.cos(ang).reshape(2 * ROPE_PAIRS, seq_len)
    sin_l = jnp.sin(ang).reshape(2 * ROPE_PAIRS, seq_len)
    cos_t = jnp.concatenate([jnp.ones((2 * ROPE_PAIRS, CTX_LEN), F32), cos_l], axis=1)
    sin_t = jnp.concatenate([jnp.zeros((2 * ROPE_PAIRS, CTX_LEN), F32), sin_l], axis=1)
    return cos_t, sin_t


def kernel(x, c, ctx, c_ctx, w_mod, b_mod, norm1_g, norm2_g, w_in, q_norm_g, k_norm_g,
           lambda_q1, lambda_k1, lambda_q2, lambda_k2, subln_g, conv_w, pool_w, pool_scale, w_out,
           router_w, router_b, w_gate_up, b_gate_up, w_down, b_down):
    b, seq_len, d = x.shape
    depth = w_mod.shape[0]
    la = CTX_LEN + seq_len
    t_all = b * la

    rows = -(-(b + 1) // SUBLANES) * SUBLANES
    c_all = jnp.zeros((rows, d), F32).at[:b].set(c).at[b].set(c_ctx)
    mod = _mod_call(c_all, w_mod, b_mod).reshape(depth, rows, N_MOD, d)

    cos_t, sin_t = _rope_tables(seq_len)
    s = jnp.concatenate([ctx, x], axis=1)

    n_rows = -(-(t_all * TOP_K + N_EXPERTS * (EXPERT_ROWS - 1)) // EXPERT_ROWS) * EXPERT_ROWS
    nb = n_rows // EXPERT_ROWS

    for l in range(depth):
        last = l + 1 == depth
        lam_init = 0.8 - 0.6 * math.exp(-0.3 * l)
        wl = w_in[l]
        wm = wl[:, 2 * ATTN_W:].astype(BF16)
        wqk = wl[:, :2 * ATTN_W].T.astype(BF16)
        q_gain = jnp.tile(q_norm_g[l], ATTN_W // QK_DIM) * (LOG2E * QK_DIM ** -0.5)
        k_gain = jnp.tile(k_norm_g[l], ATTN_W // QK_DIM)
        gain = jnp.broadcast_to(jnp.concatenate([q_gain, k_gain])[:, None], (2 * ATTN_W, ROW_TILE))
        q, kt, v, cp = _in_call(s, mod[l], norm1_g[l][None], wm, wqk, gain, cos_t, sin_t)

        lam_p = jnp.zeros((SUBLANES, LANES), F32)
        lam_p = lam_p.at[0, :QK_DIM].set(lambda_q1[l]).at[1, :QK_DIM].set(lambda_k1[l])
        lam_p = lam_p.at[2, :QK_DIM].set(lambda_q2[l]).at[3, :QK_DIM].set(lambda_k2[l])
        sg = jnp.tile(subln_g[l], 2)[None]
        oa = _attn_call(lam_p, q, kt, v, sg, lam_init)

        pw = jax.scipy.linalg.block_diag(*[pool_w[l, g] for g in range(len(POOL_WINDOWS))]).astype(BF16)
        wr = jnp.zeros((d, LANES), F32).at[:, :N_EXPERTS].set(router_w[l])
        wrh, wrl = _split_bf16(wr)
        br = jnp.full((1, LANES), NEG_BIG, F32).at[0, :N_EXPERTS].set(router_b[l])
        s, hu, te, tw = _out_call(s, oa, cp, conv_w[l], pw, pool_scale[l][None], w_out[l].astype(BF16),
                                  mod[l], norm2_g[l][None], wrh, wrl, br, seq_len)

        dest, counts = _rank_call(te.reshape(t_all, TOP_K))
        cnt = counts[0, :N_EXPERTS]
        pad_end = jnp.cumsum((cnt + EXPERT_ROWS - 1) // EXPERT_ROWS * EXPERT_ROWS)
        n_used = (pad_end[-1] // EXPERT_ROWS).astype(I32)
        blk = jnp.arange(nb, dtype=I32)
        block_e = jnp.searchsorted(pad_end, jnp.minimum(blk, n_used - 1) * EXPERT_ROWS, side='right')
        block_e = jnp.minimum(block_e, N_EXPERTS - 1).astype(I32)
        dest_flat = dest.reshape(t_all * TOP_K)

        xs = _disp_call(dest_flat, hu.reshape(t_all, d // 2), n_rows)
        ys = _exp_call(block_e, n_used.reshape(1), xs, w_gate_up[l].astype(BF16), b_gate_up[l],
                       w_down[l].astype(BF16), b_down[l])
        s = _comb_call(dest_flat, s, mod[l], tw, ys, latent_only=last)
    return s
```

```python
import functools
import math

import jax
import jax.numpy as jnp
from jax import lax
from jax.experimental import pallas as pl
from jax.experimental.pallas import tpu as pltpu

F32 = jnp.float32
BF16 = jnp.bfloat16
I32 = jnp.int32
U32 = jnp.uint32

GRID_W = 64
CTX_LEN = 256
HEAD_DIM = 64
QK_DIM = 32
N_HEADS = 8
ATTN_W = 512
CONV_W = 256
POOL_W = 256
POOL_WINDOWS = (2, 4, 8, 16)
POOL_GC = 64
ROPE_BASE = 10000.0
ROPE_PAIRS = 8
N_EXPERTS = 32
TOP_K = 4
N_MOD = 6
EPS = 1e-6
SWIGLU_ALPHA = 1.702
SWIGLU_LIMIT = 7.0
LOG2E = 1.4426950408889634

LANES = 128
SUBLANES = 8
ROW_TILE = 256
N_PAIRS = N_HEADS // 2
POOL_HALO = 8
EXPERT_ROWS = 256
RANK_TILE = 512
KEY_UNROLL = 4
VMEM_LIMIT = 48 * 1024 * 1024
ATTN_VMEM_LIMIT = 56 * 1024 * 1024
NEG_BIG = -1e30


def _split_bf16(a):
    hi = a.astype(BF16)
    lo = (a - hi.astype(F32)).astype(BF16)
    return hi, lo


def _dot(a, b):
    return jnp.dot(a, b, preferred_element_type=F32)


def _mod_kernel(c_ref, w_ref, b_ref, o_ref):
    cv = c_ref[...]
    s = cv * jax.nn.sigmoid(cv)
    s_hi, s_lo = _split_bf16(s)
    w_hi, w_lo = _split_bf16(w_ref[0])
    o_ref[0] = _dot(s_hi, w_hi) + _dot(s_lo, w_hi) + _dot(s_hi, w_lo) + b_ref[0]


def _mod_call(c_all, w_mod, b_mod):
    depth, d, n = w_mod.shape
    r = c_all.shape[0]
    tn = 1536
    return pl.pallas_call(
        _mod_kernel,
        grid=(depth, n // tn),
        in_specs=[
            pl.BlockSpec((r, d), lambda l, j: (0, 0)),
            pl.BlockSpec((1, d, tn), lambda l, j: (l, 0, j)),
            pl.BlockSpec((1, 1, tn), lambda l, j: (l, 0, j)),
        ],
        out_specs=pl.BlockSpec((1, r, tn), lambda l, j: (l, 0, j)),
        out_shape=jax.ShapeDtypeStruct((depth, r, n), F32),
        compiler_params=pltpu.CompilerParams(vmem_limit_bytes=VMEM_LIMIT),
        name="mod",
    )(c_all, w_mod, b_mod.reshape(depth, 1, n))


def _rms_mod(x, g, shift, scale):
    y = x * lax.rsqrt(jnp.mean(x * x, axis=-1, keepdims=True) + EPS) * g
    return y * (1.0 + scale) + shift


def _in_kernel(x_ref, mod_ref, g1_ref, wm_ref, wqk_ref, gain_ref, cos_ref, sin_ref,
               q_ref, kt_ref, v_ref, cp_ref):
    tm = x_ref.shape[1]
    h = _rms_mod(x_ref[0], g1_ref[...], mod_ref[0, 0:1, :], mod_ref[0, 1:2, :])
    pm = _dot(h.astype(BF16), wm_ref[...])
    for p in range(N_PAIRS):
        v_ref[0, p] = pm[:, p * LANES:(p + 1) * LANES].astype(BF16)
    o = ATTN_W
    cp_ref[0, :, 0:CONV_W] = pm[:, o:o + CONV_W]
    cp_ref[0, :, CONV_W:2 * CONV_W] = pm[:, o + CONV_W:o + 2 * CONV_W] * pm[:, o + 2 * CONV_W:o + 3 * CONV_W]
    cp_ref[0, :, 2 * CONV_W:] = pm[:, o + 3 * CONV_W:]

    ht = h.T.astype(BF16)
    qkt = _dot(wqk_ref[...], ht)
    ng = 2 * ATTN_W // QK_DIM
    t = qkt.reshape(ng, 4, ROPE_PAIRS, tm)
    ss = jnp.sum(jnp.sum(t * t, axis=2, keepdims=True), axis=1, keepdims=True)
    tn = t * lax.rsqrt(ss * (1.0 / QK_DIM) + EPS) * gain_ref[...].reshape(ng, 4, ROPE_PAIRS, tm)
    cs = cos_ref[...].reshape(2, ROPE_PAIRS, tm)
    sn = sin_ref[...].reshape(2, ROPE_PAIRS, tm)
    parts = []
    for a in range(2):
        t1 = tn[:, 2 * a]
        t2 = tn[:, 2 * a + 1]
        parts.append(t1 * cs[a] - t2 * sn[a])
        parts.append(t2 * cs[a] + t1 * sn[a])
    rot = jnp.stack(parts, axis=1).reshape(2 * ATTN_W, tm)
    qt = rot[:ATTN_W].T
    for p in range(N_PAIRS):
        q_ref[0, p] = qt[:, p * LANES:(p + 1) * LANES].astype(BF16)
        kt_ref[0, 0, p] = rot[ATTN_W + p * LANES:ATTN_W + (p + 1) * LANES].astype(BF16)


def _in_call(s, mod, g1, wm, wqk, gain, cos_t, sin_t):
    b, la, d = s.shape
    nt = la // ROW_TILE
    nmain = wm.shape[1]

    def mod_idx(i, j):
        return (jnp.where(j == 0, b, i), 0, 0)

    return pl.pallas_call(
        _in_kernel,
        grid=(b, nt),
        in_specs=[
            pl.BlockSpec((1, ROW_TILE, d), lambda i, j: (i, j, 0)),
            pl.BlockSpec((1, N_MOD, d), mod_idx),
            pl.BlockSpec((1, d), lambda i, j: (0, 0)),
            pl.BlockSpec((d, nmain), lambda i, j: (0, 0)),
            pl.BlockSpec((2 * ATTN_W, d), lambda i, j: (0, 0)),
            pl.BlockSpec((2 * ATTN_W, ROW_TILE), lambda i, j: (0, 0)),
            pl.BlockSpec((2 * ROPE_PAIRS, ROW_TILE), lambda i, j: (0, j)),
            pl.BlockSpec((2 * ROPE_PAIRS, ROW_TILE), lambda i, j: (0, j)),
        ],
        out_specs=[
            pl.BlockSpec((1, N_PAIRS, ROW_TILE, LANES), lambda i, j: (i, 0, j, 0)),
            pl.BlockSpec((1, 1, N_PAIRS, LANES, ROW_TILE), lambda i, j: (i, j, 0, 0, 0)),
            pl.BlockSpec((1, N_PAIRS, ROW_TILE, LANES), lambda i, j: (i, 0, j, 0)),
            pl.BlockSpec((1, ROW_TILE, 3 * CONV_W), lambda i, j: (i, j, 0)),
        ],
        out_shape=[
            jax.ShapeDtypeStruct((b, N_PAIRS, la, LANES), BF16),
            jax.ShapeDtypeStruct((b, nt, N_PAIRS, LANES, ROW_TILE), BF16),
            jax.ShapeDtypeStruct((b, N_PAIRS, la, LANES), BF16),
            jax.ShapeDtypeStruct((b, la, 3 * CONV_W), F32),
        ],
        compiler_params=pltpu.CompilerParams(vmem_limit_bytes=VMEM_LIMIT),
        name="inproj",
    )(s, mod, g1, wm, wqk, gain, cos_t, sin_t)


def _attn_kernel(lam_ref, q_ref, kt_ref, v_ref, sg_ref, o_ref, s_scr, m_scr, l_scr, acc_scr, *, lam_init):
    tq = q_ref.shape[2]
    nkc_all = kt_ref.shape[1]
    j = pl.program_id(1)
    n_groups = jnp.where(j == 0, 0, (nkc_all - 1) // KEY_UNROLL)
    lp = lam_ref[...]
    lam = (jnp.exp(jnp.sum(lp[0:1] * lp[1:2], axis=-1, keepdims=True))
           - jnp.exp(jnp.sum(lp[2:3] * lp[3:4], axis=-1, keepdims=True)) + lam_init)
    lane = lax.broadcasted_iota(I32, (1, LANES), 1)
    low = lane < HEAD_DIM

    for p in range(N_PAIRS):
        qp = q_ref[0, p]
        qs = jnp.concatenate(
            [jnp.where((lane >= g * QK_DIM) & (lane < (g + 1) * QK_DIM), qp, jnp.zeros_like(qp))
             for g in range(4)], axis=0)

        def scores(kc, qs=qs, p=p):
            sc = _dot(qs, kt_ref[0, kc, p])
            s_scr[kc] = sc
            return jnp.maximum(sc[:, :LANES], sc[:, LANES:])

        def sweep_max(g, _):
            ms = [scores(1 + g * KEY_UNROLL + u) for u in range(KEY_UNROLL)]
            while len(ms) > 1:
                ms = [jnp.maximum(a, b) for a, b in zip(ms[::2], ms[1::2])]
            m_scr[...] = jnp.maximum(m_scr[...], ms[0])
            return 0

        m_scr[...] = scores(0)
        lax.fori_loop(0, n_groups, sweep_max, 0)
        m_scr[...] = jnp.broadcast_to(jnp.max(m_scr[...], axis=-1, keepdims=True), m_scr.shape)

        def exp_pv(kc, p=p):
            sc = s_scr[kc]
            mb = m_scr[...]
            e0 = jnp.exp2(sc[:, :LANES] - mb)
            e1 = jnp.exp2(sc[:, LANES:] - mb)
            r0 = pl.multiple_of(kc * ROW_TILE, ROW_TILE)
            e = jnp.concatenate([e0.astype(BF16), e1.astype(BF16)], axis=1)
            return e0 + e1, _dot(e, v_ref[0, p, pl.ds(r0, ROW_TILE), :])

        def sweep_pv(g, _):
            parts = [exp_pv(1 + g * KEY_UNROLL + u) for u in range(KEY_UNROLL)]
            ls = [a for a, _ in parts]
            pv = [b for _, b in parts]
            while len(ls) > 1:
                ls = [a + b for a, b in zip(ls[::2], ls[1::2])]
                pv = [a + b for a, b in zip(pv[::2], pv[1::2])]
            l_scr[...] = l_scr[...] + ls[0]
            acc_scr[...] = acc_scr[...] + pv[0]
            return 0

        l_scr[...], acc_scr[...] = exp_pv(0)
        lax.fori_loop(0, n_groups, sweep_pv, 0)
        on = acc_scr[...] / jnp.sum(l_scr[...], axis=-1, keepdims=True)
        heads = [on[(2 * hh) * tq:(2 * hh + 1) * tq] - lam * on[(2 * hh + 1) * tq:(2 * hh + 2) * tq]
                 for hh in range(2)]
        o = jnp.where(low, heads[0], heads[1])
        o2 = o * o
        ss_lo = jnp.sum(jnp.where(low, o2, 0.0), axis=-1, keepdims=True)
        ss_hi = jnp.sum(jnp.where(low, 0.0, o2), axis=-1, keepdims=True)
        inv = lax.rsqrt(jnp.where(low, ss_lo, ss_hi) * (1.0 / HEAD_DIM) + EPS)
        o_ref[0, p] = (o * inv * sg_ref[...] * (1.0 - lam_init)).astype(BF16)


def _attn_call(lam_p, q, kt, v, sg, lam_init):
    b, _, la, _ = q.shape
    nt = la // ROW_TILE
    assert (nt - 1) % KEY_UNROLL == 0
    return pl.pallas_call(
        functools.partial(_attn_kernel, lam_init=lam_init),
        grid=(b, nt),
        in_specs=[
            pl.BlockSpec((SUBLANES, LANES), lambda i, j: (0, 0)),
            pl.BlockSpec((1, N_PAIRS, ROW_TILE, LANES), lambda i, j: (i, 0, j, 0)),
            pl.BlockSpec((1, nt, N_PAIRS, LANES, ROW_TILE), lambda i, j: (i, 0, 0, 0, 0)),
            pl.BlockSpec((1, N_PAIRS, la, LANES), lambda i, j: (i, 0, 0, 0)),
            pl.BlockSpec((1, LANES), lambda i, j: (0, 0)),
        ],
        out_specs=pl.BlockSpec((1, N_PAIRS, ROW_TILE, LANES), lambda i, j: (i, 0, j, 0)),
        out_shape=jax.ShapeDtypeStruct((b, N_PAIRS, la, LANES), BF16),
        scratch_shapes=[pltpu.VMEM((nt, 4 * ROW_TILE, ROW_TILE), F32),
                        pltpu.VMEM((4 * ROW_TILE, LANES), F32),
                        pltpu.VMEM((4 * ROW_TILE, LANES), F32),
                        pltpu.VMEM((4 * ROW_TILE, LANES), F32)],
        compiler_params=pltpu.CompilerParams(vmem_limit_bytes=ATTN_VMEM_LIMIT),
        name="attn",
    )(lam_p, q, kt, v, sg)


def _shift_rows(a, k):
    return pltpu.roll(a, k % a.shape[0], axis=0)


def _out_kernel(x_ref, oa_ref, cp_ref, cpp_ref, cpn_ref, cw_ref, pw_ref, ps_ref, wo_ref,
                mod_ref, g2_ref, wrh_ref, wrl_ref, br_ref,
                xo_ref, hu_ref, te_ref, tw_ref, *, seq_len):
    tm = x_ref.shape[1]
    d = x_ref.shape[2]
    j = pl.program_id(1)
    nt = pl.num_programs(1)
    halo = POOL_HALO
    has_prev = j >= 2
    has_next = (j >= 1) & (j < nt - 1)
    prev = jnp.where(has_prev, cpp_ref[0], 0.0)
    nxt = jnp.where(has_next, cpn_ref[0], 0.0)
    ext = jnp.concatenate([prev, cp_ref[0], nxt], axis=0)

    z = ext[:, CONV_W:2 * CONV_W]
    conv = (cw_ref[0:1, :] * _shift_rows(z, 1) + cw_ref[1:2, :] * z + cw_ref[2:3, :] * _shift_rows(z, -1))
    o_conv = ext[halo:halo + tm, 0:CONV_W] * conv[halo:halo + tm]

    u = ext[:, 2 * CONV_W:]
    a2 = u + _shift_rows(u, 1)
    a4 = _shift_rows(a2, -1) + _shift_rows(a2, 1)
    a8 = _shift_rows(a4, -2) + _shift_rows(a4, 2)
    a16 = _shift_rows(a8, -4) + _shift_rows(a8, 4)
    lane = lax.broadcasted_iota(I32, (tm, POOL_W), 1)
    grp = lane // POOL_GC
    wsum = jnp.where(grp == 0, a2[halo:halo + tm],
                     jnp.where(grp == 1, a4[halo:halo + tm],
                               jnp.where(grp == 2, a8[halo:halo + tm], a16[halo:halo + tm])))
    half = jnp.where(grp == 0, 1, jnp.where(grp == 1, 2, jnp.where(grp == 2, 4, 8)))
    row = lax.broadcasted_iota(I32, (tm, POOL_W), 0)
    pos = jnp.where(j == 0, row, (j - 1) * tm + row)
    n_seq = jnp.where(j == 0, CTX_LEN, seq_len)
    cnt = jnp.minimum(pos + half, n_seq) - jnp.maximum(pos - half, 0)
    dlt = wsum / cnt.astype(F32) - u[halo:halo + tm]
    o_pool = _dot(dlt.astype(BF16), pw_ref[...]) * ps_ref[...]

    mix = _dot(o_conv.astype(BF16), wo_ref[ATTN_W:ATTN_W + CONV_W, :])
    mix = mix + _dot(o_pool.astype(BF16), wo_ref[ATTN_W + CONV_W:, :])
    for p in range(N_PAIRS):
        mix = mix + _dot(oa_ref[0, p], wo_ref[p * LANES:(p + 1) * LANES, :])
    x = x_ref[0] + mod_ref[0, 2:3, :] * mix
    xo_ref[0] = x

    h2 = _rms_mod(x, g2_ref[...], mod_ref[0, 3:4, :], mod_ref[0, 4:5, :])
    hb = h2.astype(BF16).astype(F32)
    lo_bits = lax.shift_right_logical(pltpu.bitcast(hb[:, :d // 2], U32), jnp.uint32(16))
    hi_bits = pltpu.bitcast(hb[:, d // 2:], U32)
    hu_ref[0] = lo_bits | hi_bits

    h_hi, h_lo = _split_bf16(h2)
    logits = (_dot(h_hi, wrh_ref[...]) + _dot(h_lo, wrh_ref[...]) + _dot(h_hi, wrl_ref[...])
              + br_ref[...])
    lanef = lax.broadcasted_iota(I32, (tm, LANES), 1).astype(F32)
    work = logits
    tops = []
    for k in range(TOP_K):
        mk = jnp.max(work, axis=-1, keepdims=True)
        ik = jnp.min(jnp.where(work == mk, lanef, float(LANES)), axis=-1, keepdims=True)
        te_ref[0, :, k:k + 1] = ik.astype(I32)
        work = jnp.where(lanef == ik, -jnp.inf, work)
        tops.append(mk)
    es = [jnp.exp(mk - tops[0]) for mk in tops]
    den = es[0] + es[1] + es[2] + es[3]
    for k in range(TOP_K):
        tw_ref[0, :, k:k + 1] = es[k] / den


def _out_call(s, oa, cp, cw, pw, ps, wo, mod, g2, wrh, wrl, br, seq_len):
    b, la, d = s.shape
    nt = la // ROW_TILE
    hb = ROW_TILE // POOL_HALO
    nhb = la // POOL_HALO

    def mod_idx(i, j):
        return (jnp.where(j == 0, b, i), 0, 0)

    const2 = lambda i, j: (0, 0)
    return pl.pallas_call(
        functools.partial(_out_kernel, seq_len=seq_len),
        grid=(b, nt),
        in_specs=[
            pl.BlockSpec((1, ROW_TILE, d), lambda i, j: (i, j, 0)),
            pl.BlockSpec((1, N_PAIRS, ROW_TILE, LANES), lambda i, j: (i, 0, j, 0)),
            pl.BlockSpec((1, ROW_TILE, 3 * CONV_W), lambda i, j: (i, j, 0)),
            pl.BlockSpec((1, POOL_HALO, 3 * CONV_W), lambda i, j: (i, jnp.maximum(j * hb - 1, 0), 0)),
            pl.BlockSpec((1, POOL_HALO, 3 * CONV_W), lambda i, j: (i, jnp.minimum((j + 1) * hb, nhb - 1), 0)),
            pl.BlockSpec((3, CONV_W), const2),
            pl.BlockSpec((POOL_W, POOL_W), const2),
            pl.BlockSpec((1, POOL_W), const2),
            pl.BlockSpec((d, d), const2),
            pl.BlockSpec((1, N_MOD, d), mod_idx),
            pl.BlockSpec((1, d), const2),
            pl.BlockSpec((d, LANES), const2),
            pl.BlockSpec((d, LANES), const2),
            pl.BlockSpec((1, LANES), const2),
        ],
        out_specs=[
            pl.BlockSpec((1, ROW_TILE, d), lambda i, j: (i, j, 0)),
            pl.BlockSpec((1, ROW_TILE, d // 2), lambda i, j: (i, j, 0)),
            pl.BlockSpec((1, ROW_TILE, TOP_K), lambda i, j: (i, j, 0)),
            pl.BlockSpec((1, ROW_TILE, TOP_K), lambda i, j: (i, j, 0)),
        ],
        out_shape=[
            jax.ShapeDtypeStruct((b, la, d), F32),
            jax.ShapeDtypeStruct((b, la, d // 2), U32),
            jax.ShapeDtypeStruct((b, la, TOP_K), I32),
            jax.ShapeDtypeStruct((b, la, TOP_K), F32),
        ],
        compiler_params=pltpu.CompilerParams(vmem_limit_bytes=VMEM_LIMIT),
        name="outproj",
    )(s, oa, cp, cp, cp, cw, pw, ps, wo, mod, g2, wrh, wrl, br)


def _rank_kernel(te_ref, tri_ref, dest_ref, cnt_ref, carry):
    ph = pl.program_id(0)
    i = pl.program_id(1)
    tr = te_ref.shape[0]
    lane1 = lax.broadcasted_iota(I32, (SUBLANES, LANES), 1)

    @pl.when((ph == 0) & (i == 0))
    def _():
        carry[...] = jnp.zeros_like(carry)

    @pl.when((ph == 1) & (i == 0))
    def _():
        cnt = carry[...]
        cnt_ref[...] = cnt.astype(I32)
        padded = jnp.ceil(cnt * (1.0 / EXPERT_ROWS)) * EXPERT_ROWS
        incl = padded
        for sh in (1, 2, 4, 8, 16):
            incl = incl + jnp.where(lane1 >= sh, pltpu.roll(incl, sh, axis=1), 0.0)
        carry[...] = incl - padded

    e = te_ref[...]
    lane = lax.broadcasted_iota(I32, (tr, LANES), 1)
    ohs = [(e[:, k:k + 1] == lane).astype(F32) for k in range(TOP_K)]
    m = ohs[0] + ohs[1] + ohs[2] + ohs[3]
    base = _dot(tri_ref[...], m.astype(BF16)) + carry[0:1, :]
    for k in range(TOP_K):
        dest_ref[:, k:k + 1] = jnp.sum(ohs[k] * base, axis=-1, keepdims=True).astype(I32)
    carry[...] = carry[...] + jnp.sum(m, axis=0, keepdims=True)


def _rank_call(te):
    t = te.shape[0]
    nt = t // RANK_TILE
    r = lax.broadcasted_iota(I32, (RANK_TILE, RANK_TILE), 0)
    c = lax.broadcasted_iota(I32, (RANK_TILE, RANK_TILE), 1)
    tri = (c < r).astype(BF16)
    return pl.pallas_call(
        _rank_kernel,
        grid=(2, nt),
        in_specs=[
            pl.BlockSpec((RANK_TILE, TOP_K), lambda ph, i: (i, 0)),
            pl.BlockSpec((RANK_TILE, RANK_TILE), lambda ph, i: (0, 0)),
        ],
        out_specs=[
            pl.BlockSpec((RANK_TILE, TOP_K), lambda ph, i: (ph * i, 0)),
            pl.BlockSpec((SUBLANES, LANES), lambda ph, i: (0, 0)),
        ],
        out_shape=[
            jax.ShapeDtypeStruct((t, TOP_K), I32),
            jax.ShapeDtypeStruct((SUBLANES, LANES), I32),
        ],
        scratch_shapes=[pltpu.VMEM((SUBLANES, LANES), F32)],
        compiler_params=pltpu.CompilerParams(dimension_semantics=("arbitrary", "arbitrary")),
        name="rank",
    )(te, tri)


def _disp_kernel(dest_ref, hu_ref, xs_in_ref, xs_ref, sem):
    del xs_in_ref

    def row_copy(r, k):
        return pltpu.make_async_copy(hu_ref.at[pl.ds(r, 1)],
                                     xs_ref.at[pl.ds(dest_ref[r * TOP_K + k], 1)], sem)

    def start(r, _):
        for k in range(TOP_K):
            row_copy(r, k).start()
        return 0

    def wait(r, _):
        for k in range(TOP_K):
            row_copy(r, k).wait()
        return 0

    lax.fori_loop(0, ROW_TILE, start, 0)
    lax.fori_loop(0, ROW_TILE, wait, 0)


def _disp_call(dest_flat, hu, n_rows):
    t, w = hu.shape
    xs0 = jnp.zeros((n_rows, w), U32)
    return pl.pallas_call(
        _disp_kernel,
        grid=(t // ROW_TILE,),
        in_specs=[
            pl.BlockSpec((ROW_TILE * TOP_K,), lambda i: (i,), memory_space=pltpu.SMEM),
            pl.BlockSpec((ROW_TILE, w), lambda i: (i, 0)),
            pl.BlockSpec(memory_space=pl.ANY),
        ],
        out_specs=pl.BlockSpec(memory_space=pl.ANY),
        out_shape=jax.ShapeDtypeStruct((n_rows, w), U32),
        scratch_shapes=[pltpu.SemaphoreType.DMA(())],
        input_output_aliases={2: 0},
        compiler_params=pltpu.CompilerParams(dimension_semantics=("arbitrary",)),
        name="dispatch",
    )(dest_flat, hu, xs0)


def _exp_kernel(be_ref, nu_ref, xs_ref, wgu_ref, bgu_ref, wdn_ref, bdn_ref, ys_ref):
    del be_ref
    i = pl.program_id(0)
    ff = wdn_ref.shape[1]
    half = xs_ref.shape[1]

    @pl.when(i < nu_ref[0])
    def _():
        xu = xs_ref[...]
        x_lo = pltpu.bitcast(lax.shift_left(xu, jnp.uint32(16)), F32).astype(BF16)
        x_hi = pltpu.bitcast(xu & jnp.uint32(0xFFFF0000), F32).astype(BF16)
        gu = _dot(x_lo, wgu_ref[0, :half, :]) + _dot(x_hi, wgu_ref[0, half:, :]) + bgu_ref[0]
        glu = jnp.minimum(gu[:, :ff], SWIGLU_LIMIT)
        lin = jnp.clip(gu[:, ff:], -SWIGLU_LIMIT, SWIGLU_LIMIT)
        act = glu * jax.nn.sigmoid(SWIGLU_ALPHA * glu) * (lin + 1.0)
        ys_ref[...] = _dot(act.astype(BF16), wdn_ref[0]) + bdn_ref[0]

    @pl.when(i >= nu_ref[0])
    def _():
        ys_ref[...] = jnp.zeros_like(ys_ref)


def _exp_call(block_e, n_used, xs, wgu, bgu, wdn, bdn):
    n_rows, half = xs.shape
    _, d, ff2 = wgu.shape
    ff = ff2 // 2
    nb = n_rows // EXPERT_ROWS
    grid_spec = pltpu.PrefetchScalarGridSpec(
        num_scalar_prefetch=2,
        grid=(nb,),
        in_specs=[
            pl.BlockSpec((EXPERT_ROWS, half), lambda i, be, nu: (i, 0)),
            pl.BlockSpec((1, d, ff2), lambda i, be, nu: (be[i], 0, 0)),
            pl.BlockSpec((1, 1, ff2), lambda i, be, nu: (be[i], 0, 0)),
            pl.BlockSpec((1, ff, d), lambda i, be, nu: (be[i], 0, 0)),
            pl.BlockSpec((1, 1, d), lambda i, be, nu: (be[i], 0, 0)),
        ],
        out_specs=pl.BlockSpec((EXPERT_ROWS, d), lambda i, be, nu: (i, 0)),
    )
    return pl.pallas_call(
        _exp_kernel,
        grid_spec=grid_spec,
        out_shape=jax.ShapeDtypeStruct((n_rows, d), F32),
        compiler_params=pltpu.CompilerParams(dimension_semantics=("arbitrary",),
                                             vmem_limit_bytes=VMEM_LIMIT),
        name="experts",
    )(block_e, n_used, xs, wgu, bgu.reshape(-1, 1, ff2), wdn, bdn.reshape(-1, 1, d))


def _comb_kernel(dest_ref, x_ref, mod_ref, tw_ref, ys_ref, o_ref, buf, sem):
    def row_copy(r, k):
        return pltpu.make_async_copy(ys_ref.at[pl.ds(dest_ref[r * TOP_K + k], 1)],
                                     buf.at[k, pl.ds(r, 1)], sem)

    def start(r, _):
        for k in range(TOP_K):
            row_copy(r, k).start()
        return 0

    def wait(r, _):
        for k in range(TOP_K):
            row_copy(r, k).wait()
        return 0

    lax.fori_loop(0, ROW_TILE, start, 0)
    lax.fori_loop(0, ROW_TILE, wait, 0)
    acc = buf[0] * tw_ref[0, :, 0:1]
    for k in range(1, TOP_K):
        acc = acc + buf[k] * tw_ref[0, :, k:k + 1]
    o_ref[0] = x_ref[0] + mod_ref[0, 5:6, :] * acc


def _comb_call(dest_flat, s, mod, tw, ys, latent_only):
    b, la, d = s.shape
    nt = la // ROW_TILE
    off = 1 if latent_only else 0

    def mod_idx(i, j):
        return (jnp.where(j + off == 0, b, i), 0, 0)

    return pl.pallas_call(
        _comb_kernel,
        grid=(b, nt - off),
        in_specs=[
            pl.BlockSpec((ROW_TILE * TOP_K,), lambda i, j: (i * nt + j + off,), memory_space=pltpu.SMEM),
            pl.BlockSpec((1, ROW_TILE, d), lambda i, j: (i, j + off, 0)),
            pl.BlockSpec((1, N_MOD, d), mod_idx),
            pl.BlockSpec((1, ROW_TILE, TOP_K), lambda i, j: (i, j + off, 0)),
            pl.BlockSpec(memory_space=pl.ANY),
        ],
        out_specs=pl.BlockSpec((1, ROW_TILE, d), lambda i, j: (i, j, 0)),
        out_shape=jax.ShapeDtypeStruct((b, la - off * ROW_TILE, d), F32),
        scratch_shapes=[pltpu.VMEM((TOP_K, ROW_TILE, d), F32), pltpu.SemaphoreType.DMA(())],
        compiler_params=pltpu.CompilerParams(dimension_semantics=("arbitrary", "arbitrary"),
                                             vmem_limit_bytes=VMEM_LIMIT),
        name="combine",
    )(dest_flat, s, mod, tw, ys)


def _rope_tables(seq_len):
    rows = seq_len // GRID_W
    row = jnp.broadcast_to(jnp.arange(rows, dtype=F32)[:, None], (rows, GRID_W)).reshape(seq_len)
    col = jnp.broadcast_to(jnp.arange(GRID_W, dtype=F32)[None, :], (rows, GRID_W)).reshape(seq_len)
    inv = ROPE_BASE ** (-jnp.arange(ROPE_PAIRS, dtype=F32) / ROPE_PAIRS)
    ang = jnp.stack([row, col], axis=0)[:, None, :] * inv[None, :, None]
    cos_l = jnp.cos(ang).reshape(2 * ROPE_PAIRS, seq_len)
    sin_l = jnp.sin(ang).reshape(2 * ROPE_PAIRS, seq_len)
    cos_t = jnp.concatenate([jnp.ones((2 * ROPE_PAIRS, CTX_LEN), F32), cos_l], axis=1)
    sin_t = jnp.concatenate([jnp.zeros((2 * ROPE_PAIRS, CTX_LEN), F32), sin_l], axis=1)
    return cos_t, sin_t


def kernel(x, c, ctx, c_ctx, w_mod, b_mod, norm1_g, norm2_g, w_in, q_norm_g, k_norm_g,
           lambda_q1, lambda_k1, lambda_q2, lambda_k2, subln_g, conv_w, pool_w, pool_scale, w_out,
           router_w, router_b, w_gate_up, b_gate_up, w_down, b_down):
    b, seq_len, d = x.shape
    depth = w_mod.shape[0]
    la = CTX_LEN + seq_len
    t_all = b * la

    rows = -(-(b + 1) // SUBLANES) * SUBLANES
    c_all = jnp.zeros((rows, d), F32).at[:b].set(c).at[b].set(c_ctx)
    mod = _mod_call(c_all, w_mod, b_mod).reshape(depth, rows, N_MOD, d)

    cos_t, sin_t = _rope_tables(seq_len)
    s = jnp.concatenate([ctx, x], axis=1)

    n_rows = -(-(t_all * TOP_K + N_EXPERTS * (EXPERT_ROWS - 1)) // EXPERT_ROWS) * EXPERT_ROWS
    nb = n_rows // EXPERT_ROWS

    for l in range(depth):
        last = l + 1 == depth
        lam_init = 0.8 - 0.6 * math.exp(-0.3 * l)
        wl = w_in[l]
        wm = wl[:, 2 * ATTN_W:].astype(BF16)
        wqk = wl[:, :2 * ATTN_W].T.astype(BF16)
        q_gain = jnp.tile(q_norm_g[l], ATTN_W // QK_DIM) * (LOG2E * QK_DIM ** -0.5)
        k_gain = jnp.tile(k_norm_g[l], ATTN_W // QK_DIM)
        gain = jnp.broadcast_to(jnp.concatenate([q_gain, k_gain])[:, None], (2 * ATTN_W, ROW_TILE))
        q, kt, v, cp = _in_call(s, mod[l], norm1_g[l][None], wm, wqk, gain, cos_t, sin_t)

        lam_p = jnp.zeros((SUBLANES, LANES), F32)
        lam_p = lam_p.at[0, :QK_DIM].set(lambda_q1[l]).at[1, :QK_DIM].set(lambda_k1[l])
        lam_p = lam_p.at[2, :QK_DIM].set(lambda_q2[l]).at[3, :QK_DIM].set(lambda_k2[l])
        sg = jnp.tile(subln_g[l], 2)[None]
        oa = _attn_call(lam_p, q, kt, v, sg, lam_init)

        pw = jax.scipy.linalg.block_diag(*[pool_w[l, g] for g in range(len(POOL_WINDOWS))]).astype(BF16)
        wr = jnp.zeros((d, LANES), F32).at[:, :N_EXPERTS].set(router_w[l])
        wrh, wrl = _split_bf16(wr)
        br = jnp.full((1, LANES), NEG_BIG, F32).at[0, :N_EXPERTS].set(router_b[l])
        s, hu, te, tw = _out_call(s, oa, cp, conv_w[l], pw, pool_scale[l][None], w_out[l].astype(BF16),
                                  mod[l], norm2_g[l][None], wrh, wrl, br, seq_len)

        dest, counts = _rank_call(te.reshape(t_all, TOP_K))
        cnt = counts[0, :N_EXPERTS]
        pad_end = jnp.cumsum((cnt + EXPERT_ROWS - 1) // EXPERT_ROWS * EXPERT_ROWS)
        n_used = (pad_end[-1] // EXPERT_ROWS).astype(I32)
        blk = jnp.minimum(jnp.arange(nb, dtype=I32), n_used - 1) * EXPERT_ROWS
        block_e = jnp.sum((pad_end[None, :] <= blk[:, None]).astype(I32), axis=1)
        block_e = jnp.minimum(block_e, N_EXPERTS - 1)
        dest_flat = dest.reshape(t_all * TOP_K)

        xs = _disp_call(dest_flat, hu.reshape(t_all, d // 2), n_rows)
        ys = _exp_call(block_e, n_used.reshape(1), xs, w_gate_up[l].astype(BF16), b_gate_up[l],
                       w_down[l].astype(BF16), b_down[l])
        s = _comb_call(dest_flat, s, mod[l], tw, ys, latent_only=last)
    return s
```

```python
import functools
import math

import jax
import jax.numpy as jnp
from jax import lax
from jax.experimental import pallas as pl
from jax.experimental.pallas import tpu as pltpu

F32 = jnp.float32
BF16 = jnp.bfloat16
I32 = jnp.int32
U32 = jnp.uint32

GRID_W = 64
CTX_LEN = 256
HEAD_DIM = 64
QK_DIM = 32
N_HEADS = 8
ATTN_W = 512
CONV_W = 256
POOL_W = 256
POOL_WINDOWS = (2, 4, 8, 16)
POOL_GC = 64
ROPE_BASE = 10000.0
ROPE_PAIRS = 8
N_EXPERTS = 32
TOP_K = 4
N_MOD = 6
EPS = 1e-6
SWIGLU_ALPHA = 1.702
SWIGLU_LIMIT = 7.0
LOG2E = 1.4426950408889634

LANES = 128
SUBLANES = 8
ROW_TILE = 256
N_PAIRS = N_HEADS // 2
POOL_HALO = 8
EXPERT_ROWS = 256
RANK_TILE = 512
KEY_UNROLL = 4
VMEM_LIMIT = 48 * 1024 * 1024
ATTN_VMEM_LIMIT = 56 * 1024 * 1024
NEG_BIG = -1e30


def _split_bf16(a):
    hi = a.astype(BF16)
    lo = (a - hi.astype(F32)).astype(BF16)
    return hi, lo


def _dot(a, b):
    return jnp.dot(a, b, preferred_element_type=F32)


def _mod_kernel(c_ref, w_ref, b_ref, o_ref):
    cv = c_ref[...]
    s = cv * jax.nn.sigmoid(cv)
    s_hi, s_lo = _split_bf16(s)
    w_hi, w_lo = _split_bf16(w_ref[0])
    o_ref[0] = _dot(s_hi, w_hi) + _dot(s_lo, w_hi) + _dot(s_hi, w_lo) + b_ref[0]


def _mod_call(c_all, w_mod, b_mod):
    depth, d, n = w_mod.shape
    r = c_all.shape[0]
    tn = 1536
    return pl.pallas_call(
        _mod_kernel,
        grid=(depth, n // tn),
        in_specs=[
            pl.BlockSpec((r, d), lambda l, j: (0, 0)),
            pl.BlockSpec((1, d, tn), lambda l, j: (l, 0, j)),
            pl.BlockSpec((1, 1, tn), lambda l, j: (l, 0, j)),
        ],
        out_specs=pl.BlockSpec((1, r, tn), lambda l, j: (l, 0, j)),
        out_shape=jax.ShapeDtypeStruct((depth, r, n), F32),
        compiler_params=pltpu.CompilerParams(vmem_limit_bytes=VMEM_LIMIT),
        name="mod",
    )(c_all, w_mod, b_mod.reshape(depth, 1, n))


def _rms_mod(x, g, shift, scale):
    y = x * lax.rsqrt(jnp.mean(x * x, axis=-1, keepdims=True) + EPS) * g
    return y * (1.0 + scale) + shift


def _in_kernel(x_ref, mod_ref, g1_ref, wm_ref, wqk_ref, gain_ref, cos_ref, sin_ref,
               q_ref, kt_ref, v_ref, cp_ref):
    tm = x_ref.shape[1]
    h = _rms_mod(x_ref[0], g1_ref[...], mod_ref[0, 0:1, :], mod_ref[0, 1:2, :])
    pm = _dot(h.astype(BF16), wm_ref[...])
    for p in range(N_PAIRS):
        v_ref[0, p] = pm[:, p * LANES:(p + 1) * LANES].astype(BF16)
    o = ATTN_W
    cp_ref[0, :, 0:CONV_W] = pm[:, o:o + CONV_W]
    cp_ref[0, :, CONV_W:2 * CONV_W] = pm[:, o + CONV_W:o + 2 * CONV_W] * pm[:, o + 2 * CONV_W:o + 3 * CONV_W]
    cp_ref[0, :, 2 * CONV_W:] = pm[:, o + 3 * CONV_W:]

    ht = h.T.astype(BF16)
    qkt = _dot(wqk_ref[...], ht)
    ng = 2 * ATTN_W // QK_DIM
    t = qkt.reshape(ng, 4, ROPE_PAIRS, tm)
    ss = jnp.sum(jnp.sum(t * t, axis=2, keepdims=True), axis=1, keepdims=True)
    tn = t * lax.rsqrt(ss * (1.0 / QK_DIM) + EPS) * gain_ref[...].reshape(ng, 4, ROPE_PAIRS, tm)
    cs = cos_ref[...].reshape(2, ROPE_PAIRS, tm)
    sn = sin_ref[...].reshape(2, ROPE_PAIRS, tm)
    parts = []
    for a in range(2):
        t1 = tn[:, 2 * a]
        t2 = tn[:, 2 * a + 1]
        parts.append(t1 * cs[a] - t2 * sn[a])
        parts.append(t2 * cs[a] + t1 * sn[a])
    rot = jnp.stack(parts, axis=1).reshape(2 * ATTN_W, tm)
    qt = rot[:ATTN_W].T
    for p in range(N_PAIRS):
        q_ref[0, p] = qt[:, p * LANES:(p + 1) * LANES].astype(BF16)
        kt_ref[0, 0, p] = rot[ATTN_W + p * LANES:ATTN_W + (p + 1) * LANES].astype(BF16)


def _in_call(s, mod, g1, wm, wqk, gain, cos_t, sin_t):
    b, la, d = s.shape
    nt = la // ROW_TILE
    nmain = wm.shape[1]

    def mod_idx(i, j):
        return (jnp.where(j == 0, b, i), 0, 0)

    return pl.pallas_call(
        _in_kernel,
        grid=(b, nt),
        in_specs=[
            pl.BlockSpec((1, ROW_TILE, d), lambda i, j: (i, j, 0)),
            pl.BlockSpec((1, N_MOD, d), mod_idx),
            pl.BlockSpec((1, d), lambda i, j: (0, 0)),
            pl.BlockSpec((d, nmain), lambda i, j: (0, 0)),
            pl.BlockSpec((2 * ATTN_W, d), lambda i, j: (0, 0)),
            pl.BlockSpec((2 * ATTN_W, ROW_TILE), lambda i, j: (0, 0)),
            pl.BlockSpec((2 * ROPE_PAIRS, ROW_TILE), lambda i, j: (0, j)),
            pl.BlockSpec((2 * ROPE_PAIRS, ROW_TILE), lambda i, j: (0, j)),
        ],
        out_specs=[
            pl.BlockSpec((1, N_PAIRS, ROW_TILE, LANES), lambda i, j: (i, 0, j, 0)),
            pl.BlockSpec((1, 1, N_PAIRS, LANES, ROW_TILE), lambda i, j: (i, j, 0, 0, 0)),
            pl.BlockSpec((1, N_PAIRS, ROW_TILE, LANES), lambda i, j: (i, 0, j, 0)),
            pl.BlockSpec((1, ROW_TILE, 3 * CONV_W), lambda i, j: (i, j, 0)),
        ],
        out_shape=[
            jax.ShapeDtypeStruct((b, N_PAIRS, la, LANES), BF16),
            jax.ShapeDtypeStruct((b, nt, N_PAIRS, LANES, ROW_TILE), BF16),
            jax.ShapeDtypeStruct((b, N_PAIRS, la, LANES), BF16),
            jax.ShapeDtypeStruct((b, la, 3 * CONV_W), F32),
        ],
        compiler_params=pltpu.CompilerParams(vmem_limit_bytes=VMEM_LIMIT),
        name="inproj",
    )(s, mod, g1, wm, wqk, gain, cos_t, sin_t)


def _attn_kernel(lam_ref, q_ref, kt_ref, v_ref, sg_ref, o_ref, s_scr, m_scr, l_scr, acc_scr, *, lam_init):
    tq = q_ref.shape[2]
    nkc_all = kt_ref.shape[1]
    j = pl.program_id(1)
    n_groups = jnp.where(j == 0, 0, (nkc_all - 1) // KEY_UNROLL)
    lp = lam_ref[...]
    lam = (jnp.exp(jnp.sum(lp[0:1] * lp[1:2], axis=-1, keepdims=True))
           - jnp.exp(jnp.sum(lp[2:3] * lp[3:4], axis=-1, keepdims=True)) + lam_init)
    lane = lax.broadcasted_iota(I32, (1, LANES), 1)
    low = lane < HEAD_DIM

    for p in range(N_PAIRS):
        qp = q_ref[0, p]
        qs = jnp.concatenate(
            [jnp.where((lane >= g * QK_DIM) & (lane < (g + 1) * QK_DIM), qp, jnp.zeros_like(qp))
             for g in range(4)], axis=0)

        def scores(kc, qs=qs, p=p):
            sc = _dot(qs, kt_ref[0, kc, p])
            s_scr[kc] = sc
            return jnp.maximum(sc[:, :LANES], sc[:, LANES:])

        def sweep_max(g, _):
            ms = [scores(1 + g * KEY_UNROLL + u) for u in range(KEY_UNROLL)]
            while len(ms) > 1:
                ms = [jnp.maximum(a, b) for a, b in zip(ms[::2], ms[1::2])]
            m_scr[...] = jnp.maximum(m_scr[...], ms[0])
            return 0

        m_scr[...] = scores(0)
        lax.fori_loop(0, n_groups, sweep_max, 0)
        m_scr[...] = jnp.broadcast_to(jnp.max(m_scr[...], axis=-1, keepdims=True), m_scr.shape)

        def exp_pv(kc, p=p):
            sc = s_scr[kc]
            mb = m_scr[...]
            e0 = jnp.exp2(sc[:, :LANES] - mb)
            e1 = jnp.exp2(sc[:, LANES:] - mb)
            r0 = pl.multiple_of(kc * ROW_TILE, ROW_TILE)
            e = jnp.concatenate([e0.astype(BF16), e1.astype(BF16)], axis=1)
            return e0 + e1, _dot(e, v_ref[0, p, pl.ds(r0, ROW_TILE), :])

        def sweep_pv(g, _):
            parts = [exp_pv(1 + g * KEY_UNROLL + u) for u in range(KEY_UNROLL)]
            ls = [a for a, _ in parts]
            pv = [b for _, b in parts]
            while len(ls) > 1:
                ls = [a + b for a, b in zip(ls[::2], ls[1::2])]
                pv = [a + b for a, b in zip(pv[::2], pv[1::2])]
            l_scr[...] = l_scr[...] + ls[0]
            acc_scr[...] = acc_scr[...] + pv[0]
            return 0

        l_scr[...], acc_scr[...] = exp_pv(0)
        lax.fori_loop(0, n_groups, sweep_pv, 0)
        on = acc_scr[...] / jnp.sum(l_scr[...], axis=-1, keepdims=True)
        heads = [on[(2 * hh) * tq:(2 * hh + 1) * tq] - lam * on[(2 * hh + 1) * tq:(2 * hh + 2) * tq]
                 for hh in range(2)]
        o = jnp.where(low, heads[0], heads[1])
        o2 = o * o
        ss_lo = jnp.sum(jnp.where(low, o2, 0.0), axis=-1, keepdims=True)
        ss_hi = jnp.sum(jnp.where(low, 0.0, o2), axis=-1, keepdims=True)
        inv = lax.rsqrt(jnp.where(low, ss_lo, ss_hi) * (1.0 / HEAD_DIM) + EPS)
        o_ref[0, p] = (o * inv * sg_ref[...] * (1.0 - lam_init)).astype(BF16)


def _attn_call(lam_p, q, kt, v, sg, lam_init):
    b, _, la, _ = q.shape
    nt = la // ROW_TILE
    assert (nt - 1) % KEY_UNROLL == 0
    return pl.pallas_call(
        functools.partial(_attn_kernel, lam_init=lam_init),
        grid=(b, nt),
        in_specs=[
            pl.BlockSpec((SUBLANES, LANES), lambda i, j: (0, 0)),
            pl.BlockSpec((1, N_PAIRS, ROW_TILE, LANES), lambda i, j: (i, 0, j, 0)),
            pl.BlockSpec((1, nt, N_PAIRS, LANES, ROW_TILE), lambda i, j: (i, 0, 0, 0, 0)),
            pl.BlockSpec((1, N_PAIRS, la, LANES), lambda i, j: (i, 0, 0, 0)),
            pl.BlockSpec((1, LANES), lambda i, j: (0, 0)),
        ],
        out_specs=pl.BlockSpec((1, N_PAIRS, ROW_TILE, LANES), lambda i, j: (i, 0, j, 0)),
        out_shape=jax.ShapeDtypeStruct((b, N_PAIRS, la, LANES), BF16),
        scratch_shapes=[pltpu.VMEM((nt, 4 * ROW_TILE, ROW_TILE), F32),
                        pltpu.VMEM((4 * ROW_TILE, LANES), F32),
                        pltpu.VMEM((4 * ROW_TILE, LANES), F32),
                        pltpu.VMEM((4 * ROW_TILE, LANES), F32)],
        compiler_params=pltpu.CompilerParams(vmem_limit_bytes=ATTN_VMEM_LIMIT),
        name="attn",
    )(lam_p, q, kt, v, sg)


def _store_row_tiles(ref, val):
    rows = val.shape[0]
    for c in range(SUBLANES):
        ref[pl.ds(c, rows, stride=SUBLANES), :] = val[:, c * LANES:(c + 1) * LANES]


def _load_row_tiles(ref, first_row, rows, c):
    return ref[pl.ds(first_row * SUBLANES + c, rows, stride=SUBLANES), :]


def _shift_rows(a, k):
    return pltpu.roll(a, k % a.shape[0], axis=0)


def _out_kernel(x_ref, oa_ref, cp_ref, cpp_ref, cpn_ref, cw_ref, pw_ref, ps_ref, wo_ref,
                mod_ref, g2_ref, wrh_ref, wrl_ref, br_ref,
                xo_ref, hu_ref, te_ref, tw_ref, *, seq_len):
    tm = x_ref.shape[1]
    d = x_ref.shape[2]
    j = pl.program_id(1)
    nt = pl.num_programs(1)
    halo = POOL_HALO
    has_prev = j >= 2
    has_next = (j >= 1) & (j < nt - 1)
    prev = jnp.where(has_prev, cpp_ref[0], 0.0)
    nxt = jnp.where(has_next, cpn_ref[0], 0.0)
    ext = jnp.concatenate([prev, cp_ref[0], nxt], axis=0)

    z = ext[:, CONV_W:2 * CONV_W]
    conv = (cw_ref[0:1, :] * _shift_rows(z, 1) + cw_ref[1:2, :] * z + cw_ref[2:3, :] * _shift_rows(z, -1))
    o_conv = ext[halo:halo + tm, 0:CONV_W] * conv[halo:halo + tm]

    u = ext[:, 2 * CONV_W:]
    a2 = u + _shift_rows(u, 1)
    a4 = _shift_rows(a2, -1) + _shift_rows(a2, 1)
    a8 = _shift_rows(a4, -2) + _shift_rows(a4, 2)
    a16 = _shift_rows(a8, -4) + _shift_rows(a8, 4)
    lane = lax.broadcasted_iota(I32, (tm, POOL_W), 1)
    grp = lane // POOL_GC
    wsum = jnp.where(grp == 0, a2[halo:halo + tm],
                     jnp.where(grp == 1, a4[halo:halo + tm],
                               jnp.where(grp == 2, a8[halo:halo + tm], a16[halo:halo + tm])))
    half = jnp.where(grp == 0, 1, jnp.where(grp == 1, 2, jnp.where(grp == 2, 4, 8)))
    row = lax.broadcasted_iota(I32, (tm, POOL_W), 0)
    pos = jnp.where(j == 0, row, (j - 1) * tm + row)
    n_seq = jnp.where(j == 0, CTX_LEN, seq_len)
    cnt = jnp.minimum(pos + half, n_seq) - jnp.maximum(pos - half, 0)
    dlt = wsum / cnt.astype(F32) - u[halo:halo + tm]
    o_pool = _dot(dlt.astype(BF16), pw_ref[...]) * ps_ref[...]

    mix = _dot(o_conv.astype(BF16), wo_ref[ATTN_W:ATTN_W + CONV_W, :])
    mix = mix + _dot(o_pool.astype(BF16), wo_ref[ATTN_W + CONV_W:, :])
    for p in range(N_PAIRS):
        mix = mix + _dot(oa_ref[0, p], wo_ref[p * LANES:(p + 1) * LANES, :])
    x = x_ref[0] + mod_ref[0, 2:3, :] * mix
    xo_ref[0] = x

    h2 = _rms_mod(x, g2_ref[...], mod_ref[0, 3:4, :], mod_ref[0, 4:5, :])
    _store_row_tiles(hu_ref.at[0], h2)

    h_hi, h_lo = _split_bf16(h2)
    logits = (_dot(h_hi, wrh_ref[...]) + _dot(h_lo, wrh_ref[...]) + _dot(h_hi, wrl_ref[...])
              + br_ref[...])
    lanef = lax.broadcasted_iota(I32, (tm, LANES), 1).astype(F32)
    work = logits
    tops = []
    for k in range(TOP_K):
        mk = jnp.max(work, axis=-1, keepdims=True)
        ik = jnp.min(jnp.where(work == mk, lanef, float(LANES)), axis=-1, keepdims=True)
        te_ref[0, :, k:k + 1] = ik.astype(I32)
        work = jnp.where(lanef == ik, -jnp.inf, work)
        tops.append(mk)
    es = [jnp.exp(mk - tops[0]) for mk in tops]
    den = es[0] + es[1] + es[2] + es[3]
    for k in range(TOP_K):
        tw_ref[0, :, k:k + 1] = es[k] / den


def _out_call(s, oa, cp, cw, pw, ps, wo, mod, g2, wrh, wrl, br, seq_len):
    b, la, d = s.shape
    nt = la // ROW_TILE
    hb = ROW_TILE // POOL_HALO
    nhb = la // POOL_HALO

    def mod_idx(i, j):
        return (jnp.where(j == 0, b, i), 0, 0)

    const2 = lambda i, j: (0, 0)
    return pl.pallas_call(
        functools.partial(_out_kernel, seq_len=seq_len),
        grid=(b, nt),
        in_specs=[
            pl.BlockSpec((1, ROW_TILE, d), lambda i, j: (i, j, 0)),
            pl.BlockSpec((1, N_PAIRS, ROW_TILE, LANES), lambda i, j: (i, 0, j, 0)),
            pl.BlockSpec((1, ROW_TILE, 3 * CONV_W), lambda i, j: (i, j, 0)),
            pl.BlockSpec((1, POOL_HALO, 3 * CONV_W), lambda i, j: (i, jnp.maximum(j * hb - 1, 0), 0)),
            pl.BlockSpec((1, POOL_HALO, 3 * CONV_W), lambda i, j: (i, jnp.minimum((j + 1) * hb, nhb - 1), 0)),
            pl.BlockSpec((3, CONV_W), const2),
            pl.BlockSpec((POOL_W, POOL_W), const2),
            pl.BlockSpec((1, POOL_W), const2),
            pl.BlockSpec((d, d), const2),
            pl.BlockSpec((1, N_MOD, d), mod_idx),
            pl.BlockSpec((1, d), const2),
            pl.BlockSpec((d, LANES), const2),
            pl.BlockSpec((d, LANES), const2),
            pl.BlockSpec((1, LANES), const2),
        ],
        out_specs=[
            pl.BlockSpec((1, ROW_TILE, d), lambda i, j: (i, j, 0)),
            pl.BlockSpec((1, ROW_TILE * SUBLANES, LANES), lambda i, j: (i, j, 0)),
            pl.BlockSpec((1, ROW_TILE, TOP_K), lambda i, j: (i, j, 0)),
            pl.BlockSpec((1, ROW_TILE, TOP_K), lambda i, j: (i, j, 0)),
        ],
        out_shape=[
            jax.ShapeDtypeStruct((b, la, d), F32),
            jax.ShapeDtypeStruct((b, la * SUBLANES, LANES), F32),
            jax.ShapeDtypeStruct((b, la, TOP_K), I32),
            jax.ShapeDtypeStruct((b, la, TOP_K), F32),
        ],
        compiler_params=pltpu.CompilerParams(vmem_limit_bytes=VMEM_LIMIT),
        name="outproj",
    )(s, oa, cp, cp, cp, cw, pw, ps, wo, mod, g2, wrh, wrl, br)


def _rank_kernel(te_ref, tri_ref, dest_ref, cnt_ref, carry):
    ph = pl.program_id(0)
    i = pl.program_id(1)
    tr = te_ref.shape[0]
    lane1 = lax.broadcasted_iota(I32, (SUBLANES, LANES), 1)

    @pl.when((ph == 0) & (i == 0))
    def _():
        carry[...] = jnp.zeros_like(carry)

    @pl.when((ph == 1) & (i == 0))
    def _():
        cnt = carry[...]
        cnt_ref[...] = cnt.astype(I32)
        padded = jnp.ceil(cnt * (1.0 / EXPERT_ROWS)) * EXPERT_ROWS
        incl = padded
        for sh in (1, 2, 4, 8, 16):
            incl = incl + jnp.where(lane1 >= sh, pltpu.roll(incl, sh, axis=1), 0.0)
        carry[...] = incl - padded

    e = te_ref[...]
    lane = lax.broadcasted_iota(I32, (tr, LANES), 1)
    ohs = [(e[:, k:k + 1] == lane).astype(F32) for k in range(TOP_K)]
    m = ohs[0] + ohs[1] + ohs[2] + ohs[3]
    base = _dot(tri_ref[...], m.astype(BF16)) + carry[0:1, :]
    for k in range(TOP_K):
        dest_ref[:, k:k + 1] = jnp.sum(ohs[k] * base, axis=-1, keepdims=True).astype(I32)
    carry[...] = carry[...] + jnp.sum(m, axis=0, keepdims=True)


def _rank_call(te):
    t = te.shape[0]
    nt = t // RANK_TILE
    r = lax.broadcasted_iota(I32, (RANK_TILE, RANK_TILE), 0)
    c = lax.broadcasted_iota(I32, (RANK_TILE, RANK_TILE), 1)
    tri = (c < r).astype(BF16)
    return pl.pallas_call(
        _rank_kernel,
        grid=(2, nt),
        in_specs=[
            pl.BlockSpec((RANK_TILE, TOP_K), lambda ph, i: (i, 0)),
            pl.BlockSpec((RANK_TILE, RANK_TILE), lambda ph, i: (0, 0)),
        ],
        out_specs=[
            pl.BlockSpec((RANK_TILE, TOP_K), lambda ph, i: (ph * i, 0)),
            pl.BlockSpec((SUBLANES, LANES), lambda ph, i: (0, 0)),
        ],
        out_shape=[
            jax.ShapeDtypeStruct((t, TOP_K), I32),
            jax.ShapeDtypeStruct((SUBLANES, LANES), I32),
        ],
        scratch_shapes=[pltpu.VMEM((SUBLANES, LANES), F32)],
        compiler_params=pltpu.CompilerParams(dimension_semantics=("arbitrary", "arbitrary")),
        name="rank",
    )(te, tri)


def _disp_kernel(dest_ref, hu_ref, xs_in_ref, xs_ref, sem):
    del xs_in_ref

    def row_copy(r, k):
        return pltpu.make_async_copy(hu_ref.at[r], xs_ref.at[dest_ref[r * TOP_K + k]], sem)

    def start(r, _):
        for k in range(TOP_K):
            row_copy(r, k).start()
        return 0

    def wait(r, _):
        for k in range(TOP_K):
            row_copy(r, k).wait()
        return 0

    lax.fori_loop(0, ROW_TILE, start, 0)
    lax.fori_loop(0, ROW_TILE, wait, 0)


def _disp_call(dest_flat, hu, n_rows):
    t = hu.shape[0]
    xs0 = jnp.zeros((n_rows, SUBLANES, LANES), F32)
    return pl.pallas_call(
        _disp_kernel,
        grid=(t // ROW_TILE,),
        in_specs=[
            pl.BlockSpec((ROW_TILE * TOP_K,), lambda i: (i,), memory_space=pltpu.SMEM),
            pl.BlockSpec((ROW_TILE, SUBLANES, LANES), lambda i: (i, 0, 0)),
            pl.BlockSpec(memory_space=pl.ANY),
        ],
        out_specs=pl.BlockSpec(memory_space=pl.ANY),
        out_shape=jax.ShapeDtypeStruct((n_rows, SUBLANES, LANES), F32),
        scratch_shapes=[pltpu.SemaphoreType.DMA(())],
        input_output_aliases={2: 0},
        compiler_params=pltpu.CompilerParams(dimension_semantics=("arbitrary",)),
        name="dispatch",
    )(dest_flat, hu, xs0)


def _exp_kernel(be_ref, nu_ref, xs_ref, wgu_ref, bgu_ref, wdn_ref, bdn_ref, ys_ref):
    del be_ref
    i = pl.program_id(0)
    ff = wdn_ref.shape[1]
    rows = xs_ref.shape[0] // SUBLANES

    @pl.when(i < nu_ref[0])
    def _():
        xb = jnp.concatenate([_load_row_tiles(xs_ref, 0, rows, c).astype(BF16) for c in range(SUBLANES)],
                             axis=1)
        gu = _dot(xb, wgu_ref[0]) + bgu_ref[0]
        glu = jnp.minimum(gu[:, :ff], SWIGLU_LIMIT)
        lin = jnp.clip(gu[:, ff:], -SWIGLU_LIMIT, SWIGLU_LIMIT)
        act = glu * jax.nn.sigmoid(SWIGLU_ALPHA * glu) * (lin + 1.0)
        _store_row_tiles(ys_ref, _dot(act.astype(BF16), wdn_ref[0]) + bdn_ref[0])

    @pl.when(i >= nu_ref[0])
    def _():
        ys_ref[...] = jnp.zeros_like(ys_ref)


def _exp_call(block_e, n_used, xs, wgu, bgu, wdn, bdn):
    n_rows = xs.shape[0] // SUBLANES
    _, d, ff2 = wgu.shape
    ff = ff2 // 2
    nb = n_rows // EXPERT_ROWS
    grid_spec = pltpu.PrefetchScalarGridSpec(
        num_scalar_prefetch=2,
        grid=(nb,),
        in_specs=[
            pl.BlockSpec((EXPERT_ROWS * SUBLANES, LANES), lambda i, be, nu: (i, 0)),
            pl.BlockSpec((1, d, ff2), lambda i, be, nu: (be[i], 0, 0)),
            pl.BlockSpec((1, 1, ff2), lambda i, be, nu: (be[i], 0, 0)),
            pl.BlockSpec((1, ff, d), lambda i, be, nu: (be[i], 0, 0)),
            pl.BlockSpec((1, 1, d), lambda i, be, nu: (be[i], 0, 0)),
        ],
        out_specs=pl.BlockSpec((EXPERT_ROWS * SUBLANES, LANES), lambda i, be, nu: (i, 0)),
    )
    return pl.pallas_call(
        _exp_kernel,
        grid_spec=grid_spec,
        out_shape=jax.ShapeDtypeStruct((n_rows * SUBLANES, LANES), F32),
        compiler_params=pltpu.CompilerParams(dimension_semantics=("arbitrary",),
                                             vmem_limit_bytes=VMEM_LIMIT),
        name="experts",
    )(block_e, n_used, xs, wgu, bgu.reshape(-1, 1, ff2), wdn, bdn.reshape(-1, 1, d))


def _comb_kernel(dest_ref, x_ref, mod_ref, tw_ref, ys_ref, o_ref, buf, sem):
    def row_copy(r, k):
        r0 = pl.multiple_of((k * ROW_TILE + r) * SUBLANES, SUBLANES)
        return pltpu.make_async_copy(ys_ref.at[dest_ref[r * TOP_K + k]], buf.at[pl.ds(r0, SUBLANES)], sem)

    def start(r, _):
        for k in range(TOP_K):
            row_copy(r, k).start()
        return 0

    def wait(r, _):
        for k in range(TOP_K):
            row_copy(r, k).wait()
        return 0

    lax.fori_loop(0, ROW_TILE, start, 0)
    lax.fori_loop(0, ROW_TILE, wait, 0)
    w = [tw_ref[0, :, k:k + 1] for k in range(TOP_K)]
    for c in range(SUBLANES):
        acc = _load_row_tiles(buf, 0, ROW_TILE, c) * w[0]
        for k in range(1, TOP_K):
            acc = acc + _load_row_tiles(buf, k * ROW_TILE, ROW_TILE, c) * w[k]
        cols = slice(c * LANES, (c + 1) * LANES)
        o_ref[0, :, cols] = x_ref[0, :, cols] + mod_ref[0, 5:6, cols] * acc


def _comb_call(dest_flat, s, mod, tw, ys, latent_only):
    b, la, d = s.shape
    nt = la // ROW_TILE
    off = 1 if latent_only else 0

    def mod_idx(i, j):
        return (jnp.where(j + off == 0, b, i), 0, 0)

    return pl.pallas_call(
        _comb_kernel,
        grid=(b, nt - off),
        in_specs=[
            pl.BlockSpec((ROW_TILE * TOP_K,), lambda i, j: (i * nt + j + off,), memory_space=pltpu.SMEM),
            pl.BlockSpec((1, ROW_TILE, d), lambda i, j: (i, j + off, 0)),
            pl.BlockSpec((1, N_MOD, d), mod_idx),
            pl.BlockSpec((1, ROW_TILE, TOP_K), lambda i, j: (i, j + off, 0)),
            pl.BlockSpec(memory_space=pl.ANY),
        ],
        out_specs=pl.BlockSpec((1, ROW_TILE, d), lambda i, j: (i, j, 0)),
        out_shape=jax.ShapeDtypeStruct((b, la - off * ROW_TILE, d), F32),
        scratch_shapes=[pltpu.VMEM((TOP_K * ROW_TILE * SUBLANES, LANES), F32), pltpu.SemaphoreType.DMA(())],
        compiler_params=pltpu.CompilerParams(dimension_semantics=("arbitrary", "arbitrary"),
                                             vmem_limit_bytes=VMEM_LIMIT),
        name="combine",
    )(dest_flat, s, mod, tw, ys)


def _rope_tables(seq_len):
    rows = seq_len // GRID_W
    row = jnp.broadcast_to(jnp.arange(rows, dtype=F32)[:, None], (rows, GRID_W)).reshape(seq_len)
    col = jnp.broadcast_to(jnp.arange(GRID_W, dtype=F32)[None, :], (rows, GRID_W)).reshape(seq_len)
    inv = ROPE_BASE ** (-jnp.arange(ROPE_PAIRS, dtype=F32) / ROPE_PAIRS)
    ang = jnp.stack([row, col], axis=0)[:, None, :] * inv[None, :, None]
    cos_l = jnp.cos(ang).reshape(2 * ROPE_PAIRS, seq_len)
    sin_l = jnp.sin(ang).reshape(2 * ROPE_PAIRS, seq_len)
    cos_t = jnp.concatenate([jnp.ones((2 * ROPE_PAIRS, CTX_LEN), F32), cos_l], axis=1)
    sin_t = jnp.concatenate([jnp.zeros((2 * ROPE_PAIRS, CTX_LEN), F32), sin_l], axis=1)
    return cos_t, sin_t


def kernel(x, c, ctx, c_ctx, w_mod, b_mod, norm1_g, norm2_g, w_in, q_norm_g, k_norm_g,
           lambda_q1, lambda_k1, lambda_q2, lambda_k2, subln_g, conv_w, pool_w, pool_scale, w_out,
           router_w, router_b, w_gate_up, b_gate_up, w_down, b_down):
    b, seq_len, d = x.shape
    assert d == SUBLANES * LANES
    depth = w_mod.shape[0]
    la = CTX_LEN + seq_len
    t_all = b * la

    rows = -(-(b + 1) // SUBLANES) * SUBLANES
    c_all = jnp.zeros((rows, d), F32).at[:b].set(c).at[b].set(c_ctx)
    mod = _mod_call(c_all, w_mod, b_mod).reshape(depth, rows, N_MOD, d)

    cos_t, sin_t = _rope_tables(seq_len)
    s = jnp.concatenate([ctx, x], axis=1)

    n_rows = -(-(t_all * TOP_K + N_EXPERTS * (EXPERT_ROWS - 1)) // EXPERT_ROWS) * EXPERT_ROWS
    nb = n_rows // EXPERT_ROWS

    for l in range(depth):
        last = l + 1 == depth
        lam_init = 0.8 - 0.6 * math.exp(-0.3 * l)
        wl = w_in[l]
        wm = wl[:, 2 * ATTN_W:].astype(BF16)
        wqk = wl[:, :2 * ATTN_W].T.astype(BF16)
        q_gain = jnp.tile(q_norm_g[l], ATTN_W // QK_DIM) * (LOG2E * QK_DIM ** -0.5)
        k_gain = jnp.tile(k_norm_g[l], ATTN_W // QK_DIM)
        gain = jnp.broadcast_to(jnp.concatenate([q_gain, k_gain])[:, None], (2 * ATTN_W, ROW_TILE))
        q, kt, v, cp = _in_call(s, mod[l], norm1_g[l][None], wm, wqk, gain, cos_t, sin_t)

        lam_p = jnp.zeros((SUBLANES, LANES), F32)
        lam_p = lam_p.at[0, :QK_DIM].set(lambda_q1[l]).at[1, :QK_DIM].set(lambda_k1[l])
        lam_p = lam_p.at[2, :QK_DIM].set(lambda_q2[l]).at[3, :QK_DIM].set(lambda_k2[l])
        sg = jnp.tile(subln_g[l], 2)[None]
        oa = _attn_call(lam_p, q, kt, v, sg, lam_init)

        pw = jax.scipy.linalg.block_diag(*[pool_w[l, g] for g in range(len(POOL_WINDOWS))]).astype(BF16)
        wr = jnp.zeros((d, LANES), F32).at[:, :N_EXPERTS].set(router_w[l])
        wrh, wrl = _split_bf16(wr)
        br = jnp.full((1, LANES), NEG_BIG, F32).at[0, :N_EXPERTS].set(router_b[l])
        s, hu, te, tw = _out_call(s, oa, cp, conv_w[l], pw, pool_scale[l][None], w_out[l].astype(BF16),
                                  mod[l], norm2_g[l][None], wrh, wrl, br, seq_len)

        dest, counts = _rank_call(te.reshape(t_all, TOP_K))
        cnt = counts[0, :N_EXPERTS]
        pad_end = jnp.cumsum((cnt + EXPERT_ROWS - 1) // EXPERT_ROWS * EXPERT_ROWS)
        n_used = (pad_end[-1] // EXPERT_ROWS).astype(I32)
        blk = jnp.minimum(jnp.arange(nb, dtype=I32), n_used - 1) * EXPERT_ROWS
        block_e = jnp.sum((pad_end[None, :] <= blk[:, None]).astype(I32), axis=1)
        block_e = jnp.minimum(block_e, N_EXPERTS - 1)
        dest_flat = dest.reshape(t_all * TOP_K)

        xs = _disp_call(dest_flat, hu.reshape(t_all, SUBLANES, LANES), n_rows)
        ys = _exp_call(block_e, n_used.reshape(1), xs.reshape(n_rows * SUBLANES, LANES),
                       w_gate_up[l].astype(BF16), b_gate_up[l], w_down[l].astype(BF16), b_down[l])
        s = _comb_call(dest_flat, s, mod[l], tw, ys.reshape(n_rows, SUBLANES, LANES), latent_only=last)
    return s
```

```python
import functools
import math

import jax
import jax.numpy as jnp
from jax import lax
from jax.experimental import pallas as pl
from jax.experimental.pallas import tpu as pltpu

F32 = jnp.float32
BF16 = jnp.bfloat16
I32 = jnp.int32
U32 = jnp.uint32

GRID_W = 64
CTX_LEN = 256
HEAD_DIM = 64
QK_DIM = 32
N_HEADS = 8
ATTN_W = 512
CONV_W = 256
POOL_W = 256
POOL_WINDOWS = (2, 4, 8, 16)
POOL_GC = 64
ROPE_BASE = 10000.0
ROPE_PAIRS = 8
N_EXPERTS = 32
TOP_K = 4
N_MOD = 6
EPS = 1e-6
SWIGLU_ALPHA = 1.702
SWIGLU_LIMIT = 7.0
LOG2E = 1.4426950408889634

LANES = 128
SUBLANES = 8
ROW_TILE = 256
N_PAIRS = N_HEADS // 2
POOL_HALO = 8
EXPERT_ROWS = 256
RANK_TILE = 512
KEY_UNROLL = 8
VMEM_LIMIT = 48 * 1024 * 1024
ATTN_VMEM_LIMIT = 40 * 1024 * 1024
NEG_BIG = -1e30
BOUND_SLACK = 1.0 + 2.0 ** -9
UNDERFLOW_GUARD = 2.0 ** -100


def _split_bf16(a):
    hi = a.astype(BF16)
    lo = (a - hi.astype(F32)).astype(BF16)
    return hi, lo


def _dot(a, b):
    return jnp.dot(a, b, preferred_element_type=F32)


def _mod_kernel(c_ref, w_ref, b_ref, o_ref):
    cv = c_ref[...]
    s = cv * jax.nn.sigmoid(cv)
    s_hi, s_lo = _split_bf16(s)
    w_hi, w_lo = _split_bf16(w_ref[0])
    o_ref[0] = _dot(s_hi, w_hi) + _dot(s_lo, w_hi) + _dot(s_hi, w_lo) + b_ref[0]


def _mod_call(c_all, w_mod, b_mod):
    depth, d, n = w_mod.shape
    r = c_all.shape[0]
    tn = 1536
    return pl.pallas_call(
        _mod_kernel,
        grid=(depth, n // tn),
        in_specs=[
            pl.BlockSpec((r, d), lambda l, j: (0, 0)),
            pl.BlockSpec((1, d, tn), lambda l, j: (l, 0, j)),
            pl.BlockSpec((1, 1, tn), lambda l, j: (l, 0, j)),
        ],
        out_specs=pl.BlockSpec((1, r, tn), lambda l, j: (l, 0, j)),
        out_shape=jax.ShapeDtypeStruct((depth, r, n), F32),
        compiler_params=pltpu.CompilerParams(vmem_limit_bytes=VMEM_LIMIT),
        name="mod",
    )(c_all, w_mod, b_mod.reshape(depth, 1, n))


def _rms_mod(x, g, shift, scale):
    y = x * lax.rsqrt(jnp.mean(x * x, axis=-1, keepdims=True) + EPS) * g
    return y * (1.0 + scale) + shift


def _in_kernel(x_ref, mod_ref, g1_ref, wm_ref, wqk_ref, gain_ref, cos_ref, sin_ref,
               q_ref, kt_ref, v_ref, cp_ref):
    tm = x_ref.shape[1]
    h = _rms_mod(x_ref[0], g1_ref[...], mod_ref[0, 0:1, :], mod_ref[0, 1:2, :])
    pm = _dot(h.astype(BF16), wm_ref[...])
    for p in range(N_PAIRS):
        v_ref[0, p] = pm[:, p * LANES:(p + 1) * LANES].astype(BF16)
    o = ATTN_W
    cp_ref[0, :, 0:CONV_W] = pm[:, o:o + CONV_W]
    cp_ref[0, :, CONV_W:2 * CONV_W] = pm[:, o + CONV_W:o + 2 * CONV_W] * pm[:, o + 2 * CONV_W:o + 3 * CONV_W]
    cp_ref[0, :, 2 * CONV_W:] = pm[:, o + 3 * CONV_W:]

    ht = h.T.astype(BF16)
    qkt = _dot(wqk_ref[...], ht)
    ng = 2 * ATTN_W // QK_DIM
    t = qkt.reshape(ng, 4, ROPE_PAIRS, tm)
    ss = jnp.sum(jnp.sum(t * t, axis=2, keepdims=True), axis=1, keepdims=True)
    tn = t * lax.rsqrt(ss * (1.0 / QK_DIM) + EPS) * gain_ref[...].reshape(ng, 4, ROPE_PAIRS, tm)
    cs = cos_ref[...].reshape(2, ROPE_PAIRS, tm)
    sn = sin_ref[...].reshape(2, ROPE_PAIRS, tm)
    parts = []
    for a in range(2):
        t1 = tn[:, 2 * a]
        t2 = tn[:, 2 * a + 1]
        parts.append(t1 * cs[a] - t2 * sn[a])
        parts.append(t2 * cs[a] + t1 * sn[a])
    rot = jnp.stack(parts, axis=1).reshape(2 * ATTN_W, tm)
    qt = rot[:ATTN_W].T
    for p in range(N_PAIRS):
        q_ref[0, p] = qt[:, p * LANES:(p + 1) * LANES].astype(BF16)
        kt_ref[0, 0, p] = rot[ATTN_W + p * LANES:ATTN_W + (p + 1) * LANES].astype(BF16)


def _in_call(s, mod, g1, wm, wqk, gain, cos_t, sin_t):
    b, la, d = s.shape
    nt = la // ROW_TILE
    nmain = wm.shape[1]

    def mod_idx(i, j):
        return (jnp.where(j == 0, b, i), 0, 0)

    return pl.pallas_call(
        _in_kernel,
        grid=(b, nt),
        in_specs=[
            pl.BlockSpec((1, ROW_TILE, d), lambda i, j: (i, j, 0)),
            pl.BlockSpec((1, N_MOD, d), mod_idx),
            pl.BlockSpec((1, d), lambda i, j: (0, 0)),
            pl.BlockSpec((d, nmain), lambda i, j: (0, 0)),
            pl.BlockSpec((2 * ATTN_W, d), lambda i, j: (0, 0)),
            pl.BlockSpec((2 * ATTN_W, ROW_TILE), lambda i, j: (0, 0)),
            pl.BlockSpec((2 * ROPE_PAIRS, ROW_TILE), lambda i, j: (0, j)),
            pl.BlockSpec((2 * ROPE_PAIRS, ROW_TILE), lambda i, j: (0, j)),
        ],
        out_specs=[
            pl.BlockSpec((1, N_PAIRS, ROW_TILE, LANES), lambda i, j: (i, 0, j, 0)),
            pl.BlockSpec((1, 1, N_PAIRS, LANES, ROW_TILE), lambda i, j: (i, j, 0, 0, 0)),
            pl.BlockSpec((1, N_PAIRS, ROW_TILE, LANES), lambda i, j: (i, 0, j, 0)),
            pl.BlockSpec((1, ROW_TILE, 3 * CONV_W), lambda i, j: (i, j, 0)),
        ],
        out_shape=[
            jax.ShapeDtypeStruct((b, N_PAIRS, la, LANES), BF16),
            jax.ShapeDtypeStruct((b, nt, N_PAIRS, LANES, ROW_TILE), BF16),
            jax.ShapeDtypeStruct((b, N_PAIRS, la, LANES), BF16),
            jax.ShapeDtypeStruct((b, la, 3 * CONV_W), F32),
        ],
        compiler_params=pltpu.CompilerParams(vmem_limit_bytes=VMEM_LIMIT),
        name="inproj",
    )(s, mod, g1, wm, wqk, gain, cos_t, sin_t)


def _attn_kernel(lam_ref, q_ref, kt_ref, v_ref, sg_ref, o_ref, kn_scr, mb_scr, l_scr, acc_scr, *, lam_init):
    tq = q_ref.shape[2]
    nkc_all = kt_ref.shape[1]
    j = pl.program_id(1)
    n_groups = jnp.where(j == 0, 0, (nkc_all - 1) // KEY_UNROLL)
    n_chunks = jnp.where(j == 0, 1, nkc_all)
    lp = lam_ref[...]
    lam = (jnp.exp(jnp.sum(lp[0:1] * lp[1:2], axis=-1, keepdims=True))
           - jnp.exp(jnp.sum(lp[2:3] * lp[3:4], axis=-1, keepdims=True)) + lam_init)
    lane = lax.broadcasted_iota(I32, (1, LANES), 1)
    low = lane < HEAD_DIM
    groups = [(lane >= g * QK_DIM) & (lane < (g + 1) * QK_DIM) for g in range(4)]

    @pl.when(j == 0)
    def _():
        for p in range(N_PAIRS):
            def key_norms(kc, mx, p=p):
                k = kt_ref[0, kc, p].astype(F32)
                k2 = jnp.sum((k * k).reshape(4, QK_DIM, ROW_TILE), axis=1)
                return jnp.maximum(mx, k2)
            mx = lax.fori_loop(0, nkc_all, key_norms, jnp.zeros((4, ROW_TILE), F32))
            kn = jnp.sqrt(jnp.max(mx, axis=-1, keepdims=True))
            kn_scr[p] = jnp.broadcast_to(jnp.concatenate([kn, kn], axis=0), (SUBLANES, LANES))

    for p in range(N_PAIRS):
        qp = q_ref[0, p]
        qs = jnp.concatenate([jnp.where(groups[g], qp, jnp.zeros_like(qp)) for g in range(4)], axis=0)
        qf = qp.astype(F32)
        q2 = qf * qf
        for g in range(4):
            qn = jnp.sqrt(jnp.sum(jnp.where(groups[g], q2, 0.0), axis=-1, keepdims=True))
            mb_scr[g * tq:(g + 1) * tq, :] = qn * kn_scr[p, g:g + 1, :] * BOUND_SLACK

        def exp_pv(kc, qs=qs, p=p):
            sc = _dot(qs, kt_ref[0, kc, p])
            mb = mb_scr[...]
            e0 = jnp.exp2(sc[:, :LANES] - mb)
            e1 = jnp.exp2(sc[:, LANES:] - mb)
            r0 = pl.multiple_of(kc * ROW_TILE, ROW_TILE)
            e = jnp.concatenate([e0.astype(BF16), e1.astype(BF16)], axis=1)
            return e0 + e1, _dot(e, v_ref[0, p, pl.ds(r0, ROW_TILE), :])

        def sweep(g, _):
            parts = [exp_pv(1 + g * KEY_UNROLL + u) for u in range(KEY_UNROLL)]
            ls = [a for a, _ in parts]
            pv = [b for _, b in parts]
            while len(ls) > 1:
                ls = [a + b for a, b in zip(ls[::2], ls[1::2])]
                pv = [a + b for a, b in zip(pv[::2], pv[1::2])]
            l_scr[...] = l_scr[...] + ls[0]
            acc_scr[...] = acc_scr[...] + pv[0]
            return 0

        l_scr[...], acc_scr[...] = exp_pv(0)
        lax.fori_loop(0, n_groups, sweep, 0)
        lmin = jnp.min(jnp.sum(l_scr[...], axis=-1, keepdims=True))

        @pl.when(jnp.logical_not(lmin >= UNDERFLOW_GUARD))
        def _():
            def row_max(kc, m, qs=qs, p=p):
                sc = _dot(qs, kt_ref[0, kc, p])
                return jnp.maximum(m, jnp.maximum(sc[:, :LANES], sc[:, LANES:]))
            m = lax.fori_loop(0, n_chunks, row_max, jnp.full((4 * tq, LANES), -jnp.inf, F32))
            mb_scr[...] = jnp.broadcast_to(jnp.max(m, axis=-1, keepdims=True), mb_scr.shape)
            l_scr[...] = jnp.zeros(l_scr.shape, F32)
            acc_scr[...] = jnp.zeros(acc_scr.shape, F32)

            def redo(kc, _):
                dl, dacc = exp_pv(kc)
                l_scr[...] = l_scr[...] + dl
                acc_scr[...] = acc_scr[...] + dacc
                return 0
            lax.fori_loop(0, n_chunks, redo, 0)

        on = acc_scr[...] / jnp.sum(l_scr[...], axis=-1, keepdims=True)
        heads = [on[(2 * hh) * tq:(2 * hh + 1) * tq] - lam * on[(2 * hh + 1) * tq:(2 * hh + 2) * tq]
                 for hh in range(2)]
        o = jnp.where(low, heads[0], heads[1])
        o2 = o * o
        ss_lo = jnp.sum(jnp.where(low, o2, 0.0), axis=-1, keepdims=True)
        ss_hi = jnp.sum(jnp.where(low, 0.0, o2), axis=-1, keepdims=True)
        inv = lax.rsqrt(jnp.where(low, ss_lo, ss_hi) * (1.0 / HEAD_DIM) + EPS)
        o_ref[0, p] = (o * inv * sg_ref[...] * (1.0 - lam_init)).astype(BF16)


def _attn_call(lam_p, q, kt, v, sg, lam_init):
    b, _, la, _ = q.shape
    nt = la // ROW_TILE
    assert (nt - 1) % KEY_UNROLL == 0
    return pl.pallas_call(
        functools.partial(_attn_kernel, lam_init=lam_init),
        grid=(b, nt),
        in_specs=[
            pl.BlockSpec((SUBLANES, LANES), lambda i, j: (0, 0)),
            pl.BlockSpec((1, N_PAIRS, ROW_TILE, LANES), lambda i, j: (i, 0, j, 0)),
            pl.BlockSpec((1, nt, N_PAIRS, LANES, ROW_TILE), lambda i, j: (i, 0, 0, 0, 0)),
            pl.BlockSpec((1, N_PAIRS, la, LANES), lambda i, j: (i, 0, 0, 0)),
            pl.BlockSpec((1, LANES), lambda i, j: (0, 0)),
        ],
        out_specs=pl.BlockSpec((1, N_PAIRS, ROW_TILE, LANES), lambda i, j: (i, 0, j, 0)),
        out_shape=jax.ShapeDtypeStruct((b, N_PAIRS, la, LANES), BF16),
        scratch_shapes=[pltpu.VMEM((N_PAIRS, SUBLANES, LANES), F32),
                        pltpu.VMEM((4 * ROW_TILE, LANES), F32),
                        pltpu.VMEM((4 * ROW_TILE, LANES), F32),
                        pltpu.VMEM((4 * ROW_TILE, LANES), F32)],
        compiler_params=pltpu.CompilerParams(vmem_limit_bytes=ATTN_VMEM_LIMIT),
        name="attn",
    )(lam_p, q, kt, v, sg)


def _store_row_tiles(ref, val):
    rows = val.shape[0]
    for c in range(SUBLANES):
        ref[pl.ds(c, rows, stride=SUBLANES), :] = val[:, c * LANES:(c + 1) * LANES]


def _load_row_tiles(ref, first_row, rows, c):
    return ref[pl.ds(first_row * SUBLANES + c, rows, stride=SUBLANES), :]


def _shift_rows(a, k):
    return pltpu.roll(a, k % a.shape[0], axis=0)


def _out_kernel(x_ref, oa_ref, cp_ref, cpp_ref, cpn_ref, cw_ref, pw_ref, ps_ref, wo_ref,
                mod_ref, g2_ref, wrh_ref, wrl_ref, br_ref,
                xo_ref, hu_ref, te_ref, tw_ref, *, seq_len):
    tm = x_ref.shape[1]
    d = x_ref.shape[2]
    j = pl.program_id(1)
    nt = pl.num_programs(1)
    halo = POOL_HALO
    has_prev = j >= 2
    has_next = (j >= 1) & (j < nt - 1)
    prev = jnp.where(has_prev, cpp_ref[0], 0.0)
    nxt = jnp.where(has_next, cpn_ref[0], 0.0)
    ext = jnp.concatenate([prev, cp_ref[0], nxt], axis=0)

    z = ext[:, CONV_W:2 * CONV_W]
    conv = (cw_ref[0:1, :] * _shift_rows(z, 1) + cw_ref[1:2, :] * z + cw_ref[2:3, :] * _shift_rows(z, -1))
    o_conv = ext[halo:halo + tm, 0:CONV_W] * conv[halo:halo + tm]

    u = ext[:, 2 * CONV_W:]
    a2 = u + _shift_rows(u, 1)
    a4 = _shift_rows(a2, -1) + _shift_rows(a2, 1)
    a8 = _shift_rows(a4, -2) + _shift_rows(a4, 2)
    a16 = _shift_rows(a8, -4) + _shift_rows(a8, 4)
    lane = lax.broadcasted_iota(I32, (tm, POOL_W), 1)
    grp = lane // POOL_GC
    wsum = jnp.where(grp == 0, a2[halo:halo + tm],
                     jnp.where(grp == 1, a4[halo:halo + tm],
                               jnp.where(grp == 2, a8[halo:halo + tm], a16[halo:halo + tm])))
    half = jnp.where(grp == 0, 1, jnp.where(grp == 1, 2, jnp.where(grp == 2, 4, 8)))
    row = lax.broadcasted_iota(I32, (tm, POOL_W), 0)
    pos = jnp.where(j == 0, row, (j - 1) * tm + row)
    n_seq = jnp.where(j == 0, CTX_LEN, seq_len)
    cnt = jnp.minimum(pos + half, n_seq) - jnp.maximum(pos - half, 0)
    dlt = wsum / cnt.astype(F32) - u[halo:halo + tm]
    o_pool = _dot(dlt.astype(BF16), pw_ref[...]) * ps_ref[...]

    mix = _dot(o_conv.astype(BF16), wo_ref[ATTN_W:ATTN_W + CONV_W, :])
    mix = mix + _dot(o_pool.astype(BF16), wo_ref[ATTN_W + CONV_W:, :])
    for p in range(N_PAIRS):
        mix = mix + _dot(oa_ref[0, p], wo_ref[p * LANES:(p + 1) * LANES, :])
    x = x_ref[0] + mod_ref[0, 2:3, :] * mix
    xo_ref[0] = x

    h2 = _rms_mod(x, g2_ref[...], mod_ref[0, 3:4, :], mod_ref[0, 4:5, :])
    _store_row_tiles(hu_ref.at[0], h2)

    h_hi, h_lo = _split_bf16(h2)
    logits = (_dot(h_hi, wrh_ref[...]) + _dot(h_lo, wrh_ref[...]) + _dot(h_hi, wrl_ref[...])
              + br_ref[...])
    lanef = lax.broadcasted_iota(I32, (tm, LANES), 1).astype(F32)
    work = logits
    tops = []
    for k in range(TOP_K):
        mk = jnp.max(work, axis=-1, keepdims=True)
        ik = jnp.min(jnp.where(work == mk, lanef, float(LANES)), axis=-1, keepdims=True)
        te_ref[0, :, k:k + 1] = ik.astype(I32)
        work = jnp.where(lanef == ik, -jnp.inf, work)
        tops.append(mk)
    es = [jnp.exp(mk - tops[0]) for mk in tops]
    den = es[0] + es[1] + es[2] + es[3]
    for k in range(TOP_K):
        tw_ref[0, :, k:k + 1] = es[k] / den


def _out_call(s, oa, cp, cw, pw, ps, wo, mod, g2, wrh, wrl, br, seq_len):
    b, la, d = s.shape
    nt = la // ROW_TILE
    hb = ROW_TILE // POOL_HALO
    nhb = la // POOL_HALO

    def mod_idx(i, j):
        return (jnp.where(j == 0, b, i), 0, 0)

    const2 = lambda i, j: (0, 0)
    return pl.pallas_call(
        functools.partial(_out_kernel, seq_len=seq_len),
        grid=(b, nt),
        in_specs=[
            pl.BlockSpec((1, ROW_TILE, d), lambda i, j: (i, j, 0)),
            pl.BlockSpec((1, N_PAIRS, ROW_TILE, LANES), lambda i, j: (i, 0, j, 0)),
            pl.BlockSpec((1, ROW_TILE, 3 * CONV_W), lambda i, j: (i, j, 0)),
            pl.BlockSpec((1, POOL_HALO, 3 * CONV_W), lambda i, j: (i, jnp.maximum(j * hb - 1, 0), 0)),
            pl.BlockSpec((1, POOL_HALO, 3 * CONV_W), lambda i, j: (i, jnp.minimum((j + 1) * hb, nhb - 1), 0)),
            pl.BlockSpec((3, CONV_W), const2),
            pl.BlockSpec((POOL_W, POOL_W), const2),
            pl.BlockSpec((1, POOL_W), const2),
            pl.BlockSpec((d, d), const2),
            pl.BlockSpec((1, N_MOD, d), mod_idx),
            pl.BlockSpec((1, d), const2),
            pl.BlockSpec((d, LANES), const2),
            pl.BlockSpec((d, LANES), const2),
            pl.BlockSpec((1, LANES), const2),
        ],
        out_specs=[
            pl.BlockSpec((1, ROW_TILE, d), lambda i, j: (i, j, 0)),
            pl.BlockSpec((1, ROW_TILE * SUBLANES, LANES), lambda i, j: (i, j, 0)),
            pl.BlockSpec((1, ROW_TILE, TOP_K), lambda i, j: (i, j, 0)),
            pl.BlockSpec((1, ROW_TILE, TOP_K), lambda i, j: (i, j, 0)),
        ],
        out_shape=[
            jax.ShapeDtypeStruct((b, la, d), F32),
            jax.ShapeDtypeStruct((b, la * SUBLANES, LANES), F32),
            jax.ShapeDtypeStruct((b, la, TOP_K), I32),
            jax.ShapeDtypeStruct((b, la, TOP_K), F32),
        ],
        compiler_params=pltpu.CompilerParams(vmem_limit_bytes=VMEM_LIMIT),
        name="outproj",
    )(s, oa, cp, cp, cp, cw, pw, ps, wo, mod, g2, wrh, wrl, br)


def _rank_kernel(te_ref, tri_ref, dest_ref, cnt_ref, carry):
    ph = pl.program_id(0)
    i = pl.program_id(1)
    tr = te_ref.shape[0]
    lane1 = lax.broadcasted_iota(I32, (SUBLANES, LANES), 1)

    @pl.when((ph == 0) & (i == 0))
    def _():
        carry[...] = jnp.zeros_like(carry)

    @pl.when((ph == 1) & (i == 0))
    def _():
        cnt = carry[...]
        cnt_ref[...] = cnt.astype(I32)
        padded = jnp.ceil(cnt * (1.0 / EXPERT_ROWS)) * EXPERT_ROWS
        incl = padded
        for sh in (1, 2, 4, 8, 16):
            incl = incl + jnp.where(lane1 >= sh, pltpu.roll(incl, sh, axis=1), 0.0)
        carry[...] = incl - padded

    e = te_ref[...]
    lane = lax.broadcasted_iota(I32, (tr, LANES), 1)
    ohs = [(e[:, k:k + 1] == lane).astype(F32) for k in range(TOP_K)]
    m = ohs[0] + ohs[1] + ohs[2] + ohs[3]
    base = _dot(tri_ref[...], m.astype(BF16)) + carry[0:1, :]
    for k in range(TOP_K):
        dest_ref[:, k:k + 1] = jnp.sum(ohs[k] * base, axis=-1, keepdims=True).astype(I32)
    carry[...] = carry[...] + jnp.sum(m, axis=0, keepdims=True)


def _rank_call(te):
    t = te.shape[0]
    nt = t // RANK_TILE
    r = lax.broadcasted_iota(I32, (RANK_TILE, RANK_TILE), 0)
    c = lax.broadcasted_iota(I32, (RANK_TILE, RANK_TILE), 1)
    tri = (c < r).astype(BF16)
    return pl.pallas_call(
        _rank_kernel,
        grid=(2, nt),
        in_specs=[
            pl.BlockSpec((RANK_TILE, TOP_K), lambda ph, i: (i, 0)),
            pl.BlockSpec((RANK_TILE, RANK_TILE), lambda ph, i: (0, 0)),
        ],
        out_specs=[
            pl.BlockSpec((RANK_TILE, TOP_K), lambda ph, i: (ph * i, 0)),
            pl.BlockSpec((SUBLANES, LANES), lambda ph, i: (0, 0)),
        ],
        out_shape=[
            jax.ShapeDtypeStruct((t, TOP_K), I32),
            jax.ShapeDtypeStruct((SUBLANES, LANES), I32),
        ],
        scratch_shapes=[pltpu.VMEM((SUBLANES, LANES), F32)],
        compiler_params=pltpu.CompilerParams(dimension_semantics=("arbitrary", "arbitrary")),
        name="rank",
    )(te, tri)


def _disp_kernel(pad0_ref, npad_ref, dest_ref, hu_ref, xs_ref, zero, sem, zsem):
    def row_copy(r, k):
        return pltpu.make_async_copy(hu_ref.at[r], xs_ref.at[dest_ref[r * TOP_K + k]], sem)

    def start(r, _):
        for k in range(TOP_K):
            row_copy(r, k).start()
        return 0

    def wait(r, _):
        for k in range(TOP_K):
            row_copy(r, k).wait()
        return 0

    lax.fori_loop(0, ROW_TILE, start, 0)
    lax.fori_loop(0, ROW_TILE, wait, 0)

    @pl.when(pl.program_id(0) == pl.num_programs(0) - 1)
    def _():
        zero[...] = jnp.zeros(zero.shape, F32)

        def pad_copy(e, r):
            return pltpu.make_async_copy(zero, xs_ref.at[pad0_ref[e] + r], zsem)

        def per_expert(e, _):
            def start_pad(r, c):
                pad_copy(e, r).start()
                return c

            def wait_pad(r, c):
                pad_copy(e, r).wait()
                return c

            lax.fori_loop(0, npad_ref[e], start_pad, 0)
            lax.fori_loop(0, npad_ref[e], wait_pad, 0)
            return 0

        lax.fori_loop(0, N_EXPERTS, per_expert, 0)


def _disp_call(pad0, npad, dest_flat, hu, n_rows):
    t = hu.shape[0]
    grid_spec = pltpu.PrefetchScalarGridSpec(
        num_scalar_prefetch=2,
        grid=(t // ROW_TILE,),
        in_specs=[
            pl.BlockSpec((ROW_TILE * TOP_K,), lambda i, p0, n: (i,), memory_space=pltpu.SMEM),
            pl.BlockSpec((ROW_TILE, SUBLANES, LANES), lambda i, p0, n: (i, 0, 0)),
        ],
        out_specs=pl.BlockSpec(memory_space=pl.ANY),
        scratch_shapes=[pltpu.VMEM((SUBLANES, LANES), F32), pltpu.SemaphoreType.DMA(()),
                        pltpu.SemaphoreType.DMA(())],
    )
    return pl.pallas_call(
        _disp_kernel,
        grid_spec=grid_spec,
        out_shape=jax.ShapeDtypeStruct((n_rows, SUBLANES, LANES), F32),
        compiler_params=pltpu.CompilerParams(dimension_semantics=("arbitrary",)),
        name="dispatch",
    )(pad0, npad, dest_flat, hu)


def _exp_kernel(be_ref, nu_ref, xs_ref, wgu_ref, bgu_ref, wdn_ref, bdn_ref, ys_ref):
    del be_ref
    i = pl.program_id(0)
    ff = wdn_ref.shape[1]
    rows = xs_ref.shape[0] // SUBLANES

    @pl.when(i < nu_ref[0])
    def _():
        xb = jnp.concatenate([_load_row_tiles(xs_ref, 0, rows, c).astype(BF16) for c in range(SUBLANES)],
                             axis=1)
        gu = _dot(xb, wgu_ref[0]) + bgu_ref[0]
        glu = jnp.minimum(gu[:, :ff], SWIGLU_LIMIT)
        lin = jnp.clip(gu[:, ff:], -SWIGLU_LIMIT, SWIGLU_LIMIT)
        act = glu * jax.nn.sigmoid(SWIGLU_ALPHA * glu) * (lin + 1.0)
        _store_row_tiles(ys_ref, _dot(act.astype(BF16), wdn_ref[0]) + bdn_ref[0])

    @pl.when(i >= nu_ref[0])
    def _():
        ys_ref[...] = jnp.zeros_like(ys_ref)


def _exp_call(block_e, n_used, xs, wgu, bgu, wdn, bdn):
    n_rows = xs.shape[0] // SUBLANES
    _, d, ff2 = wgu.shape
    ff = ff2 // 2
    nb = n_rows // EXPERT_ROWS
    grid_spec = pltpu.PrefetchScalarGridSpec(
        num_scalar_prefetch=2,
        grid=(nb,),
        in_specs=[
            pl.BlockSpec((EXPERT_ROWS * SUBLANES, LANES), lambda i, be, nu: (jnp.minimum(i, nu[0] - 1), 0)),
            pl.BlockSpec((1, d, ff2), lambda i, be, nu: (be[i], 0, 0)),
            pl.BlockSpec((1, 1, ff2), lambda i, be, nu: (be[i], 0, 0)),
            pl.BlockSpec((1, ff, d), lambda i, be, nu: (be[i], 0, 0)),
            pl.BlockSpec((1, 1, d), lambda i, be, nu: (be[i], 0, 0)),
        ],
        out_specs=pl.BlockSpec((EXPERT_ROWS * SUBLANES, LANES), lambda i, be, nu: (i, 0)),
    )
    return pl.pallas_call(
        _exp_kernel,
        grid_spec=grid_spec,
        out_shape=jax.ShapeDtypeStruct((n_rows * SUBLANES, LANES), F32),
        compiler_params=pltpu.CompilerParams(dimension_semantics=("arbitrary",),
                                             vmem_limit_bytes=VMEM_LIMIT),
        name="experts",
    )(block_e, n_used, xs, wgu, bgu.reshape(-1, 1, ff2), wdn, bdn.reshape(-1, 1, d))


def _comb_kernel(dest_ref, x_ref, mod_ref, tw_ref, ys_ref, o_ref, buf, sem):
    def row_copy(r, k):
        r0 = pl.multiple_of((k * ROW_TILE + r) * SUBLANES, SUBLANES)
        return pltpu.make_async_copy(ys_ref.at[dest_ref[r * TOP_K + k]], buf.at[pl.ds(r0, SUBLANES)], sem)

    def start(r, _):
        for k in range(TOP_K):
            row_copy(r, k).start()
        return 0

    def wait(r, _):
        for k in range(TOP_K):
            row_copy(r, k).wait()
        return 0

    lax.fori_loop(0, ROW_TILE, start, 0)
    lax.fori_loop(0, ROW_TILE, wait, 0)
    w = [tw_ref[0, :, k:k + 1] for k in range(TOP_K)]
    for c in range(SUBLANES):
        acc = _load_row_tiles(buf, 0, ROW_TILE, c) * w[0]
        for k in range(1, TOP_K):
            acc = acc + _load_row_tiles(buf, k * ROW_TILE, ROW_TILE, c) * w[k]
        cols = slice(c * LANES, (c + 1) * LANES)
        o_ref[0, :, cols] = x_ref[0, :, cols] + mod_ref[0, 5:6, cols] * acc


def _comb_call(dest_flat, s, mod, tw, ys, latent_only):
    b, la, d = s.shape
    nt = la // ROW_TILE
    off = 1 if latent_only else 0

    def mod_idx(i, j):
        return (jnp.where(j + off == 0, b, i), 0, 0)

    return pl.pallas_call(
        _comb_kernel,
        grid=(b, nt - off),
        in_specs=[
            pl.BlockSpec((ROW_TILE * TOP_K,), lambda i, j: (i * nt + j + off,), memory_space=pltpu.SMEM),
            pl.BlockSpec((1, ROW_TILE, d), lambda i, j: (i, j + off, 0)),
            pl.BlockSpec((1, N_MOD, d), mod_idx),
            pl.BlockSpec((1, ROW_TILE, TOP_K), lambda i, j: (i, j + off, 0)),
            pl.BlockSpec(memory_space=pl.ANY),
        ],
        out_specs=pl.BlockSpec((1, ROW_TILE, d), lambda i, j: (i, j, 0)),
        out_shape=jax.ShapeDtypeStruct((b, la - off * ROW_TILE, d), F32),
        scratch_shapes=[pltpu.VMEM((TOP_K * ROW_TILE * SUBLANES, LANES), F32), pltpu.SemaphoreType.DMA(())],
        compiler_params=pltpu.CompilerParams(dimension_semantics=("arbitrary", "arbitrary"),
                                             vmem_limit_bytes=VMEM_LIMIT),
        name="combine",
    )(dest_flat, s, mod, tw, ys)


def _rope_tables(seq_len):
    rows = seq_len // GRID_W
    row = jnp.broadcast_to(jnp.arange(rows, dtype=F32)[:, None], (rows, GRID_W)).reshape(seq_len)
    col = jnp.broadcast_to(jnp.arange(GRID_W, dtype=F32)[None, :], (rows, GRID_W)).reshape(seq_len)
    inv = ROPE_BASE ** (-jnp.arange(ROPE_PAIRS, dtype=F32) / ROPE_PAIRS)
    ang = jnp.stack([row, col], axis=0)[:, None, :] * inv[None, :, None]
    cos_l = jnp.cos(ang).reshape(2 * ROPE_PAIRS, seq_len)
    sin_l = jnp.sin(ang).reshape(2 * ROPE_PAIRS, seq_len)
    cos_t = jnp.concatenate([jnp.ones((2 * ROPE_PAIRS, CTX_LEN), F32), cos_l], axis=1)
    sin_t = jnp.concatenate([jnp.zeros((2 * ROPE_PAIRS, CTX_LEN), F32), sin_l], axis=1)
    return cos_t, sin_t


def kernel(x, c, ctx, c_ctx, w_mod, b_mod, norm1_g, norm2_g, w_in, q_norm_g, k_norm_g,
           lambda_q1, lambda_k1, lambda_q2, lambda_k2, subln_g, conv_w, pool_w, pool_scale, w_out,
           router_w, router_b, w_gate_up, b_gate_up, w_down, b_down):
    b, seq_len, d = x.shape
    assert d == SUBLANES * LANES
    depth = w_mod.shape[0]
    la = CTX_LEN + seq_len
    t_all = b * la

    rows = -(-(b + 1) // SUBLANES) * SUBLANES
    c_all = jnp.zeros((rows, d), F32).at[:b].set(c).at[b].set(c_ctx)
    mod = _mod_call(c_all, w_mod, b_mod).reshape(depth, rows, N_MOD, d)

    cos_t, sin_t = _rope_tables(seq_len)
    s = jnp.concatenate([ctx, x], axis=1)

    n_rows = -(-(t_all * TOP_K + N_EXPERTS * (EXPERT_ROWS - 1)) // EXPERT_ROWS) * EXPERT_ROWS
    nb = n_rows // EXPERT_ROWS

    for l in range(depth):
        last = l + 1 == depth
        lam_init = 0.8 - 0.6 * math.exp(-0.3 * l)
        wl = w_in[l]
        wm = wl[:, 2 * ATTN_W:].astype(BF16)
        wqk = wl[:, :2 * ATTN_W].T.astype(BF16)
        q_gain = jnp.tile(q_norm_g[l], ATTN_W // QK_DIM) * (LOG2E * QK_DIM ** -0.5)
        k_gain = jnp.tile(k_norm_g[l], ATTN_W // QK_DIM)
        gain = jnp.broadcast_to(jnp.concatenate([q_gain, k_gain])[:, None], (2 * ATTN_W, ROW_TILE))
        q, kt, v, cp = _in_call(s, mod[l], norm1_g[l][None], wm, wqk, gain, cos_t, sin_t)

        lam_p = jnp.zeros((SUBLANES, LANES), F32)
        lam_p = lam_p.at[0, :QK_DIM].set(lambda_q1[l]).at[1, :QK_DIM].set(lambda_k1[l])
        lam_p = lam_p.at[2, :QK_DIM].set(lambda_q2[l]).at[3, :QK_DIM].set(lambda_k2[l])
        sg = jnp.tile(subln_g[l], 2)[None]
        oa = _attn_call(lam_p, q, kt, v, sg, lam_init)

        pw = jax.scipy.linalg.block_diag(*[pool_w[l, g] for g in range(len(POOL_WINDOWS))]).astype(BF16)
        wr = jnp.zeros((d, LANES), F32).at[:, :N_EXPERTS].set(router_w[l])
        wrh, wrl = _split_bf16(wr)
        br = jnp.full((1, LANES), NEG_BIG, F32).at[0, :N_EXPERTS].set(router_b[l])
        s, hu, te, tw = _out_call(s, oa, cp, conv_w[l], pw, pool_scale[l][None], w_out[l].astype(BF16),
                                  mod[l], norm2_g[l][None], wrh, wrl, br, seq_len)

        dest, counts = _rank_call(te.reshape(t_all, TOP_K))
        cnt = counts[0, :N_EXPERTS]
        padded = (cnt + EXPERT_ROWS - 1) // EXPERT_ROWS * EXPERT_ROWS
        pad_end = jnp.cumsum(padded)
        n_used = (pad_end[-1] // EXPERT_ROWS).astype(I32)
        blk = jnp.minimum(jnp.arange(nb, dtype=I32), n_used - 1) * EXPERT_ROWS
        block_e = jnp.sum((pad_end[None, :] <= blk[:, None]).astype(I32), axis=1)
        block_e = jnp.minimum(block_e, N_EXPERTS - 1)
        dest_flat = dest.reshape(t_all * TOP_K)

        xs = _disp_call(pad_end - padded + cnt, padded - cnt, dest_flat,
                        hu.reshape(t_all, SUBLANES, LANES), n_rows)
        ys = _exp_call(block_e, n_used.reshape(1), xs.reshape(n_rows * SUBLANES, LANES),
                       w_gate_up[l].astype(BF16), b_gate_up[l], w_down[l].astype(BF16), b_down[l])
        s = _comb_call(dest_flat, s, mod[l], tw, ys.reshape(n_rows, SUBLANES, LANES), latent_only=last)
    return s
```

```python
import functools
import math

import jax
import jax.numpy as jnp
from jax import lax
from jax.experimental import pallas as pl
from jax.experimental.pallas import tpu as pltpu

F32 = jnp.float32
BF16 = jnp.bfloat16
I32 = jnp.int32
U32 = jnp.uint32

GRID_W = 64
CTX_LEN = 256
HEAD_DIM = 64
QK_DIM = 32
N_HEADS = 8
ATTN_W = 512
CONV_W = 256
POOL_W = 256
POOL_WINDOWS = (2, 4, 8, 16)
POOL_GC = 64
ROPE_BASE = 10000.0
ROPE_PAIRS = 8
N_EXPERTS = 32
TOP_K = 4
N_MOD = 6
EPS = 1e-6
SWIGLU_ALPHA = 1.702
SWIGLU_LIMIT = 7.0
LOG2E = 1.4426950408889634

LANES = 128
SUBLANES = 8
ROW_TILE = 256
N_PAIRS = N_HEADS // 2
POOL_HALO = 8
EXPERT_ROWS = 512
RANK_TILE = 512
KEY_UNROLL = 8
VMEM_LIMIT = 48 * 1024 * 1024
ATTN_VMEM_LIMIT = 40 * 1024 * 1024
EXPERT_VMEM_LIMIT = 58 * 1024 * 1024
NEG_BIG = -1e30
BOUND_SLACK = 1.0 + 2.0 ** -9
UNDERFLOW_GUARD = 2.0 ** -100


def _split_bf16(a):
    hi = a.astype(BF16)
    lo = (a - hi.astype(F32)).astype(BF16)
    return hi, lo


def _dot(a, b):
    return jnp.dot(a, b, preferred_element_type=F32)


def _mod_kernel(c_ref, w_ref, b_ref, o_ref):
    cv = c_ref[...]
    s = cv * jax.nn.sigmoid(cv)
    s_hi, s_lo = _split_bf16(s)
    w_hi, w_lo = _split_bf16(w_ref[0])
    o_ref[0] = _dot(s_hi, w_hi) + _dot(s_lo, w_hi) + _dot(s_hi, w_lo) + b_ref[0]


def _mod_call(c_all, w_mod, b_mod):
    depth, d, n = w_mod.shape
    r = c_all.shape[0]
    tn = 1536
    return pl.pallas_call(
        _mod_kernel,
        grid=(depth, n // tn),
        in_specs=[
            pl.BlockSpec((r, d), lambda l, j: (0, 0)),
            pl.BlockSpec((1, d, tn), lambda l, j: (l, 0, j)),
            pl.BlockSpec((1, 1, tn), lambda l, j: (l, 0, j)),
        ],
        out_specs=pl.BlockSpec((1, r, tn), lambda l, j: (l, 0, j)),
        out_shape=jax.ShapeDtypeStruct((depth, r, n), F32),
        compiler_params=pltpu.CompilerParams(vmem_limit_bytes=VMEM_LIMIT),
        name="mod",
    )(c_all, w_mod, b_mod.reshape(depth, 1, n))


def _rms_mod(x, g, shift, scale):
    y = x * lax.rsqrt(jnp.mean(x * x, axis=-1, keepdims=True) + EPS) * g
    return y * (1.0 + scale) + shift


def _in_kernel(x_ref, mod_ref, g1_ref, wm_ref, wqk_ref, gain_ref, cos_ref, sin_ref,
               q_ref, kt_ref, v_ref, cp_ref):
    tm = x_ref.shape[1]
    h = _rms_mod(x_ref[0], g1_ref[...], mod_ref[0, 0:1, :], mod_ref[0, 1:2, :])
    pm = _dot(h.astype(BF16), wm_ref[...])
    for p in range(N_PAIRS):
        v_ref[0, p] = pm[:, p * LANES:(p + 1) * LANES].astype(BF16)
    o = ATTN_W
    cp_ref[0, :, 0:CONV_W] = pm[:, o:o + CONV_W]
    cp_ref[0, :, CONV_W:2 * CONV_W] = pm[:, o + CONV_W:o + 2 * CONV_W] * pm[:, o + 2 * CONV_W:o + 3 * CONV_W]
    cp_ref[0, :, 2 * CONV_W:] = pm[:, o + 3 * CONV_W:]

    ht = h.T.astype(BF16)
    qkt = _dot(wqk_ref[...], ht)
    ng = 2 * ATTN_W // QK_DIM
    t = qkt.reshape(ng, 4, ROPE_PAIRS, tm)
    ss = jnp.sum(jnp.sum(t * t, axis=2, keepdims=True), axis=1, keepdims=True)
    tn = t * lax.rsqrt(ss * (1.0 / QK_DIM) + EPS) * gain_ref[...].reshape(ng, 4, ROPE_PAIRS, tm)
    cs = cos_ref[...].reshape(2, ROPE_PAIRS, tm)
    sn = sin_ref[...].reshape(2, ROPE_PAIRS, tm)
    parts = []
    for a in range(2):
        t1 = tn[:, 2 * a]
        t2 = tn[:, 2 * a + 1]
        parts.append(t1 * cs[a] - t2 * sn[a])
        parts.append(t2 * cs[a] + t1 * sn[a])
    rot = jnp.stack(parts, axis=1).reshape(2 * ATTN_W, tm)
    qt = rot[:ATTN_W].T
    for p in range(N_PAIRS):
        q_ref[0, p] = qt[:, p * LANES:(p + 1) * LANES].astype(BF16)
        kt_ref[0, 0, p] = rot[ATTN_W + p * LANES:ATTN_W + (p + 1) * LANES].astype(BF16)


def _in_call(s, mod, g1, wm, wqk, gain, cos_t, sin_t):
    b, la, d = s.shape
    nt = la // ROW_TILE
    nmain = wm.shape[1]

    def mod_idx(i, j):
        return (jnp.where(j == 0, b, i), 0, 0)

    return pl.pallas_call(
        _in_kernel,
        grid=(b, nt),
        in_specs=[
            pl.BlockSpec((1, ROW_TILE, d), lambda i, j: (i, j, 0)),
            pl.BlockSpec((1, N_MOD, d), mod_idx),
            pl.BlockSpec((1, d), lambda i, j: (0, 0)),
            pl.BlockSpec((d, nmain), lambda i, j: (0, 0)),
            pl.BlockSpec((2 * ATTN_W, d), lambda i, j: (0, 0)),
            pl.BlockSpec((2 * ATTN_W, ROW_TILE), lambda i, j: (0, 0)),
            pl.BlockSpec((2 * ROPE_PAIRS, ROW_TILE), lambda i, j: (0, j)),
            pl.BlockSpec((2 * ROPE_PAIRS, ROW_TILE), lambda i, j: (0, j)),
        ],
        out_specs=[
            pl.BlockSpec((1, N_PAIRS, ROW_TILE, LANES), lambda i, j: (i, 0, j, 0)),
            pl.BlockSpec((1, 1, N_PAIRS, LANES, ROW_TILE), lambda i, j: (i, j, 0, 0, 0)),
            pl.BlockSpec((1, N_PAIRS, ROW_TILE, LANES), lambda i, j: (i, 0, j, 0)),
            pl.BlockSpec((1, ROW_TILE, 3 * CONV_W), lambda i, j: (i, j, 0)),
        ],
        out_shape=[
            jax.ShapeDtypeStruct((b, N_PAIRS, la, LANES), BF16),
            jax.ShapeDtypeStruct((b, nt, N_PAIRS, LANES, ROW_TILE), BF16),
            jax.ShapeDtypeStruct((b, N_PAIRS, la, LANES), BF16),
            jax.ShapeDtypeStruct((b, la, 3 * CONV_W), F32),
        ],
        compiler_params=pltpu.CompilerParams(vmem_limit_bytes=VMEM_LIMIT),
        name="inproj",
    )(s, mod, g1, wm, wqk, gain, cos_t, sin_t)


def _attn_kernel(lam_ref, q_ref, kt_ref, v_ref, sg_ref, o_ref, kn_scr, mb_scr, l_scr, acc_scr, *, lam_init):
    tq = q_ref.shape[2]
    nkc_all = kt_ref.shape[1]
    j = pl.program_id(1)
    n_groups = jnp.where(j == 0, 0, (nkc_all - 1) // KEY_UNROLL)
    n_chunks = jnp.where(j == 0, 1, nkc_all)
    lp = lam_ref[...]
    lam = (jnp.exp(jnp.sum(lp[0:1] * lp[1:2], axis=-1, keepdims=True))
           - jnp.exp(jnp.sum(lp[2:3] * lp[3:4], axis=-1, keepdims=True)) + lam_init)
    lane = lax.broadcasted_iota(I32, (1, LANES), 1)
    low = lane < HEAD_DIM
    groups = [(lane >= g * QK_DIM) & (lane < (g + 1) * QK_DIM) for g in range(4)]

    @pl.when(j == 0)
    def _():
        for p in range(N_PAIRS):
            def key_norms(kc, mx, p=p):
                k = kt_ref[0, kc, p].astype(F32)
                k2 = jnp.sum((k * k).reshape(4, QK_DIM, ROW_TILE), axis=1)
                return jnp.maximum(mx, k2)
            mx = lax.fori_loop(0, nkc_all, key_norms, jnp.zeros((4, ROW_TILE), F32))
            kn = jnp.sqrt(jnp.max(mx, axis=-1, keepdims=True))
            kn_scr[p] = jnp.broadcast_to(jnp.concatenate([kn, kn], axis=0), (SUBLANES, LANES))

    for p in range(N_PAIRS):
        qp = q_ref[0, p]
        qs = jnp.concatenate([jnp.where(groups[g], qp, jnp.zeros_like(qp)) for g in range(4)], axis=0)
        qf = qp.astype(F32)
        q2 = qf * qf
        for g in range(4):
            qn = jnp.sqrt(jnp.sum(jnp.where(groups[g], q2, 0.0), axis=-1, keepdims=True))
            mb_scr[g * tq:(g + 1) * tq, :] = qn * kn_scr[p, g:g + 1, :] * BOUND_SLACK

        def exp_pv(kc, qs=qs, p=p):
            sc = _dot(qs, kt_ref[0, kc, p])
            mb = mb_scr[...]
            e0 = jnp.exp2(sc[:, :LANES] - mb)
            e1 = jnp.exp2(sc[:, LANES:] - mb)
            r0 = pl.multiple_of(kc * ROW_TILE, ROW_TILE)
            e = jnp.concatenate([e0.astype(BF16), e1.astype(BF16)], axis=1)
            return e0 + e1, _dot(e, v_ref[0, p, pl.ds(r0, ROW_TILE), :])

        def sweep(g, _):
            parts = [exp_pv(1 + g * KEY_UNROLL + u) for u in range(KEY_UNROLL)]
            ls = [a for a, _ in parts]
            pv = [b for _, b in parts]
            while len(ls) > 1:
                ls = [a + b for a, b in zip(ls[::2], ls[1::2])]
                pv = [a + b for a, b in zip(pv[::2], pv[1::2])]
            l_scr[...] = l_scr[...] + ls[0]
            acc_scr[...] = acc_scr[...] + pv[0]
            return 0

        l_scr[...], acc_scr[...] = exp_pv(0)
        lax.fori_loop(0, n_groups, sweep, 0)
        lmin = jnp.min(jnp.sum(l_scr[...], axis=-1, keepdims=True))

        @pl.when(jnp.logical_not(lmin >= UNDERFLOW_GUARD))
        def _():
            def row_max(kc, m, qs=qs, p=p):
                sc = _dot(qs, kt_ref[0, kc, p])
                return jnp.maximum(m, jnp.maximum(sc[:, :LANES], sc[:, LANES:]))
            m = lax.fori_loop(0, n_chunks, row_max, jnp.full((4 * tq, LANES), -jnp.inf, F32))
            mb_scr[...] = jnp.broadcast_to(jnp.max(m, axis=-1, keepdims=True), mb_scr.shape)
            l_scr[...] = jnp.zeros(l_scr.shape, F32)
            acc_scr[...] = jnp.zeros(acc_scr.shape, F32)

            def redo(kc, _):
                dl, dacc = exp_pv(kc)
                l_scr[...] = l_scr[...] + dl
                acc_scr[...] = acc_scr[...] + dacc
                return 0
            lax.fori_loop(0, n_chunks, redo, 0)

        on = acc_scr[...] / jnp.sum(l_scr[...], axis=-1, keepdims=True)
        heads = [on[(2 * hh) * tq:(2 * hh + 1) * tq] - lam * on[(2 * hh + 1) * tq:(2 * hh + 2) * tq]
                 for hh in range(2)]
        o = jnp.where(low, heads[0], heads[1])
        o2 = o * o
        ss_lo = jnp.sum(jnp.where(low, o2, 0.0), axis=-1, keepdims=True)
        ss_hi = jnp.sum(jnp.where(low, 0.0, o2), axis=-1, keepdims=True)
        inv = lax.rsqrt(jnp.where(low, ss_lo, ss_hi) * (1.0 / HEAD_DIM) + EPS)
        o_ref[0, p] = (o * inv * sg_ref[...] * (1.0 - lam_init)).astype(BF16)


def _attn_call(lam_p, q, kt, v, sg, lam_init):
    b, _, la, _ = q.shape
    nt = la // ROW_TILE
    assert (nt - 1) % KEY_UNROLL == 0
    return pl.pallas_call(
        functools.partial(_attn_kernel, lam_init=lam_init),
        grid=(b, nt),
        in_specs=[
            pl.BlockSpec((SUBLANES, LANES), lambda i, j: (0, 0)),
            pl.BlockSpec((1, N_PAIRS, ROW_TILE, LANES), lambda i, j: (i, 0, j, 0)),
            pl.BlockSpec((1, nt, N_PAIRS, LANES, ROW_TILE), lambda i, j: (i, 0, 0, 0, 0)),
            pl.BlockSpec((1, N_PAIRS, la, LANES), lambda i, j: (i, 0, 0, 0)),
            pl.BlockSpec((1, LANES), lambda i, j: (0, 0)),
        ],
        out_specs=pl.BlockSpec((1, N_PAIRS, ROW_TILE, LANES), lambda i, j: (i, 0, j, 0)),
        out_shape=jax.ShapeDtypeStruct((b, N_PAIRS, la, LANES), BF16),
        scratch_shapes=[pltpu.VMEM((N_PAIRS, SUBLANES, LANES), F32),
                        pltpu.VMEM((4 * ROW_TILE, LANES), F32),
                        pltpu.VMEM((4 * ROW_TILE, LANES), F32),
                        pltpu.VMEM((4 * ROW_TILE, LANES), F32)],
        compiler_params=pltpu.CompilerParams(vmem_limit_bytes=ATTN_VMEM_LIMIT),
        name="attn",
    )(lam_p, q, kt, v, sg)


def _store_row_tiles(ref, val):
    rows = val.shape[0]
    for c in range(SUBLANES):
        ref[pl.ds(c, rows, stride=SUBLANES), :] = val[:, c * LANES:(c + 1) * LANES]


def _load_row_tiles(ref, first_row, rows, c):
    return ref[pl.ds(first_row * SUBLANES + c, rows, stride=SUBLANES), :]


def _shift_rows(a, k):
    return pltpu.roll(a, k % a.shape[0], axis=0)


def _out_kernel(x_ref, oa_ref, cp_ref, cpp_ref, cpn_ref, cw_ref, pw_ref, ps_ref, wo_ref,
                mod_ref, g2_ref, wrh_ref, wrl_ref, br_ref,
                xo_ref, hu_ref, te_ref, tw_ref, *, seq_len):
    tm = x_ref.shape[1]
    d = x_ref.shape[2]
    j = pl.program_id(1)
    nt = pl.num_programs(1)
    halo = POOL_HALO
    has_prev = j >= 2
    has_next = (j >= 1) & (j < nt - 1)
    prev = jnp.where(has_prev, cpp_ref[0], 0.0)
    nxt = jnp.where(has_next, cpn_ref[0], 0.0)
    ext = jnp.concatenate([prev, cp_ref[0], nxt], axis=0)

    z = ext[:, CONV_W:2 * CONV_W]
    conv = (cw_ref[0:1, :] * _shift_rows(z, 1) + cw_ref[1:2, :] * z + cw_ref[2:3, :] * _shift_rows(z, -1))
    o_conv = ext[halo:halo + tm, 0:CONV_W] * conv[halo:halo + tm]

    u = ext[:, 2 * CONV_W:]
    a2 = u + _shift_rows(u, 1)
    a4 = _shift_rows(a2, -1) + _shift_rows(a2, 1)
    a8 = _shift_rows(a4, -2) + _shift_rows(a4, 2)
    a16 = _shift_rows(a8, -4) + _shift_rows(a8, 4)
    lane = lax.broadcasted_iota(I32, (tm, POOL_W), 1)
    grp = lane // POOL_GC
    wsum = jnp.where(grp == 0, a2[halo:halo + tm],
                     jnp.where(grp == 1, a4[halo:halo + tm],
                               jnp.where(grp == 2, a8[halo:halo + tm], a16[halo:halo + tm])))
    half = jnp.where(grp == 0, 1, jnp.where(grp == 1, 2, jnp.where(grp == 2, 4, 8)))
    row = lax.broadcasted_iota(I32, (tm, POOL_W), 0)
    pos = jnp.where(j == 0, row, (j - 1) * tm + row)
    n_seq = jnp.where(j == 0, CTX_LEN, seq_len)
    cnt = jnp.minimum(pos + half, n_seq) - jnp.maximum(pos - half, 0)
    dlt = wsum / cnt.astype(F32) - u[halo:halo + tm]
    o_pool = _dot(dlt.astype(BF16), pw_ref[...]) * ps_ref[...]

    mix = _dot(o_conv.astype(BF16), wo_ref[ATTN_W:ATTN_W + CONV_W, :])
    mix = mix + _dot(o_pool.astype(BF16), wo_ref[ATTN_W + CONV_W:, :])
    for p in range(N_PAIRS):
        mix = mix + _dot(oa_ref[0, p], wo_ref[p * LANES:(p + 1) * LANES, :])
    x = x_ref[0] + mod_ref[0, 2:3, :] * mix
    xo_ref[0] = x

    h2 = _rms_mod(x, g2_ref[...], mod_ref[0, 3:4, :], mod_ref[0, 4:5, :])
    _store_row_tiles(hu_ref.at[0], h2)

    h_hi, h_lo = _split_bf16(h2)
    logits = (_dot(h_hi, wrh_ref[...]) + _dot(h_lo, wrh_ref[...]) + _dot(h_hi, wrl_ref[...])
              + br_ref[...])
    lanef = lax.broadcasted_iota(I32, (tm, LANES), 1).astype(F32)
    work = logits
    tops = []
    for k in range(TOP_K):
        mk = jnp.max(work, axis=-1, keepdims=True)
        ik = jnp.min(jnp.where(work == mk, lanef, float(LANES)), axis=-1, keepdims=True)
        te_ref[0, :, k:k + 1] = ik.astype(I32)
        work = jnp.where(lanef == ik, -jnp.inf, work)
        tops.append(mk)
    es = [jnp.exp(mk - tops[0]) for mk in tops]
    den = es[0] + es[1] + es[2] + es[3]
    for k in range(TOP_K):
        tw_ref[0, :, k:k + 1] = es[k] / den


def _out_call(s, oa, cp, cw, pw, ps, wo, mod, g2, wrh, wrl, br, seq_len):
    b, la, d = s.shape
    nt = la // ROW_TILE
    hb = ROW_TILE // POOL_HALO
    nhb = la // POOL_HALO

    def mod_idx(i, j):
        return (jnp.where(j == 0, b, i), 0, 0)

    const2 = lambda i, j: (0, 0)
    return pl.pallas_call(
        functools.partial(_out_kernel, seq_len=seq_len),
        grid=(b, nt),
        in_specs=[
            pl.BlockSpec((1, ROW_TILE, d), lambda i, j: (i, j, 0)),
            pl.BlockSpec((1, N_PAIRS, ROW_TILE, LANES), lambda i, j: (i, 0, j, 0)),
            pl.BlockSpec((1, ROW_TILE, 3 * CONV_W), lambda i, j: (i, j, 0)),
            pl.BlockSpec((1, POOL_HALO, 3 * CONV_W), lambda i, j: (i, jnp.maximum(j * hb - 1, 0), 0)),
            pl.BlockSpec((1, POOL_HALO, 3 * CONV_W), lambda i, j: (i, jnp.minimum((j + 1) * hb, nhb - 1), 0)),
            pl.BlockSpec((3, CONV_W), const2),
            pl.BlockSpec((POOL_W, POOL_W), const2),
            pl.BlockSpec((1, POOL_W), const2),
            pl.BlockSpec((d, d), const2),
            pl.BlockSpec((1, N_MOD, d), mod_idx),
            pl.BlockSpec((1, d), const2),
            pl.BlockSpec((d, LANES), const2),
            pl.BlockSpec((d, LANES), const2),
            pl.BlockSpec((1, LANES), const2),
        ],
        out_specs=[
            pl.BlockSpec((1, ROW_TILE, d), lambda i, j: (i, j, 0)),
            pl.BlockSpec((1, ROW_TILE * SUBLANES, LANES), lambda i, j: (i, j, 0)),
            pl.BlockSpec((1, ROW_TILE, TOP_K), lambda i, j: (i, j, 0)),
            pl.BlockSpec((1, ROW_TILE, TOP_K), lambda i, j: (i, j, 0)),
        ],
        out_shape=[
            jax.ShapeDtypeStruct((b, la, d), F32),
            jax.ShapeDtypeStruct((b, la * SUBLANES, LANES), F32),
            jax.ShapeDtypeStruct((b, la, TOP_K), I32),
            jax.ShapeDtypeStruct((b, la, TOP_K), F32),
        ],
        compiler_params=pltpu.CompilerParams(vmem_limit_bytes=VMEM_LIMIT),
        name="outproj",
    )(s, oa, cp, cp, cp, cw, pw, ps, wo, mod, g2, wrh, wrl, br)


def _rank_kernel(te_ref, tri_ref, dest_ref, cnt_ref, carry):
    ph = pl.program_id(0)
    i = pl.program_id(1)
    tr = te_ref.shape[0]
    lane1 = lax.broadcasted_iota(I32, (SUBLANES, LANES), 1)

    @pl.when((ph == 0) & (i == 0))
    def _():
        carry[...] = jnp.zeros_like(carry)

    @pl.when((ph == 1) & (i == 0))
    def _():
        cnt = carry[...]
        cnt_ref[...] = cnt.astype(I32)
        padded = jnp.ceil(cnt * (1.0 / EXPERT_ROWS)) * EXPERT_ROWS
        incl = padded
        for sh in (1, 2, 4, 8, 16):
            incl = incl + jnp.where(lane1 >= sh, pltpu.roll(incl, sh, axis=1), 0.0)
        carry[...] = incl - padded

    e = te_ref[...]
    lane = lax.broadcasted_iota(I32, (tr, LANES), 1)
    ohs = [(e[:, k:k + 1] == lane).astype(F32) for k in range(TOP_K)]
    m = ohs[0] + ohs[1] + ohs[2] + ohs[3]
    base = _dot(tri_ref[...], m.astype(BF16)) + carry[0:1, :]
    for k in range(TOP_K):
        dest_ref[:, k:k + 1] = jnp.sum(ohs[k] * base, axis=-1, keepdims=True).astype(I32)
    carry[...] = carry[...] + jnp.sum(m, axis=0, keepdims=True)


def _rank_call(te):
    t = te.shape[0]
    nt = t // RANK_TILE
    r = lax.broadcasted_iota(I32, (RANK_TILE, RANK_TILE), 0)
    c = lax.broadcasted_iota(I32, (RANK_TILE, RANK_TILE), 1)
    tri = (c < r).astype(BF16)
    return pl.pallas_call(
        _rank_kernel,
        grid=(2, nt),
        in_specs=[
            pl.BlockSpec((RANK_TILE, TOP_K), lambda ph, i: (i, 0)),
            pl.BlockSpec((RANK_TILE, RANK_TILE), lambda ph, i: (0, 0)),
        ],
        out_specs=[
            pl.BlockSpec((RANK_TILE, TOP_K), lambda ph, i: (ph * i, 0)),
            pl.BlockSpec((SUBLANES, LANES), lambda ph, i: (0, 0)),
        ],
        out_shape=[
            jax.ShapeDtypeStruct((t, TOP_K), I32),
            jax.ShapeDtypeStruct((SUBLANES, LANES), I32),
        ],
        scratch_shapes=[pltpu.VMEM((SUBLANES, LANES), F32)],
        compiler_params=pltpu.CompilerParams(dimension_semantics=("arbitrary", "arbitrary")),
        name="rank",
    )(te, tri)


def _disp_kernel(pad0_ref, npad_ref, dest_ref, hu_ref, xs_ref, zero, sem, zsem):
    def row_copy(r, k):
        return pltpu.make_async_copy(hu_ref.at[r], xs_ref.at[dest_ref[r * TOP_K + k]], sem)

    def start(r, _):
        for k in range(TOP_K):
            row_copy(r, k).start()
        return 0

    def wait(r, _):
        for k in range(TOP_K):
            row_copy(r, k).wait()
        return 0

    lax.fori_loop(0, ROW_TILE, start, 0)
    lax.fori_loop(0, ROW_TILE, wait, 0)

    @pl.when(pl.program_id(0) == pl.num_programs(0) - 1)
    def _():
        zero[...] = jnp.zeros(zero.shape, F32)

        def pad_copy(e, r):
            return pltpu.make_async_copy(zero, xs_ref.at[pad0_ref[e] + r], zsem)

        def per_expert(e, _):
            def start_pad(r, c):
                pad_copy(e, r).start()
                return c

            def wait_pad(r, c):
                pad_copy(e, r).wait()
                return c

            lax.fori_loop(0, npad_ref[e], start_pad, 0)
            lax.fori_loop(0, npad_ref[e], wait_pad, 0)
            return 0

        lax.fori_loop(0, pad0_ref.shape[0], per_expert, 0)


def _disp_call(pad0, npad, dest_flat, hu, n_rows):
    t = hu.shape[0]
    grid_spec = pltpu.PrefetchScalarGridSpec(
        num_scalar_prefetch=2,
        grid=(t // ROW_TILE,),
        in_specs=[
            pl.BlockSpec((ROW_TILE * TOP_K,), lambda i, p0, n: (i,), memory_space=pltpu.SMEM),
            pl.BlockSpec((ROW_TILE, SUBLANES, LANES), lambda i, p0, n: (i, 0, 0)),
        ],
        out_specs=pl.BlockSpec(memory_space=pl.ANY),
        scratch_shapes=[pltpu.VMEM((SUBLANES, LANES), F32), pltpu.SemaphoreType.DMA(()),
                        pltpu.SemaphoreType.DMA(())],
    )
    return pl.pallas_call(
        _disp_kernel,
        grid_spec=grid_spec,
        out_shape=jax.ShapeDtypeStruct((n_rows, SUBLANES, LANES), F32),
        compiler_params=pltpu.CompilerParams(dimension_semantics=("arbitrary",)),
        name="dispatch",
    )(pad0, npad, dest_flat, hu)


def _exp_kernel(be_ref, nu_ref, xs_ref, wgu_ref, bgu_ref, wdn_ref, bdn_ref, ys_ref, wgu_b, wdn_b):
    i = pl.program_id(0)
    ff = wdn_ref.shape[1]
    rows = xs_ref.shape[0] // SUBLANES

    @pl.when((i == 0) | (be_ref[i] != be_ref[jnp.maximum(i - 1, 0)]))
    def _():
        wgu_b[...] = wgu_ref[0].astype(BF16)
        wdn_b[...] = wdn_ref[0].astype(BF16)

    @pl.when(i < nu_ref[0])
    def _():
        xb = jnp.concatenate([_load_row_tiles(xs_ref, 0, rows, c).astype(BF16) for c in range(SUBLANES)],
                             axis=1)
        gu = _dot(xb, wgu_b[...]) + bgu_ref[0]
        glu = jnp.minimum(gu[:, :ff], SWIGLU_LIMIT)
        lin = jnp.clip(gu[:, ff:], -SWIGLU_LIMIT, SWIGLU_LIMIT)
        act = glu * jax.nn.sigmoid(SWIGLU_ALPHA * glu) * (lin + 1.0)
        _store_row_tiles(ys_ref, _dot(act.astype(BF16), wdn_b[...]) + bdn_ref[0])

    @pl.when(i >= nu_ref[0])
    def _():
        ys_ref[...] = jnp.zeros_like(ys_ref)


def _exp_call(block_e, n_used, xs, layer, wgu, bgu, wdn, bdn):
    n_rows = xs.shape[0] // SUBLANES
    _, _, d, ff2 = wgu.shape
    ff = ff2 // 2
    nb = n_rows // EXPERT_ROWS
    grid_spec = pltpu.PrefetchScalarGridSpec(
        num_scalar_prefetch=2,
        grid=(nb,),
        in_specs=[
            pl.BlockSpec((EXPERT_ROWS * SUBLANES, LANES), lambda i, be, nu: (jnp.minimum(i, nu[0] - 1), 0)),
            pl.BlockSpec((None, 1, d, ff2), lambda i, be, nu: (layer, be[i], 0, 0)),
            pl.BlockSpec((1, 1, ff2), lambda i, be, nu: (be[i], 0, 0)),
            pl.BlockSpec((None, 1, ff, d), lambda i, be, nu: (layer, be[i], 0, 0)),
            pl.BlockSpec((1, 1, d), lambda i, be, nu: (be[i], 0, 0)),
        ],
        out_specs=pl.BlockSpec((EXPERT_ROWS * SUBLANES, LANES), lambda i, be, nu: (i, 0)),
        scratch_shapes=[pltpu.VMEM((d, ff2), BF16), pltpu.VMEM((ff, d), BF16)],
    )
    return pl.pallas_call(
        _exp_kernel,
        grid_spec=grid_spec,
        out_shape=jax.ShapeDtypeStruct((n_rows * SUBLANES, LANES), F32),
        compiler_params=pltpu.CompilerParams(dimension_semantics=("arbitrary",),
                                             vmem_limit_bytes=EXPERT_VMEM_LIMIT),
        name="experts",
    )(block_e, n_used, xs, wgu, bgu.reshape(-1, 1, ff2), wdn, bdn.reshape(-1, 1, d))


def _comb_kernel(dest_ref, x_ref, mod_ref, tw_ref, ys_ref, o_ref, buf, sem):
    def row_copy(r, k):
        r0 = pl.multiple_of((k * ROW_TILE + r) * SUBLANES, SUBLANES)
        return pltpu.make_async_copy(ys_ref.at[dest_ref[r * TOP_K + k]], buf.at[pl.ds(r0, SUBLANES)], sem)

    def start(r, _):
        for k in range(TOP_K):
            row_copy(r, k).start()
        return 0

    def wait(r, _):
        for k in range(TOP_K):
            row_copy(r, k).wait()
        return 0

    lax.fori_loop(0, ROW_TILE, start, 0)
    lax.fori_loop(0, ROW_TILE, wait, 0)
    w = [tw_ref[0, :, k:k + 1] for k in range(TOP_K)]
    for c in range(SUBLANES):
        acc = _load_row_tiles(buf, 0, ROW_TILE, c) * w[0]
        for k in range(1, TOP_K):
            acc = acc + _load_row_tiles(buf, k * ROW_TILE, ROW_TILE, c) * w[k]
        cols = slice(c * LANES, (c + 1) * LANES)
        o_ref[0, :, cols] = x_ref[0, :, cols] + mod_ref[0, 5:6, cols] * acc


def _comb_call(dest_flat, s, mod, tw, ys, latent_only):
    b, la, d = s.shape
    nt = la // ROW_TILE
    off = 1 if latent_only else 0

    def mod_idx(i, j):
        return (jnp.where(j + off == 0, b, i), 0, 0)

    return pl.pallas_call(
        _comb_kernel,
        grid=(b, nt - off),
        in_specs=[
            pl.BlockSpec((ROW_TILE * TOP_K,), lambda i, j: (i * nt + j + off,), memory_space=pltpu.SMEM),
            pl.BlockSpec((1, ROW_TILE, d), lambda i, j: (i, j + off, 0)),
            pl.BlockSpec((1, N_MOD, d), mod_idx),
            pl.BlockSpec((1, ROW_TILE, TOP_K), lambda i, j: (i, j + off, 0)),
            pl.BlockSpec(memory_space=pl.ANY),
        ],
        out_specs=pl.BlockSpec((1, ROW_TILE, d), lambda i, j: (i, j, 0)),
        out_shape=jax.ShapeDtypeStruct((b, la - off * ROW_TILE, d), F32),
        scratch_shapes=[pltpu.VMEM((TOP_K * ROW_TILE * SUBLANES, LANES), F32), pltpu.SemaphoreType.DMA(())],
        compiler_params=pltpu.CompilerParams(dimension_semantics=("arbitrary", "arbitrary"),
                                             vmem_limit_bytes=VMEM_LIMIT),
        name="combine",
    )(dest_flat, s, mod, tw, ys)


def _rope_tables(seq_len):
    rows = seq_len // GRID_W
    row = jnp.broadcast_to(jnp.arange(rows, dtype=F32)[:, None], (rows, GRID_W)).reshape(seq_len)
    col = jnp.broadcast_to(jnp.arange(GRID_W, dtype=F32)[None, :], (rows, GRID_W)).reshape(seq_len)
    inv = ROPE_BASE ** (-jnp.arange(ROPE_PAIRS, dtype=F32) / ROPE_PAIRS)
    ang = jnp.stack([row, col], axis=0)[:, None, :] * inv[None, :, None]
    cos_l = jnp.cos(ang).reshape(2 * ROPE_PAIRS, seq_len)
    sin_l = jnp.sin(ang).reshape(2 * ROPE_PAIRS, seq_len)
    cos_t = jnp.concatenate([jnp.ones((2 * ROPE_PAIRS, CTX_LEN), F32), cos_l], axis=1)
    sin_t = jnp.concatenate([jnp.zeros((2 * ROPE_PAIRS, CTX_LEN), F32), sin_l], axis=1)
    return cos_t, sin_t


def kernel(x, c, ctx, c_ctx, w_mod, b_mod, norm1_g, norm2_g, w_in, q_norm_g, k_norm_g,
           lambda_q1, lambda_k1, lambda_q2, lambda_k2, subln_g, conv_w, pool_w, pool_scale, w_out,
           router_w, router_b, w_gate_up, b_gate_up, w_down, b_down):
    b, seq_len, d = x.shape
    assert d == SUBLANES * LANES
    depth = w_mod.shape[0]
    la = CTX_LEN + seq_len
    t_all = b * la

    rows = -(-(b + 1) // SUBLANES) * SUBLANES
    c_all = jnp.zeros((rows, d), F32).at[:b].set(c).at[b].set(c_ctx)
    mod = _mod_call(c_all, w_mod, b_mod).reshape(depth, rows, N_MOD, d)

    cos_t, sin_t = _rope_tables(seq_len)
    s = jnp.concatenate([ctx, x], axis=1)

    n_rows = -(-(t_all * TOP_K + N_EXPERTS * (EXPERT_ROWS - 1)) // EXPERT_ROWS) * EXPERT_ROWS
    nb = n_rows // EXPERT_ROWS

    for l in range(depth):
        last = l + 1 == depth
        lam_init = 0.8 - 0.6 * math.exp(-0.3 * l)
        wl = w_in[l]
        wm = wl[:, 2 * ATTN_W:].astype(BF16)
        wqk = wl[:, :2 * ATTN_W].T.astype(BF16)
        q_gain = jnp.tile(q_norm_g[l], ATTN_W // QK_DIM) * (LOG2E * QK_DIM ** -0.5)
        k_gain = jnp.tile(k_norm_g[l], ATTN_W // QK_DIM)
        gain = jnp.broadcast_to(jnp.concatenate([q_gain, k_gain])[:, None], (2 * ATTN_W, ROW_TILE))
        q, kt, v, cp = _in_call(s, mod[l], norm1_g[l][None], wm, wqk, gain, cos_t, sin_t)

        lam_p = jnp.zeros((SUBLANES, LANES), F32)
        lam_p = lam_p.at[0, :QK_DIM].set(lambda_q1[l]).at[1, :QK_DIM].set(lambda_k1[l])
        lam_p = lam_p.at[2, :QK_DIM].set(lambda_q2[l]).at[3, :QK_DIM].set(lambda_k2[l])
        sg = jnp.tile(subln_g[l], 2)[None]
        oa = _attn_call(lam_p, q, kt, v, sg, lam_init)

        pw = jax.scipy.linalg.block_diag(*[pool_w[l, g] for g in range(len(POOL_WINDOWS))]).astype(BF16)
        wr = jnp.zeros((d, LANES), F32).at[:, :N_EXPERTS].set(router_w[l])
        wrh, wrl = _split_bf16(wr)
        br = jnp.full((1, LANES), NEG_BIG, F32).at[0, :N_EXPERTS].set(router_b[l])
        s, hu, te, tw = _out_call(s, oa, cp, conv_w[l], pw, pool_scale[l][None], w_out[l].astype(BF16),
                                  mod[l], norm2_g[l][None], wrh, wrl, br, seq_len)

        dest, counts = _rank_call(te.reshape(t_all, TOP_K))
        cnt = counts[0, :N_EXPERTS]
        padded = (cnt + EXPERT_ROWS - 1) // EXPERT_ROWS * EXPERT_ROWS
        pad_end = jnp.cumsum(padded)
        n_used = (pad_end[-1] // EXPERT_ROWS).astype(I32)
        blk = jnp.minimum(jnp.arange(nb, dtype=I32), n_used - 1) * EXPERT_ROWS
        block_e = jnp.sum((pad_end[None, :] <= blk[:, None]).astype(I32), axis=1)
        block_e = jnp.minimum(block_e, N_EXPERTS - 1)
        dest_flat = dest.reshape(t_all * TOP_K)

        pad0 = jnp.concatenate([pad_end - padded + cnt, pad_end[-1:]])
        npad = jnp.concatenate([padded - cnt, n_rows - pad_end[-1:]])
        xs = _disp_call(pad0, npad, dest_flat, hu.reshape(t_all, SUBLANES, LANES), n_rows)
        ys = _exp_call(block_e, n_used.reshape(1), xs.reshape(n_rows * SUBLANES, LANES), l,
                       w_gate_up, b_gate_up[l], w_down, b_down[l])
        s = _comb_call(dest_flat, s, mod[l], tw, ys.reshape(n_rows, SUBLANES, LANES), latent_only=last)
    return s
```

```python
import functools
import math

import jax
import jax.numpy as jnp
from jax import lax
from jax.experimental import pallas as pl
from jax.experimental.pallas import tpu as pltpu

F32 = jnp.float32
BF16 = jnp.bfloat16
I32 = jnp.int32
U32 = jnp.uint32

GRID_W = 64
CTX_LEN = 256
HEAD_DIM = 64
QK_DIM = 32
N_HEADS = 8
ATTN_W = 512
CONV_W = 256
POOL_W = 256
POOL_WINDOWS = (2, 4, 8, 16)
POOL_GC = 64
ROPE_BASE = 10000.0
ROPE_PAIRS = 8
N_EXPERTS = 32
TOP_K = 4
N_MOD = 6
EPS = 1e-6
SWIGLU_ALPHA = 1.702
SWIGLU_LIMIT = 7.0
LOG2E = 1.4426950408889634

LANES = 128
SUBLANES = 8
ROW_TILE = 256
N_PAIRS = N_HEADS // 2
POOL_HALO = 8
EXPERT_ROWS = 512
RANK_TILE = 512
ZERO_RUN = 64
KEY_UNROLL = 8
VMEM_LIMIT = 48 * 1024 * 1024
ATTN_VMEM_LIMIT = 40 * 1024 * 1024
EXPERT_VMEM_LIMIT = 58 * 1024 * 1024
NEG_BIG = -1e30
BOUND_SLACK = 1.0 + 2.0 ** -9
UNDERFLOW_GUARD = 2.0 ** -100


def _split_bf16(a):
    hi = a.astype(BF16)
    lo = (a - hi.astype(F32)).astype(BF16)
    return hi, lo


def _dot(a, b):
    return jnp.dot(a, b, preferred_element_type=F32)


def _mod_kernel(c_ref, w_ref, b_ref, o_ref):
    cv = c_ref[...]
    s = cv * jax.nn.sigmoid(cv)
    s_hi, s_lo = _split_bf16(s)
    w_hi, w_lo = _split_bf16(w_ref[0])
    o_ref[0] = _dot(s_hi, w_hi) + _dot(s_lo, w_hi) + _dot(s_hi, w_lo) + b_ref[0]


def _mod_call(c_all, w_mod, b_mod):
    depth, d, n = w_mod.shape
    r = c_all.shape[0]
    tn = 1536
    return pl.pallas_call(
        _mod_kernel,
        grid=(depth, n // tn),
        in_specs=[
            pl.BlockSpec((r, d), lambda l, j: (0, 0)),
            pl.BlockSpec((1, d, tn), lambda l, j: (l, 0, j)),
            pl.BlockSpec((1, 1, tn), lambda l, j: (l, 0, j)),
        ],
        out_specs=pl.BlockSpec((1, r, tn), lambda l, j: (l, 0, j)),
        out_shape=jax.ShapeDtypeStruct((depth, r, n), F32),
        compiler_params=pltpu.CompilerParams(vmem_limit_bytes=VMEM_LIMIT),
        name="mod",
    )(c_all, w_mod, b_mod.reshape(depth, 1, n))


def _rms_mod(x, g, shift, scale):
    y = x * lax.rsqrt(jnp.mean(x * x, axis=-1, keepdims=True) + EPS) * g
    return y * (1.0 + scale) + shift


def _in_kernel(x_ref, mod_ref, g1_ref, wm_ref, wqk_ref, gain_ref, cos_ref, sin_ref,
               q_ref, kt_ref, v_ref, cp_ref):
    tm = x_ref.shape[1]
    h = _rms_mod(x_ref[0], g1_ref[...], mod_ref[0, 0:1, :], mod_ref[0, 1:2, :])
    pm = _dot(h.astype(BF16), wm_ref[...])
    for p in range(N_PAIRS):
        v_ref[0, p] = pm[:, p * LANES:(p + 1) * LANES].astype(BF16)
    o = ATTN_W
    cp_ref[0, :, 0:CONV_W] = pm[:, o:o + CONV_W]
    cp_ref[0, :, CONV_W:2 * CONV_W] = pm[:, o + CONV_W:o + 2 * CONV_W] * pm[:, o + 2 * CONV_W:o + 3 * CONV_W]
    cp_ref[0, :, 2 * CONV_W:] = pm[:, o + 3 * CONV_W:]

    ht = h.T.astype(BF16)
    qkt = _dot(wqk_ref[...], ht)
    ng = 2 * ATTN_W // QK_DIM
    t = qkt.reshape(ng, 4, ROPE_PAIRS, tm)
    ss = jnp.sum(jnp.sum(t * t, axis=2, keepdims=True), axis=1, keepdims=True)
    tn = t * lax.rsqrt(ss * (1.0 / QK_DIM) + EPS) * gain_ref[...].reshape(ng, 4, ROPE_PAIRS, tm)
    cs = cos_ref[...].reshape(2, ROPE_PAIRS, tm)
    sn = sin_ref[...].reshape(2, ROPE_PAIRS, tm)
    parts = []
    for a in range(2):
        t1 = tn[:, 2 * a]
        t2 = tn[:, 2 * a + 1]
        parts.append(t1 * cs[a] - t2 * sn[a])
        parts.append(t2 * cs[a] + t1 * sn[a])
    rot = jnp.stack(parts, axis=1).reshape(2 * ATTN_W, tm)
    qt = rot[:ATTN_W].T
    for p in range(N_PAIRS):
        q_ref[0, p] = qt[:, p * LANES:(p + 1) * LANES].astype(BF16)
        kt_ref[0, 0, p] = rot[ATTN_W + p * LANES:ATTN_W + (p + 1) * LANES].astype(BF16)


def _in_call(s, mod, g1, wm, wqk, gain, cos_t, sin_t):
    b, la, d = s.shape
    nt = la // ROW_TILE
    nmain = wm.shape[1]

    def mod_idx(i, j):
        return (jnp.where(j == 0, b, i), 0, 0)

    return pl.pallas_call(
        _in_kernel,
        grid=(b, nt),
        in_specs=[
            pl.BlockSpec((1, ROW_TILE, d), lambda i, j: (i, j, 0)),
            pl.BlockSpec((1, N_MOD, d), mod_idx),
            pl.BlockSpec((1, d), lambda i, j: (0, 0)),
            pl.BlockSpec((d, nmain), lambda i, j: (0, 0)),
            pl.BlockSpec((2 * ATTN_W, d), lambda i, j: (0, 0)),
            pl.BlockSpec((2 * ATTN_W, ROW_TILE), lambda i, j: (0, 0)),
            pl.BlockSpec((2 * ROPE_PAIRS, ROW_TILE), lambda i, j: (0, j)),
            pl.BlockSpec((2 * ROPE_PAIRS, ROW_TILE), lambda i, j: (0, j)),
        ],
        out_specs=[
            pl.BlockSpec((1, N_PAIRS, ROW_TILE, LANES), lambda i, j: (i, 0, j, 0)),
            pl.BlockSpec((1, 1, N_PAIRS, LANES, ROW_TILE), lambda i, j: (i, j, 0, 0, 0)),
            pl.BlockSpec((1, N_PAIRS, ROW_TILE, LANES), lambda i, j: (i, 0, j, 0)),
            pl.BlockSpec((1, ROW_TILE, 3 * CONV_W), lambda i, j: (i, j, 0)),
        ],
        out_shape=[
            jax.ShapeDtypeStruct((b, N_PAIRS, la, LANES), BF16),
            jax.ShapeDtypeStruct((b, nt, N_PAIRS, LANES, ROW_TILE), BF16),
            jax.ShapeDtypeStruct((b, N_PAIRS, la, LANES), BF16),
            jax.ShapeDtypeStruct((b, la, 3 * CONV_W), F32),
        ],
        compiler_params=pltpu.CompilerParams(vmem_limit_bytes=VMEM_LIMIT),
        name="inproj",
    )(s, mod, g1, wm, wqk, gain, cos_t, sin_t)


def _attn_kernel(lam_ref, q_ref, kt_ref, v_ref, sg_ref, o_ref, kn_scr, mb_scr, l_scr, acc_scr, *,
                 lam_init, first_tile):
    tq = q_ref.shape[2]
    nkc_all = kt_ref.shape[1]
    j = pl.program_id(1) + first_tile
    n_groups = jnp.where(j == 0, 0, (nkc_all - 1) // KEY_UNROLL)
    n_chunks = jnp.where(j == 0, 1, nkc_all)
    lp = lam_ref[...]
    lam = (jnp.exp(jnp.sum(lp[0:1] * lp[1:2], axis=-1, keepdims=True))
           - jnp.exp(jnp.sum(lp[2:3] * lp[3:4], axis=-1, keepdims=True)) + lam_init)
    lane = lax.broadcasted_iota(I32, (1, LANES), 1)
    low = lane < HEAD_DIM
    groups = [(lane >= g * QK_DIM) & (lane < (g + 1) * QK_DIM) for g in range(4)]

    @pl.when(j == first_tile)
    def _():
        for p in range(N_PAIRS):
            def key_norms(kc, mx, p=p):
                k = kt_ref[0, kc, p].astype(F32)
                k2 = jnp.sum((k * k).reshape(4, QK_DIM, ROW_TILE), axis=1)
                return jnp.maximum(mx, k2)
            mx = lax.fori_loop(0, nkc_all, key_norms, jnp.zeros((4, ROW_TILE), F32))
            kn = jnp.sqrt(jnp.max(mx, axis=-1, keepdims=True))
            kn_scr[p] = jnp.broadcast_to(jnp.concatenate([kn, kn], axis=0), (SUBLANES, LANES))

    for p in range(N_PAIRS):
        qp = q_ref[0, p]
        qs = jnp.concatenate([jnp.where(groups[g], qp, jnp.zeros_like(qp)) for g in range(4)], axis=0)
        qf = qp.astype(F32)
        q2 = qf * qf
        for g in range(4):
            qn = jnp.sqrt(jnp.sum(jnp.where(groups[g], q2, 0.0), axis=-1, keepdims=True))
            mb_scr[g * tq:(g + 1) * tq, :] = qn * kn_scr[p, g:g + 1, :] * BOUND_SLACK

        def exp_pv(kc, qs=qs, p=p):
            sc = _dot(qs, kt_ref[0, kc, p])
            mb = mb_scr[...]
            e0 = jnp.exp2(sc[:, :LANES] - mb)
            e1 = jnp.exp2(sc[:, LANES:] - mb)
            r0 = pl.multiple_of(kc * ROW_TILE, ROW_TILE)
            e = jnp.concatenate([e0.astype(BF16), e1.astype(BF16)], axis=1)
            return e0 + e1, _dot(e, v_ref[0, p, pl.ds(r0, ROW_TILE), :])

        def sweep(g, _):
            parts = [exp_pv(1 + g * KEY_UNROLL + u) for u in range(KEY_UNROLL)]
            ls = [a for a, _ in parts]
            pv = [b for _, b in parts]
            while len(ls) > 1:
                ls = [a + b for a, b in zip(ls[::2], ls[1::2])]
                pv = [a + b for a, b in zip(pv[::2], pv[1::2])]
            l_scr[...] = l_scr[...] + ls[0]
            acc_scr[...] = acc_scr[...] + pv[0]
            return 0

        l_scr[...], acc_scr[...] = exp_pv(0)
        lax.fori_loop(0, n_groups, sweep, 0)
        lmin = jnp.min(jnp.sum(l_scr[...], axis=-1, keepdims=True))

        @pl.when(jnp.logical_not(lmin >= UNDERFLOW_GUARD))
        def _():
            def row_max(kc, m, qs=qs, p=p):
                sc = _dot(qs, kt_ref[0, kc, p])
                return jnp.maximum(m, jnp.maximum(sc[:, :LANES], sc[:, LANES:]))
            m = lax.fori_loop(0, n_chunks, row_max, jnp.full((4 * tq, LANES), -jnp.inf, F32))
            mb_scr[...] = jnp.broadcast_to(jnp.max(m, axis=-1, keepdims=True), mb_scr.shape)
            l_scr[...] = jnp.zeros(l_scr.shape, F32)
            acc_scr[...] = jnp.zeros(acc_scr.shape, F32)

            def redo(kc, _):
                dl, dacc = exp_pv(kc)
                l_scr[...] = l_scr[...] + dl
                acc_scr[...] = acc_scr[...] + dacc
                return 0
            lax.fori_loop(0, n_chunks, redo, 0)

        on = acc_scr[...] / jnp.sum(l_scr[...], axis=-1, keepdims=True)
        heads = [on[(2 * hh) * tq:(2 * hh + 1) * tq] - lam * on[(2 * hh + 1) * tq:(2 * hh + 2) * tq]
                 for hh in range(2)]
        o = jnp.where(low, heads[0], heads[1])
        o2 = o * o
        ss_lo = jnp.sum(jnp.where(low, o2, 0.0), axis=-1, keepdims=True)
        ss_hi = jnp.sum(jnp.where(low, 0.0, o2), axis=-1, keepdims=True)
        inv = lax.rsqrt(jnp.where(low, ss_lo, ss_hi) * (1.0 / HEAD_DIM) + EPS)
        o_ref[0, p] = (o * inv * sg_ref[...] * (1.0 - lam_init)).astype(BF16)


def _attn_call(lam_p, q, kt, v, sg, lam_init, first_tile):
    b, _, la, _ = q.shape
    nt = la // ROW_TILE
    assert (nt - 1) % KEY_UNROLL == 0
    off = first_tile
    return pl.pallas_call(
        functools.partial(_attn_kernel, lam_init=lam_init, first_tile=off),
        grid=(b, nt - off),
        in_specs=[
            pl.BlockSpec((SUBLANES, LANES), lambda i, j: (0, 0)),
            pl.BlockSpec((1, N_PAIRS, ROW_TILE, LANES), lambda i, j: (i, 0, j + off, 0)),
            pl.BlockSpec((1, nt, N_PAIRS, LANES, ROW_TILE), lambda i, j: (i, 0, 0, 0, 0)),
            pl.BlockSpec((1, N_PAIRS, la, LANES), lambda i, j: (i, 0, 0, 0)),
            pl.BlockSpec((1, LANES), lambda i, j: (0, 0)),
        ],
        out_specs=pl.BlockSpec((1, N_PAIRS, ROW_TILE, LANES), lambda i, j: (i, 0, j, 0)),
        out_shape=jax.ShapeDtypeStruct((b, N_PAIRS, la - off * ROW_TILE, LANES), BF16),
        scratch_shapes=[pltpu.VMEM((N_PAIRS, SUBLANES, LANES), F32),
                        pltpu.VMEM((4 * ROW_TILE, LANES), F32),
                        pltpu.VMEM((4 * ROW_TILE, LANES), F32),
                        pltpu.VMEM((4 * ROW_TILE, LANES), F32)],
        compiler_params=pltpu.CompilerParams(vmem_limit_bytes=ATTN_VMEM_LIMIT),
        name="attn",
    )(lam_p, q, kt, v, sg)


def _store_row_tiles(ref, val):
    rows = val.shape[0]
    for c in range(SUBLANES):
        ref[pl.ds(c, rows, stride=SUBLANES), :] = val[:, c * LANES:(c + 1) * LANES]


def _load_row_tiles(ref, first_row, rows, c):
    return ref[pl.ds(first_row * SUBLANES + c, rows, stride=SUBLANES), :]


def _shift_rows(a, k):
    return pltpu.roll(a, k % a.shape[0], axis=0)


def _out_kernel(x_ref, oa_ref, cp_ref, cpp_ref, cpn_ref, cw_ref, pw_ref, ps_ref, wo_ref,
                mod_ref, g2_ref, wrh_ref, wrl_ref, br_ref,
                xo_ref, hu_ref, te_ref, tw_ref, *, seq_len, first_tile):
    tm = x_ref.shape[1]
    j = pl.program_id(1) + first_tile
    nt = pl.num_programs(1) + first_tile
    halo = POOL_HALO
    has_prev = j >= 2
    has_next = (j >= 1) & (j < nt - 1)
    prev = jnp.where(has_prev, cpp_ref[0], 0.0)
    nxt = jnp.where(has_next, cpn_ref[0], 0.0)
    ext = jnp.concatenate([prev, cp_ref[0], nxt], axis=0)

    z = ext[:, CONV_W:2 * CONV_W]
    conv = (cw_ref[0:1, :] * _shift_rows(z, 1) + cw_ref[1:2, :] * z + cw_ref[2:3, :] * _shift_rows(z, -1))
    o_conv = ext[halo:halo + tm, 0:CONV_W] * conv[halo:halo + tm]

    u = ext[:, 2 * CONV_W:]
    a2 = u + _shift_rows(u, 1)
    a4 = _shift_rows(a2, -1) + _shift_rows(a2, 1)
    a8 = _shift_rows(a4, -2) + _shift_rows(a4, 2)
    a16 = _shift_rows(a8, -4) + _shift_rows(a8, 4)
    lane = lax.broadcasted_iota(I32, (tm, POOL_W), 1)
    grp = lane // POOL_GC
    wsum = jnp.where(grp == 0, a2[halo:halo + tm],
                     jnp.where(grp == 1, a4[halo:halo + tm],
                               jnp.where(grp == 2, a8[halo:halo + tm], a16[halo:halo + tm])))
    half = jnp.where(grp == 0, 1, jnp.where(grp == 1, 2, jnp.where(grp == 2, 4, 8)))
    row = lax.broadcasted_iota(I32, (tm, POOL_W), 0)
    pos = jnp.where(j == 0, row, (j - 1) * tm + row)
    n_seq = jnp.where(j == 0, CTX_LEN, seq_len)
    cnt = jnp.minimum(pos + half, n_seq) - jnp.maximum(pos - half, 0)
    dlt = wsum / cnt.astype(F32) - u[halo:halo + tm]
    o_pool = _dot(dlt.astype(BF16), pw_ref[...]) * ps_ref[...]

    mix = _dot(o_conv.astype(BF16), wo_ref[ATTN_W:ATTN_W + CONV_W, :])
    mix = mix + _dot(o_pool.astype(BF16), wo_ref[ATTN_W + CONV_W:, :])
    for p in range(N_PAIRS):
        mix = mix + _dot(oa_ref[0, p], wo_ref[p * LANES:(p + 1) * LANES, :])
    x = x_ref[0] + mod_ref[0, 2:3, :] * mix
    xo_ref[0] = x

    h2 = _rms_mod(x, g2_ref[...], mod_ref[0, 3:4, :], mod_ref[0, 4:5, :])
    _store_row_tiles(hu_ref.at[0], h2)

    h_hi, h_lo = _split_bf16(h2)
    logits = (_dot(h_hi, wrh_ref[...]) + _dot(h_lo, wrh_ref[...]) + _dot(h_hi, wrl_ref[...])
              + br_ref[...])
    lanef = lax.broadcasted_iota(I32, (tm, LANES), 1).astype(F32)
    work = logits
    tops = []
    for k in range(TOP_K):
        mk = jnp.max(work, axis=-1, keepdims=True)
        ik = jnp.min(jnp.where(work == mk, lanef, float(LANES)), axis=-1, keepdims=True)
        te_ref[0, :, k:k + 1] = ik.astype(I32)
        work = jnp.where(lanef == ik, -jnp.inf, work)
        tops.append(mk)
    es = [jnp.exp(mk - tops[0]) for mk in tops]
    den = es[0] + es[1] + es[2] + es[3]
    for k in range(TOP_K):
        tw_ref[0, :, k:k + 1] = es[k] / den


def _out_call(s, oa, cp, cw, pw, ps, wo, mod, g2, wrh, wrl, br, seq_len, first_tile):
    b, la, d = s.shape
    nt = la // ROW_TILE
    off = first_tile
    lo = la - off * ROW_TILE
    hb = ROW_TILE // POOL_HALO
    nhb = la // POOL_HALO

    def mod_idx(i, j):
        return (jnp.where(j + off == 0, b, i), 0, 0)

    const2 = lambda i, j: (0, 0)
    return pl.pallas_call(
        functools.partial(_out_kernel, seq_len=seq_len, first_tile=off),
        grid=(b, nt - off),
        in_specs=[
            pl.BlockSpec((1, ROW_TILE, d), lambda i, j: (i, j + off, 0)),
            pl.BlockSpec((1, N_PAIRS, ROW_TILE, LANES), lambda i, j: (i, 0, j, 0)),
            pl.BlockSpec((1, ROW_TILE, 3 * CONV_W), lambda i, j: (i, j + off, 0)),
            pl.BlockSpec((1, POOL_HALO, 3 * CONV_W), lambda i, j: (i, jnp.maximum((j + off) * hb - 1, 0), 0)),
            pl.BlockSpec((1, POOL_HALO, 3 * CONV_W),
                         lambda i, j: (i, jnp.minimum((j + off + 1) * hb, nhb - 1), 0)),
            pl.BlockSpec((3, CONV_W), const2),
            pl.BlockSpec((POOL_W, POOL_W), const2),
            pl.BlockSpec((1, POOL_W), const2),
            pl.BlockSpec((d, d), const2),
            pl.BlockSpec((1, N_MOD, d), mod_idx),
            pl.BlockSpec((1, d), const2),
            pl.BlockSpec((d, LANES), const2),
            pl.BlockSpec((d, LANES), const2),
            pl.BlockSpec((1, LANES), const2),
        ],
        out_specs=[
            pl.BlockSpec((1, ROW_TILE, d), lambda i, j: (i, j, 0)),
            pl.BlockSpec((1, ROW_TILE * SUBLANES, LANES), lambda i, j: (i, j, 0)),
            pl.BlockSpec((1, ROW_TILE, TOP_K), lambda i, j: (i, j, 0)),
            pl.BlockSpec((1, ROW_TILE, TOP_K), lambda i, j: (i, j, 0)),
        ],
        out_shape=[
            jax.ShapeDtypeStruct((b, lo, d), F32),
            jax.ShapeDtypeStruct((b, lo * SUBLANES, LANES), F32),
            jax.ShapeDtypeStruct((b, lo, TOP_K), I32),
            jax.ShapeDtypeStruct((b, lo, TOP_K), F32),
        ],
        compiler_params=pltpu.CompilerParams(vmem_limit_bytes=VMEM_LIMIT),
        name="outproj",
    )(s, oa, cp, cp, cp, cw, pw, ps, wo, mod, g2, wrh, wrl, br)


def _rank_kernel(te_ref, tri_ref, dest_ref, cnt_ref, carry):
    ph = pl.program_id(0)
    i = pl.program_id(1)
    tr = te_ref.shape[0]
    lane1 = lax.broadcasted_iota(I32, (SUBLANES, LANES), 1)

    @pl.when((ph == 0) & (i == 0))
    def _():
        carry[...] = jnp.zeros_like(carry)

    @pl.when((ph == 1) & (i == 0))
    def _():
        cnt = carry[...]
        cnt_ref[...] = cnt.astype(I32)
        padded = jnp.ceil(cnt * (1.0 / EXPERT_ROWS)) * EXPERT_ROWS
        incl = padded
        for sh in (1, 2, 4, 8, 16):
            incl = incl + jnp.where(lane1 >= sh, pltpu.roll(incl, sh, axis=1), 0.0)
        carry[...] = incl - padded

    e = te_ref[...]
    lane = lax.broadcasted_iota(I32, (tr, LANES), 1)
    ohs = [(e[:, k:k + 1] == lane).astype(F32) for k in range(TOP_K)]
    m = ohs[0] + ohs[1] + ohs[2] + ohs[3]

    @pl.when(ph == 0)
    def _():
        dest_ref[...] = jnp.zeros(dest_ref.shape, I32)

    @pl.when(ph == 1)
    def _():
        base = _dot(tri_ref[...], m.astype(BF16)) + carry[0:1, :]
        for k in range(TOP_K):
            dest_ref[:, k:k + 1] = jnp.sum(ohs[k] * base, axis=-1, keepdims=True).astype(I32)

    carry[...] = carry[...] + jnp.sum(m, axis=0, keepdims=True)


def _rank_call(te):
    t = te.shape[0]
    nt = t // RANK_TILE
    r = lax.broadcasted_iota(I32, (RANK_TILE, RANK_TILE), 0)
    c = lax.broadcasted_iota(I32, (RANK_TILE, RANK_TILE), 1)
    tri = (c < r).astype(BF16)
    return pl.pallas_call(
        _rank_kernel,
        grid=(2, nt),
        in_specs=[
            pl.BlockSpec((RANK_TILE, TOP_K), lambda ph, i: (i, 0)),
            pl.BlockSpec((RANK_TILE, RANK_TILE), lambda ph, i: (0, 0)),
        ],
        out_specs=[
            pl.BlockSpec((RANK_TILE, TOP_K), lambda ph, i: (ph * i, 0)),
            pl.BlockSpec((SUBLANES, LANES), lambda ph, i: (0, 0)),
        ],
        out_shape=[
            jax.ShapeDtypeStruct((t, TOP_K), I32),
            jax.ShapeDtypeStruct((SUBLANES, LANES), I32),
        ],
        scratch_shapes=[pltpu.VMEM((SUBLANES, LANES), F32)],
        compiler_params=pltpu.CompilerParams(dimension_semantics=("arbitrary", "arbitrary")),
        name="rank",
    )(te, tri)


def _disp_kernel(pad0_ref, npad_ref, dest_ref, hu_ref, xs_ref, zero, sem, zsem):
    def row_copy(r, k):
        return pltpu.make_async_copy(hu_ref.at[r], xs_ref.at[dest_ref[r * TOP_K + k]], sem)

    def start(r, _):
        for k in range(TOP_K):
            row_copy(r, k).start()
        return 0

    lax.fori_loop(0, ROW_TILE, start, 0)
    for _ in range(TOP_K):
        pltpu.make_async_copy(hu_ref, hu_ref, sem).wait()

    @pl.when(pl.program_id(0) == pl.num_programs(0) - 1)
    def _():
        zero[...] = jnp.zeros(zero.shape, F32)
        run = zero.shape[0]

        def zero_range(e, _):
            n_runs = npad_ref[e] // run
            n_rows = npad_ref[e] - n_runs * run
            row0 = pad0_ref[e] + n_runs * run

            def run_copy(c):
                return pltpu.make_async_copy(zero, xs_ref.at[pl.ds(pad0_ref[e] + c * run, run)], zsem)

            def row_copy0(r):
                return pltpu.make_async_copy(zero.at[0], xs_ref.at[row0 + r], zsem)

            def loop(n, make, wait):
                def body(c, carry):
                    if wait:
                        make(c).wait()
                    else:
                        make(c).start()
                    return carry
                lax.fori_loop(0, n, body, 0)

            loop(n_runs, run_copy, False)
            loop(n_rows, row_copy0, False)
            loop(n_runs, run_copy, True)
            loop(n_rows, row_copy0, True)
            return 0

        lax.fori_loop(0, pad0_ref.shape[0], zero_range, 0)


def _disp_call(pad0, npad, dest_flat, hu, n_rows):
    t = hu.shape[0]
    grid_spec = pltpu.PrefetchScalarGridSpec(
        num_scalar_prefetch=2,
        grid=(t // ROW_TILE,),
        in_specs=[
            pl.BlockSpec((ROW_TILE * TOP_K,), lambda i, p0, n: (i,), memory_space=pltpu.SMEM),
            pl.BlockSpec((ROW_TILE, SUBLANES, LANES), lambda i, p0, n: (i, 0, 0)),
        ],
        out_specs=pl.BlockSpec(memory_space=pl.ANY),
        scratch_shapes=[pltpu.VMEM((ZERO_RUN, SUBLANES, LANES), F32), pltpu.SemaphoreType.DMA(()),
                        pltpu.SemaphoreType.DMA(())],
    )
    return pl.pallas_call(
        _disp_kernel,
        grid_spec=grid_spec,
        out_shape=jax.ShapeDtypeStruct((n_rows, SUBLANES, LANES), F32),
        compiler_params=pltpu.CompilerParams(dimension_semantics=("arbitrary",)),
        name="dispatch",
    )(pad0, npad, dest_flat, hu)


def _exp_kernel(be_ref, nu_ref, xs_ref, wgu_ref, bgu_ref, wdn_ref, bdn_ref, ys_ref, wgu_b, wdn_b):
    i = pl.program_id(0)
    ff = wdn_ref.shape[1]
    rows = xs_ref.shape[0] // SUBLANES

    @pl.when((i == 0) | (be_ref[i] != be_ref[jnp.maximum(i - 1, 0)]))
    def _():
        wgu_b[...] = wgu_ref[0].astype(BF16)
        wdn_b[...] = wdn_ref[0].astype(BF16)

    @pl.when(i < nu_ref[0])
    def _():
        xb = jnp.concatenate([_load_row_tiles(xs_ref, 0, rows, c).astype(BF16) for c in range(SUBLANES)],
                             axis=1)
        gu = _dot(xb, wgu_b[...]) + bgu_ref[0]
        glu = jnp.minimum(gu[:, :ff], SWIGLU_LIMIT)
        lin = jnp.clip(gu[:, ff:], -SWIGLU_LIMIT, SWIGLU_LIMIT)
        act = glu * jax.nn.sigmoid(SWIGLU_ALPHA * glu) * (lin + 1.0)
        _store_row_tiles(ys_ref, _dot(act.astype(BF16), wdn_b[...]) + bdn_ref[0])

    @pl.when(i >= nu_ref[0])
    def _():
        ys_ref[...] = jnp.zeros_like(ys_ref)


def _exp_call(block_e, n_used, xs, layer, wgu, bgu, wdn, bdn):
    n_rows = xs.shape[0] // SUBLANES
    _, _, d, ff2 = wgu.shape
    ff = ff2 // 2
    nb = n_rows // EXPERT_ROWS
    grid_spec = pltpu.PrefetchScalarGridSpec(
        num_scalar_prefetch=2,
        grid=(nb,),
        in_specs=[
            pl.BlockSpec((EXPERT_ROWS * SUBLANES, LANES), lambda i, be, nu: (jnp.minimum(i, nu[0] - 1), 0)),
            pl.BlockSpec((None, 1, d, ff2), lambda i, be, nu: (layer, be[i], 0, 0)),
            pl.BlockSpec((1, 1, ff2), lambda i, be, nu: (be[i], 0, 0)),
            pl.BlockSpec((None, 1, ff, d), lambda i, be, nu: (layer, be[i], 0, 0)),
            pl.BlockSpec((1, 1, d), lambda i, be, nu: (be[i], 0, 0)),
        ],
        out_specs=pl.BlockSpec((EXPERT_ROWS * SUBLANES, LANES), lambda i, be, nu: (i, 0)),
        scratch_shapes=[pltpu.VMEM((d, ff2), BF16), pltpu.VMEM((ff, d), BF16)],
    )
    return pl.pallas_call(
        _exp_kernel,
        grid_spec=grid_spec,
        out_shape=jax.ShapeDtypeStruct((n_rows * SUBLANES, LANES), F32),
        compiler_params=pltpu.CompilerParams(dimension_semantics=("arbitrary",),
                                             vmem_limit_bytes=EXPERT_VMEM_LIMIT),
        name="experts",
    )(block_e, n_used, xs, wgu, bgu.reshape(-1, 1, ff2), wdn, bdn.reshape(-1, 1, d))


def _comb_kernel(dest_ref, x_ref, mod_ref, tw_ref, ys_ref, o_ref, buf, sem):
    def row_copy(r, k):
        r0 = pl.multiple_of((k * ROW_TILE + r) * SUBLANES, SUBLANES)
        return pltpu.make_async_copy(ys_ref.at[dest_ref[r * TOP_K + k]], buf.at[pl.ds(r0, SUBLANES)], sem)

    def start(r, _):
        for k in range(TOP_K):
            row_copy(r, k).start()
        return 0

    lax.fori_loop(0, ROW_TILE, start, 0)
    pltpu.make_async_copy(buf, buf, sem).wait()
    w = [tw_ref[0, :, k:k + 1] for k in range(TOP_K)]
    for c in range(SUBLANES):
        acc = _load_row_tiles(buf, 0, ROW_TILE, c) * w[0]
        for k in range(1, TOP_K):
            acc = acc + _load_row_tiles(buf, k * ROW_TILE, ROW_TILE, c) * w[k]
        cols = slice(c * LANES, (c + 1) * LANES)
        o_ref[0, :, cols] = x_ref[0, :, cols] + mod_ref[0, 5:6, cols] * acc


def _comb_call(dest_flat, s, mod, tw, ys, first_tile):
    b, rows, d = s.shape
    nt = rows // ROW_TILE

    def mod_idx(i, j):
        return (jnp.where(j + first_tile == 0, b, i), 0, 0)

    return pl.pallas_call(
        _comb_kernel,
        grid=(b, nt),
        in_specs=[
            pl.BlockSpec((ROW_TILE * TOP_K,), lambda i, j: (i * nt + j,), memory_space=pltpu.SMEM),
            pl.BlockSpec((1, ROW_TILE, d), lambda i, j: (i, j, 0)),
            pl.BlockSpec((1, N_MOD, d), mod_idx),
            pl.BlockSpec((1, ROW_TILE, TOP_K), lambda i, j: (i, j, 0)),
            pl.BlockSpec(memory_space=pl.ANY),
        ],
        out_specs=pl.BlockSpec((1, ROW_TILE, d), lambda i, j: (i, j, 0)),
        out_shape=jax.ShapeDtypeStruct((b, rows, d), F32),
        scratch_shapes=[pltpu.VMEM((TOP_K * ROW_TILE * SUBLANES, LANES), F32), pltpu.SemaphoreType.DMA(())],
        compiler_params=pltpu.CompilerParams(dimension_semantics=("arbitrary", "arbitrary"),
                                             vmem_limit_bytes=VMEM_LIMIT),
        name="combine",
    )(dest_flat, s, mod, tw, ys)


def _rope_tables(seq_len):
    rows = seq_len // GRID_W
    row = jnp.broadcast_to(jnp.arange(rows, dtype=F32)[:, None], (rows, GRID_W)).reshape(seq_len)
    col = jnp.broadcast_to(jnp.arange(GRID_W, dtype=F32)[None, :], (rows, GRID_W)).reshape(seq_len)
    inv = ROPE_BASE ** (-jnp.arange(ROPE_PAIRS, dtype=F32) / ROPE_PAIRS)
    ang = jnp.stack([row, col], axis=0)[:, None, :] * inv[None, :, None]
    cos_l = jnp.cos(ang).reshape(2 * ROPE_PAIRS, seq_len)
    sin_l = jnp.sin(ang).reshape(2 * ROPE_PAIRS, seq_len)
    cos_t = jnp.concatenate([jnp.ones((2 * ROPE_PAIRS, CTX_LEN), F32), cos_l], axis=1)
    sin_t = jnp.concatenate([jnp.zeros((2 * ROPE_PAIRS, CTX_LEN), F32), sin_l], axis=1)
    return cos_t, sin_t


def kernel(x, c, ctx, c_ctx, w_mod, b_mod, norm1_g, norm2_g, w_in, q_norm_g, k_norm_g,
           lambda_q1, lambda_k1, lambda_q2, lambda_k2, subln_g, conv_w, pool_w, pool_scale, w_out,
           router_w, router_b, w_gate_up, b_gate_up, w_down, b_down):
    b, seq_len, d = x.shape
    assert d == SUBLANES * LANES
    depth = w_mod.shape[0]
    la = CTX_LEN + seq_len

    rows = -(-(b + 1) // SUBLANES) * SUBLANES
    c_all = jnp.zeros((rows, d), F32).at[:b].set(c).at[b].set(c_ctx)
    mod = _mod_call(c_all, w_mod, b_mod).reshape(depth, rows, N_MOD, d)

    cos_t, sin_t = _rope_tables(seq_len)
    s = jnp.concatenate([ctx, x], axis=1)

    for l in range(depth):
        first_tile = 1 if l + 1 == depth else 0
        t_all = b * (la - first_tile * ROW_TILE)
        n_rows = -(-(t_all * TOP_K + N_EXPERTS * (EXPERT_ROWS - 1)) // EXPERT_ROWS) * EXPERT_ROWS
        nb = n_rows // EXPERT_ROWS
        lam_init = 0.8 - 0.6 * math.exp(-0.3 * l)
        wl = w_in[l]
        wm = wl[:, 2 * ATTN_W:].astype(BF16)
        wqk = wl[:, :2 * ATTN_W].T.astype(BF16)
        q_gain = jnp.tile(q_norm_g[l], ATTN_W // QK_DIM) * (LOG2E * QK_DIM ** -0.5)
        k_gain = jnp.tile(k_norm_g[l], ATTN_W // QK_DIM)
        gain = jnp.broadcast_to(jnp.concatenate([q_gain, k_gain])[:, None], (2 * ATTN_W, ROW_TILE))
        q, kt, v, cp = _in_call(s, mod[l], norm1_g[l][None], wm, wqk, gain, cos_t, sin_t)

        lam_p = jnp.zeros((SUBLANES, LANES), F32)
        lam_p = lam_p.at[0, :QK_DIM].set(lambda_q1[l]).at[1, :QK_DIM].set(lambda_k1[l])
        lam_p = lam_p.at[2, :QK_DIM].set(lambda_q2[l]).at[3, :QK_DIM].set(lambda_k2[l])
        sg = jnp.tile(subln_g[l], 2)[None]
        oa = _attn_call(lam_p, q, kt, v, sg, lam_init, first_tile)

        pw = jax.scipy.linalg.block_diag(*[pool_w[l, g] for g in range(len(POOL_WINDOWS))]).astype(BF16)
        wr = jnp.zeros((d, LANES), F32).at[:, :N_EXPERTS].set(router_w[l])
        wrh, wrl = _split_bf16(wr)
        br = jnp.full((1, LANES), NEG_BIG, F32).at[0, :N_EXPERTS].set(router_b[l])
        s, hu, te, tw = _out_call(s, oa, cp, conv_w[l], pw, pool_scale[l][None], w_out[l].astype(BF16),
                                  mod[l], norm2_g[l][None], wrh, wrl, br, seq_len, first_tile)

        dest, counts = _rank_call(te.reshape(t_all, TOP_K))
        cnt = counts[0, :N_EXPERTS]
        padded = (cnt + EXPERT_ROWS - 1) // EXPERT_ROWS * EXPERT_ROWS
        pad_end = jnp.cumsum(padded)
        n_used = (pad_end[-1] // EXPERT_ROWS).astype(I32)
        blk = jnp.minimum(jnp.arange(nb, dtype=I32), n_used - 1) * EXPERT_ROWS
        block_e = jnp.sum((pad_end[None, :] <= blk[:, None]).astype(I32), axis=1)
        block_e = jnp.minimum(block_e, N_EXPERTS - 1)
        dest_flat = dest.reshape(t_all * TOP_K)

        pad0 = jnp.concatenate([pad_end - padded + cnt, pad_end[-1:]])
        npad = jnp.concatenate([padded - cnt, n_rows - pad_end[-1:]])
        xs = _disp_call(pad0, npad, dest_flat, hu.reshape(t_all, SUBLANES, LANES), n_rows)
        ys = _exp_call(block_e, n_used.reshape(1), xs.reshape(n_rows * SUBLANES, LANES), l,
                       w_gate_up, b_gate_up[l], w_down, b_down[l])
        s = _comb_call(dest_flat, s, mod[l], tw, ys.reshape(n_rows, SUBLANES, LANES), first_tile)
    return s
```

```python
import functools
import math

import jax
import jax.numpy as jnp
from jax import lax
from jax.experimental import pallas as pl
from jax.experimental.pallas import tpu as pltpu

F32 = jnp.float32
BF16 = jnp.bfloat16
I32 = jnp.int32
U32 = jnp.uint32

GRID_W = 64
CTX_LEN = 256
HEAD_DIM = 64
QK_DIM = 32
N_HEADS = 8
ATTN_W = 512
CONV_W = 256
POOL_W = 256
POOL_WINDOWS = (2, 4, 8, 16)
POOL_GC = 64
ROPE_BASE = 10000.0
ROPE_PAIRS = 8
N_EXPERTS = 32
TOP_K = 4
N_MOD = 6
EPS = 1e-6
SWIGLU_ALPHA = 1.702
SWIGLU_LIMIT = 7.0
LOG2E = 1.4426950408889634

LANES = 128
SUBLANES = 8
ROW_TILE = 256
N_PAIRS = N_HEADS // 2
POOL_HALO = 8
EXPERT_ROWS = 512
RANK_TILE = 512
ZERO_RUN = 64
KEY_UNROLL = 16
VMEM_LIMIT = 48 * 1024 * 1024
ATTN_VMEM_LIMIT = 40 * 1024 * 1024
EXPERT_VMEM_LIMIT = 58 * 1024 * 1024
NEG_BIG = -1e30
BOUND_SLACK = 1.0 + 2.0 ** -9
UNDERFLOW_GUARD = 2.0 ** -100


def _split_bf16(a):
    hi = a.astype(BF16)
    lo = (a - hi.astype(F32)).astype(BF16)
    return hi, lo


def _dot(a, b):
    return jnp.dot(a, b, preferred_element_type=F32)


def _mod_kernel(c_ref, w_ref, b_ref, o_ref):
    cv = c_ref[...]
    s = cv * jax.nn.sigmoid(cv)
    s_hi, s_lo = _split_bf16(s)
    w_hi, w_lo = _split_bf16(w_ref[0])
    o_ref[0] = _dot(s_hi, w_hi) + _dot(s_lo, w_hi) + _dot(s_hi, w_lo) + b_ref[0]


def _mod_call(c_all, w_mod, b_mod):
    depth, d, n = w_mod.shape
    r = c_all.shape[0]
    tn = 1536
    return pl.pallas_call(
        _mod_kernel,
        grid=(depth, n // tn),
        in_specs=[
            pl.BlockSpec((r, d), lambda l, j: (0, 0)),
            pl.BlockSpec((1, d, tn), lambda l, j: (l, 0, j)),
            pl.BlockSpec((1, 1, tn), lambda l, j: (l, 0, j)),
        ],
        out_specs=pl.BlockSpec((1, r, tn), lambda l, j: (l, 0, j)),
        out_shape=jax.ShapeDtypeStruct((depth, r, n), F32),
        compiler_params=pltpu.CompilerParams(vmem_limit_bytes=VMEM_LIMIT),
        name="mod",
    )(c_all, w_mod, b_mod.reshape(depth, 1, n))


def _rms_mod(x, g, shift, scale):
    y = x * lax.rsqrt(jnp.mean(x * x, axis=-1, keepdims=True) + EPS) * g
    return y * (1.0 + scale) + shift


def _in_kernel(x_ref, mod_ref, g1_ref, wm_ref, wqk_ref, gain_ref, cos_ref, sin_ref,
               q_ref, kt_ref, v_ref, cp_ref):
    tm = x_ref.shape[1]
    h = _rms_mod(x_ref[0], g1_ref[...], mod_ref[0, 0:1, :], mod_ref[0, 1:2, :])
    pm = _dot(h.astype(BF16), wm_ref[...])
    for p in range(N_PAIRS):
        v_ref[0, p] = pm[:, p * LANES:(p + 1) * LANES].astype(BF16)
    o = ATTN_W
    cp_ref[0, :, 0:CONV_W] = pm[:, o:o + CONV_W]
    cp_ref[0, :, CONV_W:2 * CONV_W] = pm[:, o + CONV_W:o + 2 * CONV_W] * pm[:, o + 2 * CONV_W:o + 3 * CONV_W]
    cp_ref[0, :, 2 * CONV_W:] = pm[:, o + 3 * CONV_W:]

    ht = h.T.astype(BF16)
    qkt = _dot(wqk_ref[...], ht)
    ng = 2 * ATTN_W // QK_DIM
    t = qkt.reshape(ng, 4, ROPE_PAIRS, tm)
    ss = jnp.sum(jnp.sum(t * t, axis=2, keepdims=True), axis=1, keepdims=True)
    tn = t * lax.rsqrt(ss * (1.0 / QK_DIM) + EPS) * gain_ref[...].reshape(ng, 4, ROPE_PAIRS, tm)
    cs = cos_ref[...].reshape(2, ROPE_PAIRS, tm)
    sn = sin_ref[...].reshape(2, ROPE_PAIRS, tm)
    parts = []
    for a in range(2):
        t1 = tn[:, 2 * a]
        t2 = tn[:, 2 * a + 1]
        parts.append(t1 * cs[a] - t2 * sn[a])
        parts.append(t2 * cs[a] + t1 * sn[a])
    rot = jnp.stack(parts, axis=1).reshape(2 * ATTN_W, tm)
    qt = rot[:ATTN_W].T
    for p in range(N_PAIRS):
        q_ref[0, p] = qt[:, p * LANES:(p + 1) * LANES].astype(BF16)
        kt_ref[0, 0, p] = rot[ATTN_W + p * LANES:ATTN_W + (p + 1) * LANES].astype(BF16)


def _in_call(s, mod, g1, wm, wqk, gain, cos_t, sin_t):
    b, la, d = s.shape
    nt = la // ROW_TILE
    nmain = wm.shape[1]

    def mod_idx(i, j):
        return (jnp.where(j == 0, b, i), 0, 0)

    return pl.pallas_call(
        _in_kernel,
        grid=(b, nt),
        in_specs=[
            pl.BlockSpec((1, ROW_TILE, d), lambda i, j: (i, j, 0)),
            pl.BlockSpec((1, N_MOD, d), mod_idx),
            pl.BlockSpec((1, d), lambda i, j: (0, 0)),
            pl.BlockSpec((d, nmain), lambda i, j: (0, 0)),
            pl.BlockSpec((2 * ATTN_W, d), lambda i, j: (0, 0)),
            pl.BlockSpec((2 * ATTN_W, ROW_TILE), lambda i, j: (0, 0)),
            pl.BlockSpec((2 * ROPE_PAIRS, ROW_TILE), lambda i, j: (0, j)),
            pl.BlockSpec((2 * ROPE_PAIRS, ROW_TILE), lambda i, j: (0, j)),
        ],
        out_specs=[
            pl.BlockSpec((1, N_PAIRS, ROW_TILE, LANES), lambda i, j: (i, 0, j, 0)),
            pl.BlockSpec((1, 1, N_PAIRS, LANES, ROW_TILE), lambda i, j: (i, j, 0, 0, 0)),
            pl.BlockSpec((1, N_PAIRS, ROW_TILE, LANES), lambda i, j: (i, 0, j, 0)),
            pl.BlockSpec((1, ROW_TILE, 3 * CONV_W), lambda i, j: (i, j, 0)),
        ],
        out_shape=[
            jax.ShapeDtypeStruct((b, N_PAIRS, la, LANES), BF16),
            jax.ShapeDtypeStruct((b, nt, N_PAIRS, LANES, ROW_TILE), BF16),
            jax.ShapeDtypeStruct((b, N_PAIRS, la, LANES), BF16),
            jax.ShapeDtypeStruct((b, la, 3 * CONV_W), F32),
        ],
        compiler_params=pltpu.CompilerParams(vmem_limit_bytes=VMEM_LIMIT),
        name="inproj",
    )(s, mod, g1, wm, wqk, gain, cos_t, sin_t)


def _attn_kernel(lam_ref, q_ref, kt_ref, v_ref, sg_ref, o_ref, kn_scr, mb_scr, l_scr, acc_scr, *,
                 lam_init, first_tile):
    tq = q_ref.shape[2]
    nkc_all = kt_ref.shape[1]
    j = pl.program_id(1) + first_tile
    n_groups = jnp.where(j == 0, 0, (nkc_all - 1) // KEY_UNROLL)
    n_chunks = jnp.where(j == 0, 1, nkc_all)
    lp = lam_ref[...]
    lam = (jnp.exp(jnp.sum(lp[0:1] * lp[1:2], axis=-1, keepdims=True))
           - jnp.exp(jnp.sum(lp[2:3] * lp[3:4], axis=-1, keepdims=True)) + lam_init)
    lane = lax.broadcasted_iota(I32, (1, LANES), 1)
    low = lane < HEAD_DIM
    groups = [(lane >= g * QK_DIM) & (lane < (g + 1) * QK_DIM) for g in range(4)]

    @pl.when(j == first_tile)
    def _():
        for p in range(N_PAIRS):
            def key_norms(kc, mx, p=p):
                k = kt_ref[0, kc, p].astype(F32)
                k2 = jnp.sum((k * k).reshape(4, QK_DIM, ROW_TILE), axis=1)
                return jnp.maximum(mx, k2)
            mx = lax.fori_loop(0, nkc_all, key_norms, jnp.zeros((4, ROW_TILE), F32))
            kn = jnp.sqrt(jnp.max(mx, axis=-1, keepdims=True))
            kn_scr[p] = jnp.broadcast_to(jnp.concatenate([kn, kn], axis=0), (SUBLANES, LANES))

    for p in range(N_PAIRS):
        qp = q_ref[0, p]
        qs = jnp.concatenate([jnp.where(groups[g], qp, jnp.zeros_like(qp)) for g in range(4)], axis=0)
        qf = qp.astype(F32)
        q2 = qf * qf
        for g in range(4):
            qn = jnp.sqrt(jnp.sum(jnp.where(groups[g], q2, 0.0), axis=-1, keepdims=True))
            mb_scr[g * tq:(g + 1) * tq, :] = qn * kn_scr[p, g:g + 1, :] * BOUND_SLACK

        def exp_pv(kc, qs=qs, p=p):
            sc = _dot(qs, kt_ref[0, kc, p])
            mb = mb_scr[...]
            e0 = jnp.exp2(sc[:, :LANES] - mb)
            e1 = jnp.exp2(sc[:, LANES:] - mb)
            r0 = pl.multiple_of(kc * ROW_TILE, ROW_TILE)
            e = jnp.concatenate([e0.astype(BF16), e1.astype(BF16)], axis=1)
            return e0 + e1, _dot(e, v_ref[0, p, pl.ds(r0, ROW_TILE), :])

        def sweep(g, _):
            parts = [exp_pv(1 + g * KEY_UNROLL + u) for u in range(KEY_UNROLL)]
            ls = [a for a, _ in parts]
            pv = [b for _, b in parts]
            while len(ls) > 1:
                ls = [a + b for a, b in zip(ls[::2], ls[1::2])]
                pv = [a + b for a, b in zip(pv[::2], pv[1::2])]
            l_scr[...] = l_scr[...] + ls[0]
            acc_scr[...] = acc_scr[...] + pv[0]
            return 0

        l_scr[...], acc_scr[...] = exp_pv(0)
        lax.fori_loop(0, n_groups, sweep, 0)
        lmin = jnp.min(jnp.sum(l_scr[...], axis=-1, keepdims=True))

        @pl.when(jnp.logical_not(lmin >= UNDERFLOW_GUARD))
        def _():
            def row_max(kc, m, qs=qs, p=p):
                sc = _dot(qs, kt_ref[0, kc, p])
                return jnp.maximum(m, jnp.maximum(sc[:, :LANES], sc[:, LANES:]))
            m = lax.fori_loop(0, n_chunks, row_max, jnp.full((4 * tq, LANES), -jnp.inf, F32))
            mb_scr[...] = jnp.broadcast_to(jnp.max(m, axis=-1, keepdims=True), mb_scr.shape)
            l_scr[...] = jnp.zeros(l_scr.shape, F32)
            acc_scr[...] = jnp.zeros(acc_scr.shape, F32)

            def redo(kc, _):
                dl, dacc = exp_pv(kc)
                l_scr[...] = l_scr[...] + dl
                acc_scr[...] = acc_scr[...] + dacc
                return 0
            lax.fori_loop(0, n_chunks, redo, 0)

        on = acc_scr[...] / jnp.sum(l_scr[...], axis=-1, keepdims=True)
        heads = [on[(2 * hh) * tq:(2 * hh + 1) * tq] - lam * on[(2 * hh + 1) * tq:(2 * hh + 2) * tq]
                 for hh in range(2)]
        o = jnp.where(low, heads[0], heads[1])
        o2 = o * o
        ss_lo = jnp.sum(jnp.where(low, o2, 0.0), axis=-1, keepdims=True)
        ss_hi = jnp.sum(jnp.where(low, 0.0, o2), axis=-1, keepdims=True)
        inv = lax.rsqrt(jnp.where(low, ss_lo, ss_hi) * (1.0 / HEAD_DIM) + EPS)
        o_ref[0, p] = (o * inv * sg_ref[...] * (1.0 - lam_init)).astype(BF16)


def _attn_call(lam_p, q, kt, v, sg, lam_init, first_tile):
    b, _, la, _ = q.shape
    nt = la // ROW_TILE
    assert (nt - 1) % KEY_UNROLL == 0
    off = first_tile
    return pl.pallas_call(
        functools.partial(_attn_kernel, lam_init=lam_init, first_tile=off),
        grid=(b, nt - off),
        in_specs=[
            pl.BlockSpec((SUBLANES, LANES), lambda i, j: (0, 0)),
            pl.BlockSpec((1, N_PAIRS, ROW_TILE, LANES), lambda i, j: (i, 0, j + off, 0)),
            pl.BlockSpec((1, nt, N_PAIRS, LANES, ROW_TILE), lambda i, j: (i, 0, 0, 0, 0)),
            pl.BlockSpec((1, N_PAIRS, la, LANES), lambda i, j: (i, 0, 0, 0)),
            pl.BlockSpec((1, LANES), lambda i, j: (0, 0)),
        ],
        out_specs=pl.BlockSpec((1, N_PAIRS, ROW_TILE, LANES), lambda i, j: (i, 0, j, 0)),
        out_shape=jax.ShapeDtypeStruct((b, N_PAIRS, la - off * ROW_TILE, LANES), BF16),
        scratch_shapes=[pltpu.VMEM((N_PAIRS, SUBLANES, LANES), F32),
                        pltpu.VMEM((4 * ROW_TILE, LANES), F32),
                        pltpu.VMEM((4 * ROW_TILE, LANES), F32),
                        pltpu.VMEM((4 * ROW_TILE, LANES), F32)],
        compiler_params=pltpu.CompilerParams(vmem_limit_bytes=ATTN_VMEM_LIMIT),
        name="attn",
    )(lam_p, q, kt, v, sg)


def _store_row_tiles(ref, val):
    rows = val.shape[0]
    for c in range(SUBLANES):
        ref[pl.ds(c, rows, stride=SUBLANES), :] = val[:, c * LANES:(c + 1) * LANES]


def _load_row_tiles(ref, first_row, rows, c):
    return ref[pl.ds(first_row * SUBLANES + c, rows, stride=SUBLANES), :]


def _shift_rows(a, k):
    return pltpu.roll(a, k % a.shape[0], axis=0)


def _out_kernel(x_ref, oa_ref, cp_ref, cpp_ref, cpn_ref, cw_ref, pw_ref, ps_ref, wo_ref,
                mod_ref, g2_ref, wrh_ref, wrl_ref, br_ref,
                xo_ref, hu_ref, te_ref, tw_ref, *, seq_len, first_tile):
    tm = x_ref.shape[1]
    j = pl.program_id(1) + first_tile
    nt = pl.num_programs(1) + first_tile
    halo = POOL_HALO
    has_prev = j >= 2
    has_next = (j >= 1) & (j < nt - 1)
    prev = jnp.where(has_prev, cpp_ref[0], 0.0)
    nxt = jnp.where(has_next, cpn_ref[0], 0.0)
    ext = jnp.concatenate([prev, cp_ref[0], nxt], axis=0)

    z = ext[:, CONV_W:2 * CONV_W]
    conv = (cw_ref[0:1, :] * _shift_rows(z, 1) + cw_ref[1:2, :] * z + cw_ref[2:3, :] * _shift_rows(z, -1))
    o_conv = ext[halo:halo + tm, 0:CONV_W] * conv[halo:halo + tm]

    u = ext[:, 2 * CONV_W:]
    a2 = u + _shift_rows(u, 1)
    a4 = _shift_rows(a2, -1) + _shift_rows(a2, 1)
    a8 = _shift_rows(a4, -2) + _shift_rows(a4, 2)
    a16 = _shift_rows(a8, -4) + _shift_rows(a8, 4)
    lane = lax.broadcasted_iota(I32, (tm, POOL_W), 1)
    grp = lane // POOL_GC
    wsum = jnp.where(grp == 0, a2[halo:halo + tm],
                     jnp.where(grp == 1, a4[halo:halo + tm],
                               jnp.where(grp == 2, a8[halo:halo + tm], a16[halo:halo + tm])))
    half = jnp.where(grp == 0, 1, jnp.where(grp == 1, 2, jnp.where(grp == 2, 4, 8)))
    row = lax.broadcasted_iota(I32, (tm, POOL_W), 0)
    pos = jnp.where(j == 0, row, (j - 1) * tm + row)
    n_seq = jnp.where(j == 0, CTX_LEN, seq_len)
    cnt = jnp.minimum(pos + half, n_seq) - jnp.maximum(pos - half, 0)
    dlt = wsum / cnt.astype(F32) - u[halo:halo + tm]
    o_pool = _dot(dlt.astype(BF16), pw_ref[...]) * ps_ref[...]

    mix = _dot(o_conv.astype(BF16), wo_ref[ATTN_W:ATTN_W + CONV_W, :])
    mix = mix + _dot(o_pool.astype(BF16), wo_ref[ATTN_W + CONV_W:, :])
    for p in range(N_PAIRS):
        mix = mix + _dot(oa_ref[0, p], wo_ref[p * LANES:(p + 1) * LANES, :])
    x = x_ref[0] + mod_ref[0, 2:3, :] * mix
    xo_ref[0] = x

    h2 = _rms_mod(x, g2_ref[...], mod_ref[0, 3:4, :], mod_ref[0, 4:5, :])
    _store_row_tiles(hu_ref.at[0], h2)

    h_hi, h_lo = _split_bf16(h2)
    logits = (_dot(h_hi, wrh_ref[...]) + _dot(h_lo, wrh_ref[...]) + _dot(h_hi, wrl_ref[...])
              + br_ref[...])
    lanef = lax.broadcasted_iota(I32, (tm, LANES), 1).astype(F32)
    work = logits
    tops = []
    for k in range(TOP_K):
        mk = jnp.max(work, axis=-1, keepdims=True)
        ik = jnp.min(jnp.where(work == mk, lanef, float(LANES)), axis=-1, keepdims=True)
        te_ref[0, :, k:k + 1] = ik.astype(I32)
        work = jnp.where(lanef == ik, -jnp.inf, work)
        tops.append(mk)
    es = [jnp.exp(mk - tops[0]) for mk in tops]
    den = es[0] + es[1] + es[2] + es[3]
    for k in range(TOP_K):
        tw_ref[0, :, k:k + 1] = es[k] / den


def _out_call(s, oa, cp, cw, pw, ps, wo, mod, g2, wrh, wrl, br, seq_len, first_tile):
    b, la, d = s.shape
    nt = la // ROW_TILE
    off = first_tile
    lo = la - off * ROW_TILE
    hb = ROW_TILE // POOL_HALO
    nhb = la // POOL_HALO

    def mod_idx(i, j):
        return (jnp.where(j + off == 0, b, i), 0, 0)

    const2 = lambda i, j: (0, 0)
    return pl.pallas_call(
        functools.partial(_out_kernel, seq_len=seq_len, first_tile=off),
        grid=(b, nt - off),
        in_specs=[
            pl.BlockSpec((1, ROW_TILE, d), lambda i, j: (i, j + off, 0)),
            pl.BlockSpec((1, N_PAIRS, ROW_TILE, LANES), lambda i, j: (i, 0, j, 0)),
            pl.BlockSpec((1, ROW_TILE, 3 * CONV_W), lambda i, j: (i, j + off, 0)),
            pl.BlockSpec((1, POOL_HALO, 3 * CONV_W), lambda i, j: (i, jnp.maximum((j + off) * hb - 1, 0), 0)),
            pl.BlockSpec((1, POOL_HALO, 3 * CONV_W),
                         lambda i, j: (i, jnp.minimum((j + off + 1) * hb, nhb - 1), 0)),
            pl.BlockSpec((3, CONV_W), const2),
            pl.BlockSpec((POOL_W, POOL_W), const2),
            pl.BlockSpec((1, POOL_W), const2),
            pl.BlockSpec((d, d), const2),
            pl.BlockSpec((1, N_MOD, d), mod_idx),
            pl.BlockSpec((1, d), const2),
            pl.BlockSpec((d, LANES), const2),
            pl.BlockSpec((d, LANES), const2),
            pl.BlockSpec((1, LANES), const2),
        ],
        out_specs=[
            pl.BlockSpec((1, ROW_TILE, d), lambda i, j: (i, j, 0)),
            pl.BlockSpec((1, ROW_TILE * SUBLANES, LANES), lambda i, j: (i, j, 0)),
            pl.BlockSpec((1, ROW_TILE, TOP_K), lambda i, j: (i, j, 0)),
            pl.BlockSpec((1, ROW_TILE, TOP_K), lambda i, j: (i, j, 0)),
        ],
        out_shape=[
            jax.ShapeDtypeStruct((b, lo, d), F32),
            jax.ShapeDtypeStruct((b, lo * SUBLANES, LANES), F32),
            jax.ShapeDtypeStruct((b, lo, TOP_K), I32),
            jax.ShapeDtypeStruct((b, lo, TOP_K), F32),
        ],
        compiler_params=pltpu.CompilerParams(vmem_limit_bytes=VMEM_LIMIT),
        name="outproj",
    )(s, oa, cp, cp, cp, cw, pw, ps, wo, mod, g2, wrh, wrl, br)


def _rank_kernel(te_ref, tri_ref, dest_ref, cnt_ref, carry):
    ph = pl.program_id(0)
    i = pl.program_id(1)
    tr = te_ref.shape[0]
    lane1 = lax.broadcasted_iota(I32, (SUBLANES, LANES), 1)

    @pl.when((ph == 0) & (i == 0))
    def _():
        carry[...] = jnp.zeros_like(carry)

    @pl.when((ph == 1) & (i == 0))
    def _():
        cnt = carry[...]
        cnt_ref[...] = cnt.astype(I32)
        padded = jnp.ceil(cnt * (1.0 / EXPERT_ROWS)) * EXPERT_ROWS
        incl = padded
        for sh in (1, 2, 4, 8, 16):
            incl = incl + jnp.where(lane1 >= sh, pltpu.roll(incl, sh, axis=1), 0.0)
        carry[...] = incl - padded

    e = te_ref[...]
    lane = lax.broadcasted_iota(I32, (tr, LANES), 1)
    ohs = [(e[:, k:k + 1] == lane).astype(F32) for k in range(TOP_K)]
    m = ohs[0] + ohs[1] + ohs[2] + ohs[3]

    @pl.when(ph == 0)
    def _():
        dest_ref[...] = jnp.zeros(dest_ref.shape, I32)

    @pl.when(ph == 1)
    def _():
        base = _dot(tri_ref[...], m.astype(BF16)) + carry[0:1, :]
        for k in range(TOP_K):
            dest_ref[:, k:k + 1] = jnp.sum(ohs[k] * base, axis=-1, keepdims=True).astype(I32)

    carry[...] = carry[...] + jnp.sum(m, axis=0, keepdims=True)


def _rank_call(te):
    t = te.shape[0]
    nt = t // RANK_TILE
    r = lax.broadcasted_iota(I32, (RANK_TILE, RANK_TILE), 0)
    c = lax.broadcasted_iota(I32, (RANK_TILE, RANK_TILE), 1)
    tri = (c < r).astype(BF16)
    return pl.pallas_call(
        _rank_kernel,
        grid=(2, nt),
        in_specs=[
            pl.BlockSpec((RANK_TILE, TOP_K), lambda ph, i: (i, 0)),
            pl.BlockSpec((RANK_TILE, RANK_TILE), lambda ph, i: (0, 0)),
        ],
        out_specs=[
            pl.BlockSpec((RANK_TILE, TOP_K), lambda ph, i: (ph * i, 0)),
            pl.BlockSpec((SUBLANES, LANES), lambda ph, i: (0, 0)),
        ],
        out_shape=[
            jax.ShapeDtypeStruct((t, TOP_K), I32),
            jax.ShapeDtypeStruct((SUBLANES, LANES), I32),
        ],
        scratch_shapes=[pltpu.VMEM((SUBLANES, LANES), F32)],
        compiler_params=pltpu.CompilerParams(dimension_semantics=("arbitrary", "arbitrary")),
        name="rank",
    )(te, tri)


def _disp_kernel(pad0_ref, npad_ref, dest_ref, hu_ref, xs_ref, zero, sem, zsem):
    def row_copy(r, k):
        return pltpu.make_async_copy(hu_ref.at[r], xs_ref.at[dest_ref[r * TOP_K + k]], sem)

    def start(r, _):
        for k in range(TOP_K):
            row_copy(r, k).start(priority=k % 2)
        return 0

    lax.fori_loop(0, ROW_TILE, start, 0)
    for _ in range(TOP_K):
        pltpu.make_async_copy(hu_ref, hu_ref, sem).wait()

    @pl.when(pl.program_id(0) == pl.num_programs(0) - 1)
    def _():
        zero[...] = jnp.zeros(zero.shape, F32)
        run = zero.shape[0]

        def zero_range(e, _):
            n_runs = npad_ref[e] // run
            n_rows = npad_ref[e] - n_runs * run
            row0 = pad0_ref[e] + n_runs * run

            def run_copy(c):
                return pltpu.make_async_copy(zero, xs_ref.at[pl.ds(pad0_ref[e] + c * run, run)], zsem)

            def row_copy0(r):
                return pltpu.make_async_copy(zero.at[0], xs_ref.at[row0 + r], zsem)

            def loop(n, make, wait):
                def body(c, carry):
                    if wait:
                        make(c).wait()
                    else:
                        make(c).start()
                    return carry
                lax.fori_loop(0, n, body, 0)

            loop(n_runs, run_copy, False)
            loop(n_rows, row_copy0, False)
            loop(n_runs, run_copy, True)
            loop(n_rows, row_copy0, True)
            return 0

        lax.fori_loop(0, pad0_ref.shape[0], zero_range, 0)


def _disp_call(pad0, npad, dest_flat, hu, n_rows):
    t = hu.shape[0]
    grid_spec = pltpu.PrefetchScalarGridSpec(
        num_scalar_prefetch=2,
        grid=(t // ROW_TILE,),
        in_specs=[
            pl.BlockSpec((ROW_TILE * TOP_K,), lambda i, p0, n: (i,), memory_space=pltpu.SMEM),
            pl.BlockSpec((ROW_TILE, SUBLANES, LANES), lambda i, p0, n: (i, 0, 0)),
        ],
        out_specs=pl.BlockSpec(memory_space=pl.ANY),
        scratch_shapes=[pltpu.VMEM((ZERO_RUN, SUBLANES, LANES), F32), pltpu.SemaphoreType.DMA(()),
                        pltpu.SemaphoreType.DMA(())],
    )
    return pl.pallas_call(
        _disp_kernel,
        grid_spec=grid_spec,
        out_shape=jax.ShapeDtypeStruct((n_rows, SUBLANES, LANES), F32),
        compiler_params=pltpu.CompilerParams(dimension_semantics=("arbitrary",)),
        name="dispatch",
    )(pad0, npad, dest_flat, hu)


def _exp_kernel(be_ref, nu_ref, xs_ref, wgu_ref, bgu_ref, wdn_ref, bdn_ref, ys_ref, wgu_b, wdn_b):
    i = pl.program_id(0)
    ff = wdn_ref.shape[1]
    rows = xs_ref.shape[0] // SUBLANES

    @pl.when((i == 0) | (be_ref[i] != be_ref[jnp.maximum(i - 1, 0)]))
    def _():
        wgu_b[...] = wgu_ref[0].astype(BF16)
        wdn_b[...] = wdn_ref[0].astype(BF16)

    @pl.when(i < nu_ref[0])
    def _():
        xb = jnp.concatenate([_load_row_tiles(xs_ref, 0, rows, c).astype(BF16) for c in range(SUBLANES)],
                             axis=1)
        gu = _dot(xb, wgu_b[...]) + bgu_ref[0]
        glu = jnp.minimum(gu[:, :ff], SWIGLU_LIMIT)
        lin = jnp.clip(gu[:, ff:], -SWIGLU_LIMIT, SWIGLU_LIMIT)
        act = glu * jax.nn.sigmoid(SWIGLU_ALPHA * glu) * (lin + 1.0)
        _store_row_tiles(ys_ref, _dot(act.astype(BF16), wdn_b[...]) + bdn_ref[0])

    @pl.when(i >= nu_ref[0])
    def _():
        ys_ref[...] = jnp.zeros_like(ys_ref)


def _exp_call(block_e, n_used, xs, layer, wgu, bgu, wdn, bdn):
    n_rows = xs.shape[0] // SUBLANES
    _, _, d, ff2 = wgu.shape
    ff = ff2 // 2
    nb = n_rows // EXPERT_ROWS
    grid_spec = pltpu.PrefetchScalarGridSpec(
        num_scalar_prefetch=2,
        grid=(nb,),
        in_specs=[
            pl.BlockSpec((EXPERT_ROWS * SUBLANES, LANES), lambda i, be, nu: (jnp.minimum(i, nu[0] - 1), 0)),
            pl.BlockSpec((None, 1, d, ff2), lambda i, be, nu: (layer, be[i], 0, 0)),
            pl.BlockSpec((1, 1, ff2), lambda i, be, nu: (be[i], 0, 0)),
            pl.BlockSpec((None, 1, ff, d), lambda i, be, nu: (layer, be[i], 0, 0)),
            pl.BlockSpec((1, 1, d), lambda i, be, nu: (be[i], 0, 0)),
        ],
        out_specs=pl.BlockSpec((EXPERT_ROWS * SUBLANES, LANES), lambda i, be, nu: (i, 0)),
        scratch_shapes=[pltpu.VMEM((d, ff2), BF16), pltpu.VMEM((ff, d), BF16)],
    )
    return pl.pallas_call(
        _exp_kernel,
        grid_spec=grid_spec,
        out_shape=jax.ShapeDtypeStruct((n_rows * SUBLANES, LANES), F32),
        compiler_params=pltpu.CompilerParams(dimension_semantics=("arbitrary",),
                                             vmem_limit_bytes=EXPERT_VMEM_LIMIT),
        name="experts",
    )(block_e, n_used, xs, wgu, bgu.reshape(-1, 1, ff2), wdn, bdn.reshape(-1, 1, d))


def _comb_kernel(dest_ref, x_ref, mod_ref, tw_ref, ys_ref, o_ref, buf, sem):
    def row_copy(r, k):
        r0 = pl.multiple_of((k * ROW_TILE + r) * SUBLANES, SUBLANES)
        return pltpu.make_async_copy(ys_ref.at[dest_ref[r * TOP_K + k]], buf.at[pl.ds(r0, SUBLANES)], sem)

    def start(r, _):
        for k in range(TOP_K):
            row_copy(r, k).start(priority=k % 2)
        return 0

    lax.fori_loop(0, ROW_TILE, start, 0)
    pltpu.make_async_copy(buf, buf, sem).wait()
    w = [tw_ref[0, :, k:k + 1] for k in range(TOP_K)]
    for c in range(SUBLANES):
        acc = _load_row_tiles(buf, 0, ROW_TILE, c) * w[0]
        for k in range(1, TOP_K):
            acc = acc + _load_row_tiles(buf, k * ROW_TILE, ROW_TILE, c) * w[k]
        cols = slice(c * LANES, (c + 1) * LANES)
        o_ref[0, :, cols] = x_ref[0, :, cols] + mod_ref[0, 5:6, cols] * acc


def _comb_call(dest_flat, s, mod, tw, ys, first_tile):
    b, rows, d = s.shape
    nt = rows // ROW_TILE

    def mod_idx(i, j):
        return (jnp.where(j + first_tile == 0, b, i), 0, 0)

    return pl.pallas_call(
        _comb_kernel,
        grid=(b, nt),
        in_specs=[
            pl.BlockSpec((ROW_TILE * TOP_K,), lambda i, j: (i * nt + j,), memory_space=pltpu.SMEM),
            pl.BlockSpec((1, ROW_TILE, d), lambda i, j: (i, j, 0)),
            pl.BlockSpec((1, N_MOD, d), mod_idx),
            pl.BlockSpec((1, ROW_TILE, TOP_K), lambda i, j: (i, j, 0)),
            pl.BlockSpec(memory_space=pl.ANY),
        ],
        out_specs=pl.BlockSpec((1, ROW_TILE, d), lambda i, j: (i, j, 0)),
        out_shape=jax.ShapeDtypeStruct((b, rows, d), F32),
        scratch_shapes=[pltpu.VMEM((TOP_K * ROW_TILE * SUBLANES, LANES), F32), pltpu.SemaphoreType.DMA(())],
        compiler_params=pltpu.CompilerParams(dimension_semantics=("arbitrary", "arbitrary"),
                                             vmem_limit_bytes=VMEM_LIMIT),
        name="combine",
    )(dest_flat, s, mod, tw, ys)


def _rope_tables(seq_len):
    rows = seq_len // GRID_W
    row = jnp.broadcast_to(jnp.arange(rows, dtype=F32)[:, None], (rows, GRID_W)).reshape(seq_len)
    col = jnp.broadcast_to(jnp.arange(GRID_W, dtype=F32)[None, :], (rows, GRID_W)).reshape(seq_len)
    inv = ROPE_BASE ** (-jnp.arange(ROPE_PAIRS, dtype=F32) / ROPE_PAIRS)
    ang = jnp.stack([row, col], axis=0)[:, None, :] * inv[None, :, None]
    cos_l = jnp.cos(ang).reshape(2 * ROPE_PAIRS, seq_len)
    sin_l = jnp.sin(ang).reshape(2 * ROPE_PAIRS, seq_len)
    cos_t = jnp.concatenate([jnp.ones((2 * ROPE_PAIRS, CTX_LEN), F32), cos_l], axis=1)
    sin_t = jnp.concatenate([jnp.zeros((2 * ROPE_PAIRS, CTX_LEN), F32), sin_l], axis=1)
    return cos_t, sin_t


def kernel(x, c, ctx, c_ctx, w_mod, b_mod, norm1_g, norm2_g, w_in, q_norm_g, k_norm_g,
           lambda_q1, lambda_k1, lambda_q2, lambda_k2, subln_g, conv_w, pool_w, pool_scale, w_out,
           router_w, router_b, w_gate_up, b_gate_up, w_down, b_down):
    b, seq_len, d = x.shape
    assert d == SUBLANES * LANES
    depth = w_mod.shape[0]
    la = CTX_LEN + seq_len

    rows = -(-(b + 1) // SUBLANES) * SUBLANES
    c_all = jnp.zeros((rows, d), F32).at[:b].set(c).at[b].set(c_ctx)
    mod = _mod_call(c_all, w_mod, b_mod).reshape(depth, rows, N_MOD, d)

    cos_t, sin_t = _rope_tables(seq_len)
    s = jnp.concatenate([ctx, x], axis=1)

    for l in range(depth):
        first_tile = 1 if l + 1 == depth else 0
        t_all = b * (la - first_tile * ROW_TILE)
        n_rows = -(-(t_all * TOP_K + N_EXPERTS * (EXPERT_ROWS - 1)) // EXPERT_ROWS) * EXPERT_ROWS
        nb = n_rows // EXPERT_ROWS
        lam_init = 0.8 - 0.6 * math.exp(-0.3 * l)
        wl = w_in[l]
        wm = wl[:, 2 * ATTN_W:].astype(BF16)
        wqk = wl[:, :2 * ATTN_W].T.astype(BF16)
        q_gain = jnp.tile(q_norm_g[l], ATTN_W // QK_DIM) * (LOG2E * QK_DIM ** -0.5)
        k_gain = jnp.tile(k_norm_g[l], ATTN_W // QK_DIM)
        gain = jnp.broadcast_to(jnp.concatenate([q_gain, k_gain])[:, None], (2 * ATTN_W, ROW_TILE))
        q, kt, v, cp = _in_call(s, mod[l], norm1_g[l][None], wm, wqk, gain, cos_t, sin_t)

        lam_p = jnp.zeros((SUBLANES, LANES), F32)
        lam_p = lam_p.at[0, :QK_DIM].set(lambda_q1[l]).at[1, :QK_DIM].set(lambda_k1[l])
        lam_p = lam_p.at[2, :QK_DIM].set(lambda_q2[l]).at[3, :QK_DIM].set(lambda_k2[l])
        sg = jnp.tile(subln_g[l], 2)[None]
        oa = _attn_call(lam_p, q, kt, v, sg, lam_init, first_tile)

        pw = jax.scipy.linalg.block_diag(*[pool_w[l, g] for g in range(len(POOL_WINDOWS))]).astype(BF16)
        wr = jnp.zeros((d, LANES), F32).at[:, :N_EXPERTS].set(router_w[l])
        wrh, wrl = _split_bf16(wr)
        br = jnp.full((1, LANES), NEG_BIG, F32).at[0, :N_EXPERTS].set(router_b[l])
        s, hu, te, tw = _out_call(s, oa, cp, conv_w[l], pw, pool_scale[l][None], w_out[l].astype(BF16),
                                  mod[l], norm2_g[l][None], wrh, wrl, br, seq_len, first_tile)

        dest, counts = _rank_call(te.reshape(t_all, TOP_K))
        cnt = counts[0, :N_EXPERTS]
        padded = (cnt + EXPERT_ROWS - 1) // EXPERT_ROWS * EXPERT_ROWS
        pad_end = jnp.cumsum(padded)
        n_used = (pad_end[-1] // EXPERT_ROWS).astype(I32)
        blk = jnp.minimum(jnp.arange(nb, dtype=I32), n_used - 1) * EXPERT_ROWS
        block_e = jnp.sum((pad_end[None, :] <= blk[:, None]).astype(I32), axis=1)
        block_e = jnp.minimum(block_e, N_EXPERTS - 1)
        dest_flat = dest.reshape(t_all * TOP_K)

        pad0 = jnp.concatenate([pad_end - padded + cnt, pad_end[-1:]])
        npad = jnp.concatenate([padded - cnt, n_rows - pad_end[-1:]])
        xs = _disp_call(pad0, npad, dest_flat, hu.reshape(t_all, SUBLANES, LANES), n_rows)
        ys = _exp_call(block_e, n_used.reshape(1), xs.reshape(n_rows * SUBLANES, LANES), l,
                       w_gate_up, b_gate_up[l], w_down, b_down[l])
        s = _comb_call(dest_flat, s, mod[l], tw, ys.reshape(n_rows, SUBLANES, LANES), first_tile)
    return s
```

```python
import functools
import math

import jax
import jax.numpy as jnp
from jax import lax
from jax.experimental import pallas as pl
from jax.experimental.pallas import tpu as pltpu

F32 = jnp.float32
BF16 = jnp.bfloat16
I32 = jnp.int32
U32 = jnp.uint32

GRID_W = 64
CTX_LEN = 256
HEAD_DIM = 64
QK_DIM = 32
N_HEADS = 8
ATTN_W = 512
CONV_W = 256
POOL_W = 256
POOL_WINDOWS = (2, 4, 8, 16)
POOL_GC = 64
ROPE_BASE = 10000.0
ROPE_PAIRS = 8
N_EXPERTS = 32
TOP_K = 4
N_MOD = 6
EPS = 1e-6
SWIGLU_ALPHA = 1.702
SWIGLU_LIMIT = 7.0
LOG2E = 1.4426950408889634

LANES = 128
SUBLANES = 8
ROW_TILE = 256
N_PAIRS = N_HEADS // 2
POOL_HALO = 8
EXPERT_ROWS = 512
RANK_TILE = 512
ZERO_RUN = 64
KEY_UNROLL = 16
VMEM_LIMIT = 48 * 1024 * 1024
ATTN_VMEM_LIMIT = 40 * 1024 * 1024
EXPERT_VMEM_LIMIT = 58 * 1024 * 1024
NEG_BIG = -1e30
BOUND_SLACK = 1.0 + 2.0 ** -6
UNDERFLOW_GUARD = 2.0 ** -100


def _split_bf16(a):
    hi = a.astype(BF16)
    lo = (a - hi.astype(F32)).astype(BF16)
    return hi, lo


def _dot(a, b):
    return jnp.dot(a, b, preferred_element_type=F32)


def _mod_kernel(c_ref, w_ref, b_ref, o_ref):
    cv = c_ref[...]
    s = cv * jax.nn.sigmoid(cv)
    s_hi, s_lo = _split_bf16(s)
    w_hi, w_lo = _split_bf16(w_ref[0])
    o_ref[0] = _dot(s_hi, w_hi) + _dot(s_lo, w_hi) + _dot(s_hi, w_lo) + b_ref[0]


def _mod_call(c_all, w_mod, b_mod):
    depth, d, n = w_mod.shape
    r = c_all.shape[0]
    tn = 1536
    return pl.pallas_call(
        _mod_kernel,
        grid=(depth, n // tn),
        in_specs=[
            pl.BlockSpec((r, d), lambda l, j: (0, 0)),
            pl.BlockSpec((1, d, tn), lambda l, j: (l, 0, j)),
            pl.BlockSpec((1, 1, tn), lambda l, j: (l, 0, j)),
        ],
        out_specs=pl.BlockSpec((1, r, tn), lambda l, j: (l, 0, j)),
        out_shape=jax.ShapeDtypeStruct((depth, r, n), F32),
        compiler_params=pltpu.CompilerParams(vmem_limit_bytes=VMEM_LIMIT),
        name="mod",
    )(c_all, w_mod, b_mod.reshape(depth, 1, n))


def _rms_mod(x, g, shift, scale):
    y = x * lax.rsqrt(jnp.mean(x * x, axis=-1, keepdims=True) + EPS) * g
    return y * (1.0 + scale) + shift


def _stream_tile(c_ref, x_ref, j):
    return jnp.where(j == 0, c_ref[0], x_ref[0])


def _stream_specs(d, first_tile, x_is_latent):
    shift = first_tile - (1 if x_is_latent else 0)
    return [pl.BlockSpec((1, ROW_TILE, d), lambda i, j: (i, 0, 0)),
            pl.BlockSpec((1, ROW_TILE, d), lambda i, j: (i, jnp.maximum(j + shift, 0), 0))]


def _in_kernel(c_ref, x_ref, mod_ref, g1_ref, wm_ref, wqk_ref, gain_ref, cos_ref, sin_ref,
               q_ref, kt_ref, v_ref, cp_ref):
    tm = x_ref.shape[1]
    xin = _stream_tile(c_ref, x_ref, pl.program_id(1))
    h = _rms_mod(xin, g1_ref[...], mod_ref[0, 0:1, :], mod_ref[0, 1:2, :])
    pm = _dot(h.astype(BF16), wm_ref[...])
    for p in range(N_PAIRS):
        v_ref[0, p] = pm[:, p * LANES:(p + 1) * LANES].astype(BF16)
    o = ATTN_W
    cp_ref[0, :, 0:CONV_W] = pm[:, o:o + CONV_W]
    cp_ref[0, :, CONV_W:2 * CONV_W] = pm[:, o + CONV_W:o + 2 * CONV_W] * pm[:, o + 2 * CONV_W:o + 3 * CONV_W]
    cp_ref[0, :, 2 * CONV_W:] = pm[:, o + 3 * CONV_W:]

    ht = h.T.astype(BF16)
    qkt = _dot(wqk_ref[...], ht)
    ng = 2 * ATTN_W // QK_DIM
    t = qkt.reshape(ng, 4, ROPE_PAIRS, tm)
    ss = jnp.sum(jnp.sum(t * t, axis=2, keepdims=True), axis=1, keepdims=True)
    tn = t * lax.rsqrt(ss * (1.0 / QK_DIM) + EPS) * gain_ref[...].reshape(ng, 4, ROPE_PAIRS, tm)
    cs = cos_ref[...].reshape(2, ROPE_PAIRS, tm)
    sn = sin_ref[...].reshape(2, ROPE_PAIRS, tm)
    parts = []
    for a in range(2):
        t1 = tn[:, 2 * a]
        t2 = tn[:, 2 * a + 1]
        parts.append(t1 * cs[a] - t2 * sn[a])
        parts.append(t2 * cs[a] + t1 * sn[a])
    rot = jnp.stack(parts, axis=1).reshape(2 * ATTN_W, tm)
    qt = rot[:ATTN_W].T
    for p in range(N_PAIRS):
        q_ref[0, p] = qt[:, p * LANES:(p + 1) * LANES].astype(BF16)
        kt_ref[0, 0, p] = rot[ATTN_W + p * LANES:ATTN_W + (p + 1) * LANES].astype(BF16)


def _in_call(s_ctx, s_x, la, mod, g1, wm, wqk, gain, cos_t, sin_t):
    b, _, d = s_x.shape
    nt = la // ROW_TILE
    nmain = wm.shape[1]

    def mod_idx(i, j):
        return (jnp.where(j == 0, b, i), 0, 0)

    return pl.pallas_call(
        _in_kernel,
        grid=(b, nt),
        in_specs=_stream_specs(d, 0, s_x.shape[1] < la) + [
            pl.BlockSpec((1, N_MOD, d), mod_idx),
            pl.BlockSpec((1, d), lambda i, j: (0, 0)),
            pl.BlockSpec((d, nmain), lambda i, j: (0, 0)),
            pl.BlockSpec((2 * ATTN_W, d), lambda i, j: (0, 0)),
            pl.BlockSpec((2 * ATTN_W, ROW_TILE), lambda i, j: (0, 0)),
            pl.BlockSpec((2 * ROPE_PAIRS, ROW_TILE), lambda i, j: (0, j)),
            pl.BlockSpec((2 * ROPE_PAIRS, ROW_TILE), lambda i, j: (0, j)),
        ],
        out_specs=[
            pl.BlockSpec((1, N_PAIRS, ROW_TILE, LANES), lambda i, j: (i, 0, j, 0)),
            pl.BlockSpec((1, 1, N_PAIRS, LANES, ROW_TILE), lambda i, j: (i, j, 0, 0, 0)),
            pl.BlockSpec((1, N_PAIRS, ROW_TILE, LANES), lambda i, j: (i, 0, j, 0)),
            pl.BlockSpec((1, ROW_TILE, 3 * CONV_W), lambda i, j: (i, j, 0)),
        ],
        out_shape=[
            jax.ShapeDtypeStruct((b, N_PAIRS, la, LANES), BF16),
            jax.ShapeDtypeStruct((b, nt, N_PAIRS, LANES, ROW_TILE), BF16),
            jax.ShapeDtypeStruct((b, N_PAIRS, la, LANES), BF16),
            jax.ShapeDtypeStruct((b, la, 3 * CONV_W), F32),
        ],
        compiler_params=pltpu.CompilerParams(vmem_limit_bytes=VMEM_LIMIT),
        name="inproj",
    )(s_ctx, s_x, mod, g1, wm, wqk, gain, cos_t, sin_t)


def _attn_kernel(lam_ref, mb_ref, q_ref, kt_ref, v_ref, sg_ref, o_ref, mb_scr, l_scr, acc_scr, *,
                 lam_init, first_tile):
    tq = q_ref.shape[2]
    nkc_all = kt_ref.shape[1]
    j = pl.program_id(1) + first_tile
    n_groups = jnp.where(j == 0, 0, (nkc_all - 1) // KEY_UNROLL)
    n_chunks = jnp.where(j == 0, 1, nkc_all)
    lp = lam_ref[...]
    lam = (jnp.exp(jnp.sum(lp[0:1] * lp[1:2], axis=-1, keepdims=True))
           - jnp.exp(jnp.sum(lp[2:3] * lp[3:4], axis=-1, keepdims=True)) + lam_init)
    lane = lax.broadcasted_iota(I32, (1, LANES), 1)
    low = lane < HEAD_DIM
    groups = [(lane >= g * QK_DIM) & (lane < (g + 1) * QK_DIM) for g in range(4)]
    bound = mb_ref[...]

    for p in range(N_PAIRS):
        qp = q_ref[0, p]
        qs = jnp.concatenate([jnp.where(groups[g], qp, jnp.zeros_like(qp)) for g in range(4)], axis=0)

        def exp_pv(kc, mb, qs=qs, p=p):
            sc = _dot(qs, kt_ref[0, kc, p])
            e0 = jnp.exp2(sc[:, :LANES] - mb)
            e1 = jnp.exp2(sc[:, LANES:] - mb)
            r0 = pl.multiple_of(kc * ROW_TILE, ROW_TILE)
            e = jnp.concatenate([e0.astype(BF16), e1.astype(BF16)], axis=1)
            return e0 + e1, _dot(e, v_ref[0, p, pl.ds(r0, ROW_TILE), :])

        def sweep(g, _):
            parts = [exp_pv(1 + g * KEY_UNROLL + u, bound) for u in range(KEY_UNROLL)]
            while len(parts) > 1:
                parts = [(a[0] + b[0], a[1] + b[1]) for a, b in zip(parts[::2], parts[1::2])]
            l_scr[...] = l_scr[...] + parts[0][0]
            acc_scr[...] = acc_scr[...] + parts[0][1]
            return 0

        l_scr[...], acc_scr[...] = exp_pv(0, bound)
        lax.fori_loop(0, n_groups, sweep, 0)
        lmin = jnp.min(jnp.sum(l_scr[...], axis=-1, keepdims=True))

        @pl.when(jnp.logical_not(lmin >= UNDERFLOW_GUARD))
        def _():
            def row_max(kc, m, qs=qs, p=p):
                sc = _dot(qs, kt_ref[0, kc, p])
                return jnp.maximum(m, jnp.maximum(sc[:, :LANES], sc[:, LANES:]))
            m = lax.fori_loop(0, n_chunks, row_max, jnp.full((4 * tq, LANES), -jnp.inf, F32))
            mb_scr[...] = jnp.broadcast_to(jnp.max(m, axis=-1, keepdims=True), mb_scr.shape)
            l_scr[...] = jnp.zeros(l_scr.shape, F32)
            acc_scr[...] = jnp.zeros(acc_scr.shape, F32)

            def redo(kc, _):
                dl, dacc = exp_pv(kc, mb_scr[...])
                l_scr[...] = l_scr[...] + dl
                acc_scr[...] = acc_scr[...] + dacc
                return 0
            lax.fori_loop(0, n_chunks, redo, 0)

        on = acc_scr[...] / jnp.sum(l_scr[...], axis=-1, keepdims=True)
        heads = [on[(2 * hh) * tq:(2 * hh + 1) * tq] - lam * on[(2 * hh + 1) * tq:(2 * hh + 2) * tq]
                 for hh in range(2)]
        o = jnp.where(low, heads[0], heads[1])
        o2 = o * o
        ss_lo = jnp.sum(jnp.where(low, o2, 0.0), axis=-1, keepdims=True)
        ss_hi = jnp.sum(jnp.where(low, 0.0, o2), axis=-1, keepdims=True)
        inv = lax.rsqrt(jnp.where(low, ss_lo, ss_hi) * (1.0 / HEAD_DIM) + EPS)
        o_ref[0, p] = (o * inv * sg_ref[...] * (1.0 - lam_init)).astype(BF16)


def _attn_call(lam_p, score_bound, q, kt, v, sg, lam_init, first_tile):
    b, _, la, _ = q.shape
    nt = la // ROW_TILE
    assert (nt - 1) % KEY_UNROLL == 0
    off = first_tile
    return pl.pallas_call(
        functools.partial(_attn_kernel, lam_init=lam_init, first_tile=off),
        grid=(b, nt - off),
        in_specs=[
            pl.BlockSpec((SUBLANES, LANES), lambda i, j: (0, 0)),
            pl.BlockSpec((1, LANES), lambda i, j: (0, 0)),
            pl.BlockSpec((1, N_PAIRS, ROW_TILE, LANES), lambda i, j: (i, 0, j + off, 0)),
            pl.BlockSpec((1, nt, N_PAIRS, LANES, ROW_TILE), lambda i, j: (i, 0, 0, 0, 0)),
            pl.BlockSpec((1, N_PAIRS, la, LANES), lambda i, j: (i, 0, 0, 0)),
            pl.BlockSpec((1, LANES), lambda i, j: (0, 0)),
        ],
        out_specs=pl.BlockSpec((1, N_PAIRS, ROW_TILE, LANES), lambda i, j: (i, 0, j, 0)),
        out_shape=jax.ShapeDtypeStruct((b, N_PAIRS, la - off * ROW_TILE, LANES), BF16),
        scratch_shapes=[pltpu.VMEM((4 * ROW_TILE, LANES), F32),
                        pltpu.VMEM((4 * ROW_TILE, LANES), F32),
                        pltpu.VMEM((4 * ROW_TILE, LANES), F32)],
        compiler_params=pltpu.CompilerParams(vmem_limit_bytes=ATTN_VMEM_LIMIT),
        name="attn",
    )(lam_p, score_bound, q, kt, v, sg)


def _store_row_tiles(ref, val):
    rows = val.shape[0]
    for c in range(SUBLANES):
        ref[pl.ds(c, rows, stride=SUBLANES), :] = val[:, c * LANES:(c + 1) * LANES]


def _load_row_tiles(ref, first_row, rows, c):
    return ref[pl.ds(first_row * SUBLANES + c, rows, stride=SUBLANES), :]


def _shift_rows(a, k):
    return pltpu.roll(a, k % a.shape[0], axis=0)


def _out_kernel(c_ref, x_ref, oa_ref, cp_ref, cpp_ref, cpn_ref, cw_ref, pw_ref, ps_ref, wo_ref,
                mod_ref, g2_ref, wrh_ref, wrl_ref, br_ref,
                xo_ref, hu_ref, te_ref, tw_ref, *, seq_len, first_tile):
    tm = x_ref.shape[1]
    j = pl.program_id(1) + first_tile
    nt = pl.num_programs(1) + first_tile
    halo = POOL_HALO
    has_prev = j >= 2
    has_next = (j >= 1) & (j < nt - 1)
    prev = jnp.where(has_prev, cpp_ref[0], 0.0)
    nxt = jnp.where(has_next, cpn_ref[0], 0.0)
    ext = jnp.concatenate([prev, cp_ref[0], nxt], axis=0)

    z = ext[:, CONV_W:2 * CONV_W]
    conv = (cw_ref[0:1, :] * _shift_rows(z, 1) + cw_ref[1:2, :] * z + cw_ref[2:3, :] * _shift_rows(z, -1))
    o_conv = ext[halo:halo + tm, 0:CONV_W] * conv[halo:halo + tm]

    u = ext[:, 2 * CONV_W:]
    a2 = u + _shift_rows(u, 1)
    a4 = _shift_rows(a2, -1) + _shift_rows(a2, 1)
    a8 = _shift_rows(a4, -2) + _shift_rows(a4, 2)
    a16 = _shift_rows(a8, -4) + _shift_rows(a8, 4)
    lane = lax.broadcasted_iota(I32, (tm, POOL_W), 1)
    grp = lane // POOL_GC
    wsum = jnp.where(grp == 0, a2[halo:halo + tm],
                     jnp.where(grp == 1, a4[halo:halo + tm],
                               jnp.where(grp == 2, a8[halo:halo + tm], a16[halo:halo + tm])))
    half = jnp.where(grp == 0, 1, jnp.where(grp == 1, 2, jnp.where(grp == 2, 4, 8)))
    row = lax.broadcasted_iota(I32, (tm, POOL_W), 0)
    pos = jnp.where(j == 0, row, (j - 1) * tm + row)
    n_seq = jnp.where(j == 0, CTX_LEN, seq_len)
    cnt = jnp.minimum(pos + half, n_seq) - jnp.maximum(pos - half, 0)
    dlt = wsum / cnt.astype(F32) - u[halo:halo + tm]
    o_pool = _dot(dlt.astype(BF16), pw_ref[...]) * ps_ref[...]

    mix = _dot(o_conv.astype(BF16), wo_ref[ATTN_W:ATTN_W + CONV_W, :])
    mix = mix + _dot(o_pool.astype(BF16), wo_ref[ATTN_W + CONV_W:, :])
    for p in range(N_PAIRS):
        mix = mix + _dot(oa_ref[0, p], wo_ref[p * LANES:(p + 1) * LANES, :])
    x = _stream_tile(c_ref, x_ref, j) + mod_ref[0, 2:3, :] * mix
    xo_ref[0] = x

    h2 = _rms_mod(x, g2_ref[...], mod_ref[0, 3:4, :], mod_ref[0, 4:5, :])
    _store_row_tiles(hu_ref.at[0], h2)

    h_hi, h_lo = _split_bf16(h2)
    logits = (_dot(h_hi, wrh_ref[...]) + _dot(h_lo, wrh_ref[...]) + _dot(h_hi, wrl_ref[...])
              + br_ref[...])
    lanef = lax.broadcasted_iota(I32, (tm, LANES), 1).astype(F32)
    work = logits
    tops = []
    for k in range(TOP_K):
        mk = jnp.max(work, axis=-1, keepdims=True)
        ik = jnp.min(jnp.where(work == mk, lanef, float(LANES)), axis=-1, keepdims=True)
        te_ref[0, :, k:k + 1] = ik.astype(I32)
        work = jnp.where(lanef == ik, -jnp.inf, work)
        tops.append(mk)
    es = [jnp.exp(mk - tops[0]) for mk in tops]
    den = es[0] + es[1] + es[2] + es[3]
    for k in range(TOP_K):
        tw_ref[0, :, k:k + 1] = es[k] / den


def _out_call(s_ctx, s_x, oa, cp, cw, pw, ps, wo, mod, g2, wrh, wrl, br, seq_len, first_tile):
    b, _, d = s_x.shape
    la = CTX_LEN + seq_len
    nt = la // ROW_TILE
    off = first_tile
    lo = la - off * ROW_TILE
    hb = ROW_TILE // POOL_HALO
    nhb = la // POOL_HALO

    def mod_idx(i, j):
        return (jnp.where(j + off == 0, b, i), 0, 0)

    const2 = lambda i, j: (0, 0)
    return pl.pallas_call(
        functools.partial(_out_kernel, seq_len=seq_len, first_tile=off),
        grid=(b, nt - off),
        in_specs=_stream_specs(d, off, s_x.shape[1] < la) + [
            pl.BlockSpec((1, N_PAIRS, ROW_TILE, LANES), lambda i, j: (i, 0, j, 0)),
            pl.BlockSpec((1, ROW_TILE, 3 * CONV_W), lambda i, j: (i, j + off, 0)),
            pl.BlockSpec((1, POOL_HALO, 3 * CONV_W), lambda i, j: (i, jnp.maximum((j + off) * hb - 1, 0), 0)),
            pl.BlockSpec((1, POOL_HALO, 3 * CONV_W),
                         lambda i, j: (i, jnp.minimum((j + off + 1) * hb, nhb - 1), 0)),
            pl.BlockSpec((3, CONV_W), const2),
            pl.BlockSpec((POOL_W, POOL_W), const2),
            pl.BlockSpec((1, POOL_W), const2),
            pl.BlockSpec((d, d), const2),
            pl.BlockSpec((1, N_MOD, d), mod_idx),
            pl.BlockSpec((1, d), const2),
            pl.BlockSpec((d, LANES), const2),
            pl.BlockSpec((d, LANES), const2),
            pl.BlockSpec((1, LANES), const2),
        ],
        out_specs=[
            pl.BlockSpec((1, ROW_TILE, d), lambda i, j: (i, j, 0)),
            pl.BlockSpec((1, ROW_TILE * SUBLANES, LANES), lambda i, j: (i, j, 0)),
            pl.BlockSpec((1, ROW_TILE, TOP_K), lambda i, j: (i, j, 0)),
            pl.BlockSpec((1, ROW_TILE, TOP_K), lambda i, j: (i, j, 0)),
        ],
        out_shape=[
            jax.ShapeDtypeStruct((b, lo, d), F32),
            jax.ShapeDtypeStruct((b, lo * SUBLANES, LANES), F32),
            jax.ShapeDtypeStruct((b, lo, TOP_K), I32),
            jax.ShapeDtypeStruct((b, lo, TOP_K), F32),
        ],
        compiler_params=pltpu.CompilerParams(vmem_limit_bytes=VMEM_LIMIT),
        name="outproj",
    )(s_ctx, s_x, oa, cp, cp, cp, cw, pw, ps, wo, mod, g2, wrh, wrl, br)


def _rank_kernel(te_ref, tri_ref, dest_ref, cnt_ref, carry):
    ph = pl.program_id(0)
    i = pl.program_id(1)
    tr = te_ref.shape[0]
    lane1 = lax.broadcasted_iota(I32, (SUBLANES, LANES), 1)

    @pl.when((ph == 0) & (i == 0))
    def _():
        carry[...] = jnp.zeros_like(carry)

    @pl.when((ph == 1) & (i == 0))
    def _():
        cnt = carry[...]
        cnt_ref[...] = cnt.astype(I32)
        padded = jnp.ceil(cnt * (1.0 / EXPERT_ROWS)) * EXPERT_ROWS
        incl = padded
        for sh in (1, 2, 4, 8, 16):
            incl = incl + jnp.where(lane1 >= sh, pltpu.roll(incl, sh, axis=1), 0.0)
        carry[...] = incl - padded

    e = te_ref[...]
    lane = lax.broadcasted_iota(I32, (tr, LANES), 1)
    ohs = [(e[:, k:k + 1] == lane).astype(F32) for k in range(TOP_K)]
    m = ohs[0] + ohs[1] + ohs[2] + ohs[3]

    @pl.when(ph == 0)
    def _():
        dest_ref[...] = jnp.zeros(dest_ref.shape, I32)

    @pl.when(ph == 1)
    def _():
        base = _dot(tri_ref[...], m.astype(BF16)) + carry[0:1, :]
        for k in range(TOP_K):
            dest_ref[:, k:k + 1] = jnp.sum(ohs[k] * base, axis=-1, keepdims=True).astype(I32)

    carry[...] = carry[...] + jnp.sum(m, axis=0, keepdims=True)


def _rank_call(te):
    t = te.shape[0]
    nt = t // RANK_TILE
    r = lax.broadcasted_iota(I32, (RANK_TILE, RANK_TILE), 0)
    c = lax.broadcasted_iota(I32, (RANK_TILE, RANK_TILE), 1)
    tri = (c < r).astype(BF16)
    return pl.pallas_call(
        _rank_kernel,
        grid=(2, nt),
        in_specs=[
            pl.BlockSpec((RANK_TILE, TOP_K), lambda ph, i: (i, 0)),
            pl.BlockSpec((RANK_TILE, RANK_TILE), lambda ph, i: (0, 0)),
        ],
        out_specs=[
            pl.BlockSpec((RANK_TILE, TOP_K), lambda ph, i: (ph * i, 0)),
            pl.BlockSpec((SUBLANES, LANES), lambda ph, i: (0, 0)),
        ],
        out_shape=[
            jax.ShapeDtypeStruct((t, TOP_K), I32),
            jax.ShapeDtypeStruct((SUBLANES, LANES), I32),
        ],
        scratch_shapes=[pltpu.VMEM((SUBLANES, LANES), F32)],
        compiler_params=pltpu.CompilerParams(dimension_semantics=("arbitrary", "arbitrary")),
        name="rank",
    )(te, tri)


def _disp_kernel(pad0_ref, npad_ref, dest_ref, hu_ref, xs_ref, zero, sem, zsem):
    def row_copy(r, k):
        return pltpu.make_async_copy(hu_ref.at[r], xs_ref.at[dest_ref[r * TOP_K + k]], sem)

    def start(r, _):
        for k in range(TOP_K):
            row_copy(r, k).start(priority=k % 2)
        return 0

    lax.fori_loop(0, ROW_TILE, start, 0)
    for _ in range(TOP_K):
        pltpu.make_async_copy(hu_ref, hu_ref, sem).wait()

    @pl.when(pl.program_id(0) == pl.num_programs(0) - 1)
    def _():
        zero[...] = jnp.zeros(zero.shape, F32)
        run = zero.shape[0]

        def zero_range(e, _):
            n_runs = npad_ref[e] // run
            n_rows = npad_ref[e] - n_runs * run
            row0 = pad0_ref[e] + n_runs * run

            def run_copy(c):
                return pltpu.make_async_copy(zero, xs_ref.at[pl.ds(pad0_ref[e] + c * run, run)], zsem)

            def row_copy0(r):
                return pltpu.make_async_copy(zero.at[0], xs_ref.at[row0 + r], zsem)

            def loop(n, make, wait):
                def body(c, carry):
                    if wait:
                        make(c).wait()
                    else:
                        make(c).start()
                    return carry
                lax.fori_loop(0, n, body, 0)

            loop(n_runs, run_copy, False)
            loop(n_rows, row_copy0, False)
            loop(n_runs, run_copy, True)
            loop(n_rows, row_copy0, True)
            return 0

        lax.fori_loop(0, pad0_ref.shape[0], zero_range, 0)


def _disp_call(pad0, npad, dest_flat, hu, n_rows):
    t = hu.shape[0]
    grid_spec = pltpu.PrefetchScalarGridSpec(
        num_scalar_prefetch=2,
        grid=(t // ROW_TILE,),
        in_specs=[
            pl.BlockSpec((ROW_TILE * TOP_K,), lambda i, p0, n: (i,), memory_space=pltpu.SMEM),
            pl.BlockSpec((ROW_TILE, SUBLANES, LANES), lambda i, p0, n: (i, 0, 0)),
        ],
        out_specs=pl.BlockSpec(memory_space=pl.ANY),
        scratch_shapes=[pltpu.VMEM((ZERO_RUN, SUBLANES, LANES), F32), pltpu.SemaphoreType.DMA(()),
                        pltpu.SemaphoreType.DMA(())],
    )
    return pl.pallas_call(
        _disp_kernel,
        grid_spec=grid_spec,
        out_shape=jax.ShapeDtypeStruct((n_rows, SUBLANES, LANES), F32),
        compiler_params=pltpu.CompilerParams(dimension_semantics=("arbitrary",)),
        name="dispatch",
    )(pad0, npad, dest_flat, hu)


def _exp_kernel(be_ref, nu_ref, xs_ref, wgu_ref, bgu_ref, wdn_ref, bdn_ref, ys_ref, wgu_b, wdn_b):
    i = pl.program_id(0)
    ff = wdn_ref.shape[1]
    rows = xs_ref.shape[0] // SUBLANES

    @pl.when((i == 0) | (be_ref[i] != be_ref[jnp.maximum(i - 1, 0)]))
    def _():
        wgu_b[...] = wgu_ref[0].astype(BF16)
        wdn_b[...] = wdn_ref[0].astype(BF16)

    @pl.when(i < nu_ref[0])
    def _():
        xb = jnp.concatenate([_load_row_tiles(xs_ref, 0, rows, c).astype(BF16) for c in range(SUBLANES)],
                             axis=1)
        gu = _dot(xb, wgu_b[...]) + bgu_ref[0]
        glu = jnp.minimum(gu[:, :ff], SWIGLU_LIMIT)
        lin = jnp.clip(gu[:, ff:], -SWIGLU_LIMIT, SWIGLU_LIMIT)
        act = glu * jax.nn.sigmoid(SWIGLU_ALPHA * glu) * (lin + 1.0)
        _store_row_tiles(ys_ref, _dot(act.astype(BF16), wdn_b[...]) + bdn_ref[0])

    @pl.when(i >= nu_ref[0])
    def _():
        ys_ref[...] = jnp.zeros_like(ys_ref)


def _exp_call(block_e, n_used, xs, layer, wgu, bgu, wdn, bdn):
    n_rows = xs.shape[0] // SUBLANES
    _, _, d, ff2 = wgu.shape
    ff = ff2 // 2
    nb = n_rows // EXPERT_ROWS
    grid_spec = pltpu.PrefetchScalarGridSpec(
        num_scalar_prefetch=2,
        grid=(nb,),
        in_specs=[
            pl.BlockSpec((EXPERT_ROWS * SUBLANES, LANES), lambda i, be, nu: (jnp.minimum(i, nu[0] - 1), 0)),
            pl.BlockSpec((None, 1, d, ff2), lambda i, be, nu: (layer, be[i], 0, 0)),
            pl.BlockSpec((1, 1, ff2), lambda i, be, nu: (be[i], 0, 0)),
            pl.BlockSpec((None, 1, ff, d), lambda i, be, nu: (layer, be[i], 0, 0)),
            pl.BlockSpec((1, 1, d), lambda i, be, nu: (be[i], 0, 0)),
        ],
        out_specs=pl.BlockSpec((EXPERT_ROWS * SUBLANES, LANES), lambda i, be, nu: (i, 0)),
        scratch_shapes=[pltpu.VMEM((d, ff2), BF16), pltpu.VMEM((ff, d), BF16)],
    )
    return pl.pallas_call(
        _exp_kernel,
        grid_spec=grid_spec,
        out_shape=jax.ShapeDtypeStruct((n_rows * SUBLANES, LANES), F32),
        compiler_params=pltpu.CompilerParams(dimension_semantics=("arbitrary",),
                                             vmem_limit_bytes=EXPERT_VMEM_LIMIT),
        name="experts",
    )(block_e, n_used, xs, wgu, bgu.reshape(-1, 1, ff2), wdn, bdn.reshape(-1, 1, d))


def _comb_kernel(dest_ref, x_ref, mod_ref, tw_ref, ys_ref, o_ref, buf, sem):
    def row_copy(r, k):
        r0 = pl.multiple_of((k * ROW_TILE + r) * SUBLANES, SUBLANES)
        return pltpu.make_async_copy(ys_ref.at[dest_ref[r * TOP_K + k]], buf.at[pl.ds(r0, SUBLANES)], sem)

    def start(r, _):
        for k in range(TOP_K):
            row_copy(r, k).start(priority=k % 2)
        return 0

    lax.fori_loop(0, ROW_TILE, start, 0)
    pltpu.make_async_copy(buf, buf, sem).wait()
    w = [tw_ref[0, :, k:k + 1] for k in range(TOP_K)]
    for c in range(SUBLANES):
        acc = _load_row_tiles(buf, 0, ROW_TILE, c) * w[0]
        for k in range(1, TOP_K):
            acc = acc + _load_row_tiles(buf, k * ROW_TILE, ROW_TILE, c) * w[k]
        cols = slice(c * LANES, (c + 1) * LANES)
        o_ref[0, :, cols] = x_ref[0, :, cols] + mod_ref[0, 5:6, cols] * acc


def _comb_call(dest_flat, s, mod, tw, ys, first_tile):
    b, rows, d = s.shape
    nt = rows // ROW_TILE

    def mod_idx(i, j):
        return (jnp.where(j + first_tile == 0, b, i), 0, 0)

    return pl.pallas_call(
        _comb_kernel,
        grid=(b, nt),
        in_specs=[
            pl.BlockSpec((ROW_TILE * TOP_K,), lambda i, j: (i * nt + j,), memory_space=pltpu.SMEM),
            pl.BlockSpec((1, ROW_TILE, d), lambda i, j: (i, j, 0)),
            pl.BlockSpec((1, N_MOD, d), mod_idx),
            pl.BlockSpec((1, ROW_TILE, TOP_K), lambda i, j: (i, j, 0)),
            pl.BlockSpec(memory_space=pl.ANY),
        ],
        out_specs=pl.BlockSpec((1, ROW_TILE, d), lambda i, j: (i, j, 0)),
        out_shape=jax.ShapeDtypeStruct((b, rows, d), F32),
        scratch_shapes=[pltpu.VMEM((TOP_K * ROW_TILE * SUBLANES, LANES), F32), pltpu.SemaphoreType.DMA(())],
        compiler_params=pltpu.CompilerParams(dimension_semantics=("arbitrary", "arbitrary"),
                                             vmem_limit_bytes=VMEM_LIMIT),
        name="combine",
    )(dest_flat, s, mod, tw, ys)


def _rope_tables(seq_len):
    rows = seq_len // GRID_W
    row = jnp.broadcast_to(jnp.arange(rows, dtype=F32)[:, None], (rows, GRID_W)).reshape(seq_len)
    col = jnp.broadcast_to(jnp.arange(GRID_W, dtype=F32)[None, :], (rows, GRID_W)).reshape(seq_len)
    inv = ROPE_BASE ** (-jnp.arange(ROPE_PAIRS, dtype=F32) / ROPE_PAIRS)
    ang = jnp.stack([row, col], axis=0)[:, None, :] * inv[None, :, None]
    cos_l = jnp.cos(ang).reshape(2 * ROPE_PAIRS, seq_len)
    sin_l = jnp.sin(ang).reshape(2 * ROPE_PAIRS, seq_len)
    cos_t = jnp.concatenate([jnp.ones((2 * ROPE_PAIRS, CTX_LEN), F32), cos_l], axis=1)
    sin_t = jnp.concatenate([jnp.zeros((2 * ROPE_PAIRS, CTX_LEN), F32), sin_l], axis=1)
    return cos_t, sin_t


def kernel(x, c, ctx, c_ctx, w_mod, b_mod, norm1_g, norm2_g, w_in, q_norm_g, k_norm_g,
           lambda_q1, lambda_k1, lambda_q2, lambda_k2, subln_g, conv_w, pool_w, pool_scale, w_out,
           router_w, router_b, w_gate_up, b_gate_up, w_down, b_down):
    b, seq_len, d = x.shape
    assert d == SUBLANES * LANES
    depth = w_mod.shape[0]
    la = CTX_LEN + seq_len

    rows = -(-(b + 1) // SUBLANES) * SUBLANES
    c_all = jnp.zeros((rows, d), F32).at[:b].set(c).at[b].set(c_ctx)
    mod = _mod_call(c_all, w_mod, b_mod).reshape(depth, rows, N_MOD, d)

    cos_t, sin_t = _rope_tables(seq_len)
    s_ctx, s_x = ctx, x

    for l in range(depth):
        first_tile = 1 if l + 1 == depth else 0
        t_all = b * (la - first_tile * ROW_TILE)
        n_rows = -(-(t_all * TOP_K + N_EXPERTS * (EXPERT_ROWS - 1)) // EXPERT_ROWS) * EXPERT_ROWS
        nb = n_rows // EXPERT_ROWS
        lam_init = 0.8 - 0.6 * math.exp(-0.3 * l)
        wl = w_in[l]
        wm = wl[:, 2 * ATTN_W:].astype(BF16)
        wqk = wl[:, :2 * ATTN_W].T.astype(BF16)
        q_gain = jnp.tile(q_norm_g[l], ATTN_W // QK_DIM) * (LOG2E * QK_DIM ** -0.5)
        k_gain = jnp.tile(k_norm_g[l], ATTN_W // QK_DIM)
        gain = jnp.broadcast_to(jnp.concatenate([q_gain, k_gain])[:, None], (2 * ATTN_W, ROW_TILE))
        q, kt, v, cp = _in_call(s_ctx, s_x, la, mod[l], norm1_g[l][None], wm, wqk, gain, cos_t, sin_t)

        lam_p = jnp.zeros((SUBLANES, LANES), F32)
        lam_p = lam_p.at[0, :QK_DIM].set(lambda_q1[l]).at[1, :QK_DIM].set(lambda_k1[l])
        lam_p = lam_p.at[2, :QK_DIM].set(lambda_q2[l]).at[3, :QK_DIM].set(lambda_k2[l])
        sg = jnp.tile(subln_g[l], 2)[None]
        score_bound = jnp.full((1, LANES), QK_DIM * BOUND_SLACK, F32) * (
            jnp.max(jnp.abs(q_gain)) * jnp.max(jnp.abs(k_gain)))
        oa = _attn_call(lam_p, score_bound, q, kt, v, sg, lam_init, first_tile)

        pw = jax.scipy.linalg.block_diag(*[pool_w[l, g] for g in range(len(POOL_WINDOWS))]).astype(BF16)
        wr = jnp.zeros((d, LANES), F32).at[:, :N_EXPERTS].set(router_w[l])
        wrh, wrl = _split_bf16(wr)
        br = jnp.full((1, LANES), NEG_BIG, F32).at[0, :N_EXPERTS].set(router_b[l])
        s, hu, te, tw = _out_call(s_ctx, s_x, oa, cp, conv_w[l], pw, pool_scale[l][None],
                                  w_out[l].astype(BF16), mod[l], norm2_g[l][None], wrh, wrl, br,
                                  seq_len, first_tile)

        dest, counts = _rank_call(te.reshape(t_all, TOP_K))
        cnt = counts[0, :N_EXPERTS]
        padded = (cnt + EXPERT_ROWS - 1) // EXPERT_ROWS * EXPERT_ROWS
        pad_end = jnp.cumsum(padded)
        n_used = (pad_end[-1] // EXPERT_ROWS).astype(I32)
        blk = jnp.minimum(jnp.arange(nb, dtype=I32), n_used - 1) * EXPERT_ROWS
        block_e = jnp.sum((pad_end[None, :] <= blk[:, None]).astype(I32), axis=1)
        block_e = jnp.minimum(block_e, N_EXPERTS - 1)
        dest_flat = dest.reshape(t_all * TOP_K)

        pad0 = jnp.concatenate([pad_end - padded + cnt, pad_end[-1:]])
        npad = jnp.concatenate([padded - cnt, n_rows - pad_end[-1:]])
        xs = _disp_call(pad0, npad, dest_flat, hu.reshape(t_all, SUBLANES, LANES), n_rows)
        ys = _exp_call(block_e, n_used.reshape(1), xs.reshape(n_rows * SUBLANES, LANES), l,
                       w_gate_up, b_gate_up[l], w_down, b_down[l])
        s = _comb_call(dest_flat, s, mod[l], tw, ys.reshape(n_rows, SUBLANES, LANES), first_tile)
        s_ctx = s_x = s
    return s
```

```python
import functools
import math

import jax
import jax.numpy as jnp
from jax import lax
from jax.experimental import pallas as pl
from jax.experimental.pallas import tpu as pltpu

F32 = jnp.float32
BF16 = jnp.bfloat16
I32 = jnp.int32
U32 = jnp.uint32

GRID_W = 64
CTX_LEN = 256
HEAD_DIM = 64
QK_DIM = 32
N_HEADS = 8
ATTN_W = 512
CONV_W = 256
POOL_W = 256
POOL_WINDOWS = (2, 4, 8, 16)
POOL_GC = 64
ROPE_BASE = 10000.0
ROPE_PAIRS = 8
N_EXPERTS = 32
TOP_K = 4
N_MOD = 6
EPS = 1e-6
SWIGLU_ALPHA = 1.702
SWIGLU_LIMIT = 7.0
LOG2E = 1.4426950408889634

LANES = 128
SUBLANES = 8
ROW_TILE = 256
STEP_SAMPLES = 2
N_PAIRS = N_HEADS // 2
POOL_HALO = 8
EXPERT_ROWS = 512
RANK_TILE = 512
ZERO_RUN = 64
KEY_UNROLL = 16
VMEM_LIMIT = 48 * 1024 * 1024
ATTN_VMEM_LIMIT = 40 * 1024 * 1024
EXPERT_VMEM_LIMIT = 58 * 1024 * 1024
NEG_BIG = -1e30
BOUND_SLACK = 1.0 + 2.0 ** -6
UNDERFLOW_GUARD = 2.0 ** -100


def _split_bf16(a):
    hi = a.astype(BF16)
    lo = (a - hi.astype(F32)).astype(BF16)
    return hi, lo


def _dot(a, b):
    return jnp.dot(a, b, preferred_element_type=F32)


def _mod_kernel(c_ref, w_ref, b_ref, o_ref):
    cv = c_ref[...]
    s = cv * jax.nn.sigmoid(cv)
    s_hi, s_lo = _split_bf16(s)
    w_hi, w_lo = _split_bf16(w_ref[0])
    o_ref[0] = _dot(s_hi, w_hi) + _dot(s_lo, w_hi) + _dot(s_hi, w_lo) + b_ref[0]


def _mod_call(c_all, w_mod, b_mod):
    depth, d, n = w_mod.shape
    r = c_all.shape[0]
    tn = 1536
    return pl.pallas_call(
        _mod_kernel,
        grid=(depth, n // tn),
        in_specs=[
            pl.BlockSpec((r, d), lambda l, j: (0, 0)),
            pl.BlockSpec((1, d, tn), lambda l, j: (l, 0, j)),
            pl.BlockSpec((1, 1, tn), lambda l, j: (l, 0, j)),
        ],
        out_specs=pl.BlockSpec((1, r, tn), lambda l, j: (l, 0, j)),
        out_shape=jax.ShapeDtypeStruct((depth, r, n), F32),
        compiler_params=pltpu.CompilerParams(vmem_limit_bytes=VMEM_LIMIT),
        name="mod",
    )(c_all, w_mod, b_mod.reshape(depth, 1, n))


def _rms_mod(x, g, shift, scale):
    y = x * lax.rsqrt(jnp.mean(x * x, axis=-1, keepdims=True) + EPS) * g
    return y * (1.0 + scale) + shift


def _stream_tile(c_ref, x_ref, j, s=0):
    return jnp.where(j == 0, c_ref[s], x_ref[s])


def _stream_specs(d, first_tile, x_is_latent, samples=1):
    shift = first_tile - (1 if x_is_latent else 0)
    return [pl.BlockSpec((samples, ROW_TILE, d), lambda i, j: (i, 0, 0)),
            pl.BlockSpec((samples, ROW_TILE, d), lambda i, j: (i, jnp.maximum(j + shift, 0), 0))]


def _in_kernel(c_ref, x_ref, mod_ref, g1_ref, wm_ref, wqk_ref, gain_ref, cos_ref, sin_ref,
               q_ref, kt_ref, v_ref, cp_ref):
    for s in range(x_ref.shape[0]):
        _in_one_sample(s, c_ref, x_ref, mod_ref, g1_ref, wm_ref, wqk_ref, gain_ref, cos_ref, sin_ref,
                       q_ref, kt_ref, v_ref, cp_ref)


def _in_one_sample(s, c_ref, x_ref, mod_ref, g1_ref, wm_ref, wqk_ref, gain_ref, cos_ref, sin_ref,
                   q_ref, kt_ref, v_ref, cp_ref):
    tm = x_ref.shape[1]
    xin = _stream_tile(c_ref, x_ref, pl.program_id(1), s)
    h = _rms_mod(xin, g1_ref[...], mod_ref[s, 0:1, :], mod_ref[s, 1:2, :])
    pm = _dot(h.astype(BF16), wm_ref[...])
    for p in range(N_PAIRS):
        v_ref[s, p] = pm[:, p * LANES:(p + 1) * LANES].astype(BF16)
    o = ATTN_W
    cp_ref[s, :, 0:CONV_W] = pm[:, o:o + CONV_W]
    cp_ref[s, :, CONV_W:2 * CONV_W] = pm[:, o + CONV_W:o + 2 * CONV_W] * pm[:, o + 2 * CONV_W:o + 3 * CONV_W]
    cp_ref[s, :, 2 * CONV_W:] = pm[:, o + 3 * CONV_W:]

    ht = h.T.astype(BF16)
    qkt = _dot(wqk_ref[...], ht)
    ng = 2 * ATTN_W // QK_DIM
    t = qkt.reshape(ng, 4, ROPE_PAIRS, tm)
    ss = jnp.sum(jnp.sum(t * t, axis=2, keepdims=True), axis=1, keepdims=True)
    tn = t * lax.rsqrt(ss * (1.0 / QK_DIM) + EPS) * gain_ref[...].reshape(ng, 4, ROPE_PAIRS, tm)
    cs = cos_ref[...].reshape(2, ROPE_PAIRS, tm)
    sn = sin_ref[...].reshape(2, ROPE_PAIRS, tm)
    parts = []
    for a in range(2):
        t1 = tn[:, 2 * a]
        t2 = tn[:, 2 * a + 1]
        parts.append(t1 * cs[a] - t2 * sn[a])
        parts.append(t2 * cs[a] + t1 * sn[a])
    rot = jnp.stack(parts, axis=1).reshape(2 * ATTN_W, tm)
    qt = rot[:ATTN_W].T
    for p in range(N_PAIRS):
        q_ref[s, p] = qt[:, p * LANES:(p + 1) * LANES].astype(BF16)
        kt_ref[s, 0, p] = rot[ATTN_W + p * LANES:ATTN_W + (p + 1) * LANES].astype(BF16)


def _in_call(s_ctx, s_x, la, mod, g1, wm, wqk, gain, cos_t, sin_t):
    b, _, d = s_x.shape
    nt = la // ROW_TILE
    nmain = wm.shape[1]
    sps = STEP_SAMPLES
    assert b % sps == 0

    def mod_idx(i, j):
        return (jnp.where(j == 0, b // sps, i), 0, 0)

    return pl.pallas_call(
        _in_kernel,
        grid=(b // sps, nt),
        in_specs=_stream_specs(d, 0, s_x.shape[1] < la, sps) + [
            pl.BlockSpec((sps, N_MOD, d), mod_idx),
            pl.BlockSpec((1, d), lambda i, j: (0, 0)),
            pl.BlockSpec((d, nmain), lambda i, j: (0, 0)),
            pl.BlockSpec((2 * ATTN_W, d), lambda i, j: (0, 0)),
            pl.BlockSpec((2 * ATTN_W, ROW_TILE), lambda i, j: (0, 0)),
            pl.BlockSpec((2 * ROPE_PAIRS, ROW_TILE), lambda i, j: (0, j)),
            pl.BlockSpec((2 * ROPE_PAIRS, ROW_TILE), lambda i, j: (0, j)),
        ],
        out_specs=[
            pl.BlockSpec((sps, N_PAIRS, ROW_TILE, LANES), lambda i, j: (i, 0, j, 0)),
            pl.BlockSpec((sps, 1, N_PAIRS, LANES, ROW_TILE), lambda i, j: (i, j, 0, 0, 0)),
            pl.BlockSpec((sps, N_PAIRS, ROW_TILE, LANES), lambda i, j: (i, 0, j, 0)),
            pl.BlockSpec((sps, ROW_TILE, 3 * CONV_W), lambda i, j: (i, j, 0)),
        ],
        out_shape=[
            jax.ShapeDtypeStruct((b, N_PAIRS, la, LANES), BF16),
            jax.ShapeDtypeStruct((b, nt, N_PAIRS, LANES, ROW_TILE), BF16),
            jax.ShapeDtypeStruct((b, N_PAIRS, la, LANES), BF16),
            jax.ShapeDtypeStruct((b, la, 3 * CONV_W), F32),
        ],
        compiler_params=pltpu.CompilerParams(vmem_limit_bytes=VMEM_LIMIT),
        name="inproj",
    )(s_ctx, s_x, mod, g1, wm, wqk, gain, cos_t, sin_t)


def _attn_kernel(lam_ref, mb_ref, q_ref, kt_ref, v_ref, sg_ref, o_ref, mb_scr, l_scr, acc_scr, *,
                 lam_init, first_tile):
    tq = q_ref.shape[2]
    nkc_all = kt_ref.shape[1]
    j = pl.program_id(1) + first_tile
    n_groups = jnp.where(j == 0, 0, (nkc_all - 1) // KEY_UNROLL)
    n_chunks = jnp.where(j == 0, 1, nkc_all)
    lp = lam_ref[...]
    lam = (jnp.exp(jnp.sum(lp[0:1] * lp[1:2], axis=-1, keepdims=True))
           - jnp.exp(jnp.sum(lp[2:3] * lp[3:4], axis=-1, keepdims=True)) + lam_init)
    lane = lax.broadcasted_iota(I32, (1, LANES), 1)
    low = lane < HEAD_DIM
    groups = [(lane >= g * QK_DIM) & (lane < (g + 1) * QK_DIM) for g in range(4)]
    bound = mb_ref[...]

    for p in range(N_PAIRS):
        qp = q_ref[0, p]
        qs = jnp.concatenate([jnp.where(groups[g], qp, jnp.zeros_like(qp)) for g in range(4)], axis=0)

        def exp_pv(kc, mb, qs=qs, p=p):
            sc = _dot(qs, kt_ref[0, kc, p])
            e0 = jnp.exp2(sc[:, :LANES] - mb)
            e1 = jnp.exp2(sc[:, LANES:] - mb)
            r0 = pl.multiple_of(kc * ROW_TILE, ROW_TILE)
            e = jnp.concatenate([e0.astype(BF16), e1.astype(BF16)], axis=1)
            return e0 + e1, _dot(e, v_ref[0, p, pl.ds(r0, ROW_TILE), :])

        def sweep(g, _):
            parts = [exp_pv(1 + g * KEY_UNROLL + u, bound) for u in range(KEY_UNROLL)]
            while len(parts) > 1:
                parts = [(a[0] + b[0], a[1] + b[1]) for a, b in zip(parts[::2], parts[1::2])]
            l_scr[...] = l_scr[...] + parts[0][0]
            acc_scr[...] = acc_scr[...] + parts[0][1]
            return 0

        l_scr[...], acc_scr[...] = exp_pv(0, bound)
        lax.fori_loop(0, n_groups, sweep, 0)
        lmin = jnp.min(jnp.sum(l_scr[...], axis=-1, keepdims=True))

        @pl.when(jnp.logical_not(lmin >= UNDERFLOW_GUARD))
        def _():
            def row_max(kc, m, qs=qs, p=p):
                sc = _dot(qs, kt_ref[0, kc, p])
                return jnp.maximum(m, jnp.maximum(sc[:, :LANES], sc[:, LANES:]))
            m = lax.fori_loop(0, n_chunks, row_max, jnp.full((4 * tq, LANES), -jnp.inf, F32))
            mb_scr[...] = jnp.broadcast_to(jnp.max(m, axis=-1, keepdims=True), mb_scr.shape)
            l_scr[...] = jnp.zeros(l_scr.shape, F32)
            acc_scr[...] = jnp.zeros(acc_scr.shape, F32)

            def redo(kc, _):
                dl, dacc = exp_pv(kc, mb_scr[...])
                l_scr[...] = l_scr[...] + dl
                acc_scr[...] = acc_scr[...] + dacc
                return 0
            lax.fori_loop(0, n_chunks, redo, 0)

        on = acc_scr[...] / jnp.sum(l_scr[...], axis=-1, keepdims=True)
        heads = [on[(2 * hh) * tq:(2 * hh + 1) * tq] - lam * on[(2 * hh + 1) * tq:(2 * hh + 2) * tq]
                 for hh in range(2)]
        o = jnp.where(low, heads[0], heads[1])
        o2 = o * o
        ss_lo = jnp.sum(jnp.where(low, o2, 0.0), axis=-1, keepdims=True)
        ss_hi = jnp.sum(jnp.where(low, 0.0, o2), axis=-1, keepdims=True)
        inv = lax.rsqrt(jnp.where(low, ss_lo, ss_hi) * (1.0 / HEAD_DIM) + EPS)
        o_ref[0, p] = (o * inv * sg_ref[...] * (1.0 - lam_init)).astype(BF16)


def _attn_call(lam_p, score_bound, q, kt, v, sg, lam_init, first_tile):
    b, _, la, _ = q.shape
    nt = la // ROW_TILE
    assert (nt - 1) % KEY_UNROLL == 0
    off = first_tile
    return pl.pallas_call(
        functools.partial(_attn_kernel, lam_init=lam_init, first_tile=off),
        grid=(b, nt - off),
        in_specs=[
            pl.BlockSpec((SUBLANES, LANES), lambda i, j: (0, 0)),
            pl.BlockSpec((1, LANES), lambda i, j: (0, 0)),
            pl.BlockSpec((1, N_PAIRS, ROW_TILE, LANES), lambda i, j: (i, 0, j + off, 0)),
            pl.BlockSpec((1, nt, N_PAIRS, LANES, ROW_TILE), lambda i, j: (i, 0, 0, 0, 0)),
            pl.BlockSpec((1, N_PAIRS, la, LANES), lambda i, j: (i, 0, 0, 0)),
            pl.BlockSpec((1, LANES), lambda i, j: (0, 0)),
        ],
        out_specs=pl.BlockSpec((1, N_PAIRS, ROW_TILE, LANES), lambda i, j: (i, 0, j, 0)),
        out_shape=jax.ShapeDtypeStruct((b, N_PAIRS, la - off * ROW_TILE, LANES), BF16),
        scratch_shapes=[pltpu.VMEM((4 * ROW_TILE, LANES), F32),
                        pltpu.VMEM((4 * ROW_TILE, LANES), F32),
                        pltpu.VMEM((4 * ROW_TILE, LANES), F32)],
        compiler_params=pltpu.CompilerParams(vmem_limit_bytes=ATTN_VMEM_LIMIT),
        name="attn",
    )(lam_p, score_bound, q, kt, v, sg)


def _store_row_tiles(ref, val):
    rows = val.shape[0]
    for c in range(SUBLANES):
        ref[pl.ds(c, rows, stride=SUBLANES), :] = val[:, c * LANES:(c + 1) * LANES]


def _load_row_tiles(ref, first_row, rows, c):
    return ref[pl.ds(first_row * SUBLANES + c, rows, stride=SUBLANES), :]


def _shift_rows(a, k):
    return pltpu.roll(a, k % a.shape[0], axis=0)


def _out_kernel(c_ref, x_ref, oa_ref, cp_ref, cpp_ref, cpn_ref, cw_ref, pw_ref, ps_ref, wo_ref,
                mod_ref, g2_ref, wrh_ref, wrl_ref, br_ref,
                xo_ref, hu_ref, te_ref, tw_ref, *, seq_len, first_tile):
    tm = x_ref.shape[1]
    j = pl.program_id(1) + first_tile
    nt = pl.num_programs(1) + first_tile
    halo = POOL_HALO
    has_prev = j >= 2
    has_next = (j >= 1) & (j < nt - 1)
    prev = jnp.where(has_prev, cpp_ref[0], 0.0)
    nxt = jnp.where(has_next, cpn_ref[0], 0.0)
    ext = jnp.concatenate([prev, cp_ref[0], nxt], axis=0)

    z = ext[:, CONV_W:2 * CONV_W]
    conv = (cw_ref[0:1, :] * _shift_rows(z, 1) + cw_ref[1:2, :] * z + cw_ref[2:3, :] * _shift_rows(z, -1))
    o_conv = ext[halo:halo + tm, 0:CONV_W] * conv[halo:halo + tm]

    u = ext[:, 2 * CONV_W:]
    a2 = u + _shift_rows(u, 1)
    a4 = _shift_rows(a2, -1) + _shift_rows(a2, 1)
    a8 = _shift_rows(a4, -2) + _shift_rows(a4, 2)
    a16 = _shift_rows(a8, -4) + _shift_rows(a8, 4)
    lane = lax.broadcasted_iota(I32, (tm, POOL_W), 1)
    grp = lane // POOL_GC
    wsum = jnp.where(grp == 0, a2[halo:halo + tm],
                     jnp.where(grp == 1, a4[halo:halo + tm],
                               jnp.where(grp == 2, a8[halo:halo + tm], a16[halo:halo + tm])))
    half = jnp.where(grp == 0, 1, jnp.where(grp == 1, 2, jnp.where(grp == 2, 4, 8)))
    row = lax.broadcasted_iota(I32, (tm, POOL_W), 0)
    pos = jnp.where(j == 0, row, (j - 1) * tm + row)
    n_seq = jnp.where(j == 0, CTX_LEN, seq_len)
    cnt = jnp.minimum(pos + half, n_seq) - jnp.maximum(pos - half, 0)
    dlt = wsum / cnt.astype(F32) - u[halo:halo + tm]
    o_pool = _dot(dlt.astype(BF16), pw_ref[...]) * ps_ref[...]

    mix = _dot(o_conv.astype(BF16), wo_ref[ATTN_W:ATTN_W + CONV_W, :])
    mix = mix + _dot(o_pool.astype(BF16), wo_ref[ATTN_W + CONV_W:, :])
    for p in range(N_PAIRS):
        mix = mix + _dot(oa_ref[0, p], wo_ref[p * LANES:(p + 1) * LANES, :])
    x = _stream_tile(c_ref, x_ref, j) + mod_ref[0, 2:3, :] * mix
    xo_ref[0] = x

    h2 = _rms_mod(x, g2_ref[...], mod_ref[0, 3:4, :], mod_ref[0, 4:5, :])
    _store_row_tiles(hu_ref.at[0], h2)

    h_hi, h_lo = _split_bf16(h2)
    logits = (_dot(h_hi, wrh_ref[...]) + _dot(h_lo, wrh_ref[...]) + _dot(h_hi, wrl_ref[...])
              + br_ref[...])
    lanef = lax.broadcasted_iota(I32, (tm, LANES), 1).astype(F32)
    work = logits
    tops = []
    for k in range(TOP_K):
        mk = jnp.max(work, axis=-1, keepdims=True)
        ik = jnp.min(jnp.where(work == mk, lanef, float(LANES)), axis=-1, keepdims=True)
        te_ref[0, :, k:k + 1] = ik.astype(I32)
        work = jnp.where(lanef == ik, -jnp.inf, work)
        tops.append(mk)
    es = [jnp.exp(mk - tops[0]) for mk in tops]
    den = es[0] + es[1] + es[2] + es[3]
    for k in range(TOP_K):
        tw_ref[0, :, k:k + 1] = es[k] / den


def _out_call(s_ctx, s_x, oa, cp, cw, pw, ps, wo, mod, g2, wrh, wrl, br, seq_len, first_tile):
    b, _, d = s_x.shape
    la = CTX_LEN + seq_len
    nt = la // ROW_TILE
    off = first_tile
    lo = la - off * ROW_TILE
    hb = ROW_TILE // POOL_HALO
    nhb = la // POOL_HALO

    def mod_idx(i, j):
        return (jnp.where(j + off == 0, b, i), 0, 0)

    const2 = lambda i, j: (0, 0)
    return pl.pallas_call(
        functools.partial(_out_kernel, seq_len=seq_len, first_tile=off),
        grid=(b, nt - off),
        in_specs=_stream_specs(d, off, s_x.shape[1] < la) + [
            pl.BlockSpec((1, N_PAIRS, ROW_TILE, LANES), lambda i, j: (i, 0, j, 0)),
            pl.BlockSpec((1, ROW_TILE, 3 * CONV_W), lambda i, j: (i, j + off, 0)),
            pl.BlockSpec((1, POOL_HALO, 3 * CONV_W), lambda i, j: (i, jnp.maximum((j + off) * hb - 1, 0), 0)),
            pl.BlockSpec((1, POOL_HALO, 3 * CONV_W),
                         lambda i, j: (i, jnp.minimum((j + off + 1) * hb, nhb - 1), 0)),
            pl.BlockSpec((3, CONV_W), const2),
            pl.BlockSpec((POOL_W, POOL_W), const2),
            pl.BlockSpec((1, POOL_W), const2),
            pl.BlockSpec((d, d), const2),
            pl.BlockSpec((1, N_MOD, d), mod_idx),
            pl.BlockSpec((1, d), const2),
            pl.BlockSpec((d, LANES), const2),
            pl.BlockSpec((d, LANES), const2),
            pl.BlockSpec((1, LANES), const2),
        ],
        out_specs=[
            pl.BlockSpec((1, ROW_TILE, d), lambda i, j: (i, j, 0)),
            pl.BlockSpec((1, ROW_TILE * SUBLANES, LANES), lambda i, j: (i, j, 0)),
            pl.BlockSpec((1, ROW_TILE, TOP_K), lambda i, j: (i, j, 0)),
            pl.BlockSpec((1, ROW_TILE, TOP_K), lambda i, j: (i, j, 0)),
        ],
        out_shape=[
            jax.ShapeDtypeStruct((b, lo, d), F32),
            jax.ShapeDtypeStruct((b, lo * SUBLANES, LANES), F32),
            jax.ShapeDtypeStruct((b, lo, TOP_K), I32),
            jax.ShapeDtypeStruct((b, lo, TOP_K), F32),
        ],
        compiler_params=pltpu.CompilerParams(vmem_limit_bytes=VMEM_LIMIT),
        name="outproj",
    )(s_ctx, s_x, oa, cp, cp, cp, cw, pw, ps, wo, mod, g2, wrh, wrl, br)


def _rank_kernel(te_ref, tri_ref, dest_ref, cnt_ref, carry):
    ph = pl.program_id(0)
    i = pl.program_id(1)
    tr = te_ref.shape[0]
    lane1 = lax.broadcasted_iota(I32, (SUBLANES, LANES), 1)

    @pl.when((ph == 0) & (i == 0))
    def _():
        carry[...] = jnp.zeros_like(carry)

    @pl.when((ph == 1) & (i == 0))
    def _():
        cnt = carry[...]
        cnt_ref[...] = cnt.astype(I32)
        padded = jnp.ceil(cnt * (1.0 / EXPERT_ROWS)) * EXPERT_ROWS
        incl = padded
        for sh in (1, 2, 4, 8, 16):
            incl = incl + jnp.where(lane1 >= sh, pltpu.roll(incl, sh, axis=1), 0.0)
        carry[...] = incl - padded

    e = te_ref[...]
    lane = lax.broadcasted_iota(I32, (tr, LANES), 1)
    ohs = [(e[:, k:k + 1] == lane).astype(F32) for k in range(TOP_K)]
    m = ohs[0] + ohs[1] + ohs[2] + ohs[3]

    @pl.when(ph == 0)
    def _():
        dest_ref[...] = jnp.zeros(dest_ref.shape, I32)

    @pl.when(ph == 1)
    def _():
        base = _dot(tri_ref[...], m.astype(BF16)) + carry[0:1, :]
        for k in range(TOP_K):
            dest_ref[:, k:k + 1] = jnp.sum(ohs[k] * base, axis=-1, keepdims=True).astype(I32)

    carry[...] = carry[...] + jnp.sum(m, axis=0, keepdims=True)


def _rank_call(te):
    t = te.shape[0]
    nt = t // RANK_TILE
    r = lax.broadcasted_iota(I32, (RANK_TILE, RANK_TILE), 0)
    c = lax.broadcasted_iota(I32, (RANK_TILE, RANK_TILE), 1)
    tri = (c < r).astype(BF16)
    return pl.pallas_call(
        _rank_kernel,
        grid=(2, nt),
        in_specs=[
            pl.BlockSpec((RANK_TILE, TOP_K), lambda ph, i: (i, 0)),
            pl.BlockSpec((RANK_TILE, RANK_TILE), lambda ph, i: (0, 0)),
        ],
        out_specs=[
            pl.BlockSpec((RANK_TILE, TOP_K), lambda ph, i: (ph * i, 0)),
            pl.BlockSpec((SUBLANES, LANES), lambda ph, i: (0, 0)),
        ],
        out_shape=[
            jax.ShapeDtypeStruct((t, TOP_K), I32),
            jax.ShapeDtypeStruct((SUBLANES, LANES), I32),
        ],
        scratch_shapes=[pltpu.VMEM((SUBLANES, LANES), F32)],
        compiler_params=pltpu.CompilerParams(dimension_semantics=("arbitrary", "arbitrary")),
        name="rank",
    )(te, tri)


def _disp_kernel(pad0_ref, npad_ref, dest_ref, hu_ref, xs_ref, zero, sem, zsem):
    def row_copy(r, k):
        return pltpu.make_async_copy(hu_ref.at[r], xs_ref.at[dest_ref[r * TOP_K + k]], sem)

    def start(r, _):
        for k in range(TOP_K):
            row_copy(r, k).start(priority=k % 2)
        return 0

    lax.fori_loop(0, ROW_TILE, start, 0)
    for _ in range(TOP_K):
        pltpu.make_async_copy(hu_ref, hu_ref, sem).wait()

    @pl.when(pl.program_id(0) == pl.num_programs(0) - 1)
    def _():
        zero[...] = jnp.zeros(zero.shape, F32)
        run = zero.shape[0]

        def zero_range(e, _):
            n_runs = npad_ref[e] // run
            n_rows = npad_ref[e] - n_runs * run
            row0 = pad0_ref[e] + n_runs * run

            def run_copy(c):
                return pltpu.make_async_copy(zero, xs_ref.at[pl.ds(pad0_ref[e] + c * run, run)], zsem)

            def row_copy0(r):
                return pltpu.make_async_copy(zero.at[0], xs_ref.at[row0 + r], zsem)

            def loop(n, make, wait):
                def body(c, carry):
                    if wait:
                        make(c).wait()
                    else:
                        make(c).start()
                    return carry
                lax.fori_loop(0, n, body, 0)

            loop(n_runs, run_copy, False)
            loop(n_rows, row_copy0, False)
            loop(n_runs, run_copy, True)
            loop(n_rows, row_copy0, True)
            return 0

        lax.fori_loop(0, pad0_ref.shape[0], zero_range, 0)


def _disp_call(pad0, npad, dest_flat, hu, n_rows):
    t = hu.shape[0]
    grid_spec = pltpu.PrefetchScalarGridSpec(
        num_scalar_prefetch=2,
        grid=(t // ROW_TILE,),
        in_specs=[
            pl.BlockSpec((ROW_TILE * TOP_K,), lambda i, p0, n: (i,), memory_space=pltpu.SMEM),
            pl.BlockSpec((ROW_TILE, SUBLANES, LANES), lambda i, p0, n: (i, 0, 0)),
        ],
        out_specs=pl.BlockSpec(memory_space=pl.ANY),
        scratch_shapes=[pltpu.VMEM((ZERO_RUN, SUBLANES, LANES), F32), pltpu.SemaphoreType.DMA(()),
                        pltpu.SemaphoreType.DMA(())],
    )
    return pl.pallas_call(
        _disp_kernel,
        grid_spec=grid_spec,
        out_shape=jax.ShapeDtypeStruct((n_rows, SUBLANES, LANES), F32),
        compiler_params=pltpu.CompilerParams(dimension_semantics=("arbitrary",)),
        name="dispatch",
    )(pad0, npad, dest_flat, hu)


def _exp_kernel(be_ref, nu_ref, xs_ref, wgu_ref, bgu_ref, wdn_ref, bdn_ref, ys_ref, wgu_b, wdn_b):
    i = pl.program_id(0)
    ff = wdn_ref.shape[1]
    rows = xs_ref.shape[0] // SUBLANES

    @pl.when((i == 0) | (be_ref[i] != be_ref[jnp.maximum(i - 1, 0)]))
    def _():
        wgu_b[...] = wgu_ref[0].astype(BF16)
        wdn_b[...] = wdn_ref[0].astype(BF16)

    @pl.when(i < nu_ref[0])
    def _():
        xb = jnp.concatenate([_load_row_tiles(xs_ref, 0, rows, c).astype(BF16) for c in range(SUBLANES)],
                             axis=1)
        gu = _dot(xb, wgu_b[...]) + bgu_ref[0]
        glu = jnp.minimum(gu[:, :ff], SWIGLU_LIMIT)
        lin = jnp.clip(gu[:, ff:], -SWIGLU_LIMIT, SWIGLU_LIMIT)
        act = glu * jax.nn.sigmoid(SWIGLU_ALPHA * glu) * (lin + 1.0)
        _store_row_tiles(ys_ref, _dot(act.astype(BF16), wdn_b[...]) + bdn_ref[0])

    @pl.when(i >= nu_ref[0])
    def _():
        ys_ref[...] = jnp.zeros_like(ys_ref)


def _exp_call(block_e, n_used, xs, layer, wgu, bgu, wdn, bdn):
    n_rows = xs.shape[0] // SUBLANES
    _, _, d, ff2 = wgu.shape
    ff = ff2 // 2
    nb = n_rows // EXPERT_ROWS
    grid_spec = pltpu.PrefetchScalarGridSpec(
        num_scalar_prefetch=2,
        grid=(nb,),
        in_specs=[
            pl.BlockSpec((EXPERT_ROWS * SUBLANES, LANES), lambda i, be, nu: (jnp.minimum(i, nu[0] - 1), 0)),
            pl.BlockSpec((None, 1, d, ff2), lambda i, be, nu: (layer, be[i], 0, 0)),
            pl.BlockSpec((1, 1, ff2), lambda i, be, nu: (be[i], 0, 0)),
            pl.BlockSpec((None, 1, ff, d), lambda i, be, nu: (layer, be[i], 0, 0)),
            pl.BlockSpec((1, 1, d), lambda i, be, nu: (be[i], 0, 0)),
        ],
        out_specs=pl.BlockSpec((EXPERT_ROWS * SUBLANES, LANES), lambda i, be, nu: (i, 0)),
        scratch_shapes=[pltpu.VMEM((d, ff2), BF16), pltpu.VMEM((ff, d), BF16)],
    )
    return pl.pallas_call(
        _exp_kernel,
        grid_spec=grid_spec,
        out_shape=jax.ShapeDtypeStruct((n_rows * SUBLANES, LANES), F32),
        compiler_params=pltpu.CompilerParams(dimension_semantics=("arbitrary",),
                                             vmem_limit_bytes=EXPERT_VMEM_LIMIT),
        name="experts",
    )(block_e, n_used, xs, wgu, bgu.reshape(-1, 1, ff2), wdn, bdn.reshape(-1, 1, d))


def _comb_kernel(dest_ref, x_ref, mod_ref, tw_ref, ys_ref, o_ref, buf, sem):
    def row_copy(r, k):
        r0 = pl.multiple_of((k * ROW_TILE + r) * SUBLANES, SUBLANES)
        return pltpu.make_async_copy(ys_ref.at[dest_ref[r * TOP_K + k]], buf.at[pl.ds(r0, SUBLANES)], sem)

    def start(r, _):
        for k in range(TOP_K):
            row_copy(r, k).start(priority=k % 2)
        return 0

    lax.fori_loop(0, ROW_TILE, start, 0)
    pltpu.make_async_copy(buf, buf, sem).wait()
    w = [tw_ref[0, :, k:k + 1] for k in range(TOP_K)]
    for c in range(SUBLANES):
        acc = _load_row_tiles(buf, 0, ROW_TILE, c) * w[0]
        for k in range(1, TOP_K):
            acc = acc + _load_row_tiles(buf, k * ROW_TILE, ROW_TILE, c) * w[k]
        cols = slice(c * LANES, (c + 1) * LANES)
        o_ref[0, :, cols] = x_ref[0, :, cols] + mod_ref[0, 5:6, cols] * acc


def _comb_call(dest_flat, s, mod, tw, ys, first_tile):
    b, rows, d = s.shape
    nt = rows // ROW_TILE

    def mod_idx(i, j):
        return (jnp.where(j + first_tile == 0, b, i), 0, 0)

    return pl.pallas_call(
        _comb_kernel,
        grid=(b, nt),
        in_specs=[
            pl.BlockSpec((ROW_TILE * TOP_K,), lambda i, j: (i * nt + j,), memory_space=pltpu.SMEM),
            pl.BlockSpec((1, ROW_TILE, d), lambda i, j: (i, j, 0)),
            pl.BlockSpec((1, N_MOD, d), mod_idx),
            pl.BlockSpec((1, ROW_TILE, TOP_K), lambda i, j: (i, j, 0)),
            pl.BlockSpec(memory_space=pl.ANY),
        ],
        out_specs=pl.BlockSpec((1, ROW_TILE, d), lambda i, j: (i, j, 0)),
        out_shape=jax.ShapeDtypeStruct((b, rows, d), F32),
        scratch_shapes=[pltpu.VMEM((TOP_K * ROW_TILE * SUBLANES, LANES), F32), pltpu.SemaphoreType.DMA(())],
        compiler_params=pltpu.CompilerParams(dimension_semantics=("arbitrary", "arbitrary"),
                                             vmem_limit_bytes=VMEM_LIMIT),
        name="combine",
    )(dest_flat, s, mod, tw, ys)


def _rope_tables(seq_len):
    rows = seq_len // GRID_W
    row = jnp.broadcast_to(jnp.arange(rows, dtype=F32)[:, None], (rows, GRID_W)).reshape(seq_len)
    col = jnp.broadcast_to(jnp.arange(GRID_W, dtype=F32)[None, :], (rows, GRID_W)).reshape(seq_len)
    inv = ROPE_BASE ** (-jnp.arange(ROPE_PAIRS, dtype=F32) / ROPE_PAIRS)
    ang = jnp.stack([row, col], axis=0)[:, None, :] * inv[None, :, None]
    cos_l = jnp.cos(ang).reshape(2 * ROPE_PAIRS, seq_len)
    sin_l = jnp.sin(ang).reshape(2 * ROPE_PAIRS, seq_len)
    cos_t = jnp.concatenate([jnp.ones((2 * ROPE_PAIRS, CTX_LEN), F32), cos_l], axis=1)
    sin_t = jnp.concatenate([jnp.zeros((2 * ROPE_PAIRS, CTX_LEN), F32), sin_l], axis=1)
    return cos_t, sin_t


def kernel(x, c, ctx, c_ctx, w_mod, b_mod, norm1_g, norm2_g, w_in, q_norm_g, k_norm_g,
           lambda_q1, lambda_k1, lambda_q2, lambda_k2, subln_g, conv_w, pool_w, pool_scale, w_out,
           router_w, router_b, w_gate_up, b_gate_up, w_down, b_down):
    b, seq_len, d = x.shape
    assert d == SUBLANES * LANES
    depth = w_mod.shape[0]
    la = CTX_LEN + seq_len

    rows = -(-(b + STEP_SAMPLES) // SUBLANES) * SUBLANES
    c_all = jnp.zeros((rows, d), F32).at[:b].set(c).at[b:b + STEP_SAMPLES].set(c_ctx)
    mod = _mod_call(c_all, w_mod, b_mod).reshape(depth, rows, N_MOD, d)

    cos_t, sin_t = _rope_tables(seq_len)
    s_ctx, s_x = ctx, x

    for l in range(depth):
        first_tile = 1 if l + 1 == depth else 0
        t_all = b * (la - first_tile * ROW_TILE)
        n_rows = -(-(t_all * TOP_K + N_EXPERTS * (EXPERT_ROWS - 1)) // EXPERT_ROWS) * EXPERT_ROWS
        nb = n_rows // EXPERT_ROWS
        lam_init = 0.8 - 0.6 * math.exp(-0.3 * l)
        wl = w_in[l]
        wm = wl[:, 2 * ATTN_W:].astype(BF16)
        wqk = wl[:, :2 * ATTN_W].T.astype(BF16)
        q_gain = jnp.tile(q_norm_g[l], ATTN_W // QK_DIM) * (LOG2E * QK_DIM ** -0.5)
        k_gain = jnp.tile(k_norm_g[l], ATTN_W // QK_DIM)
        gain = jnp.broadcast_to(jnp.concatenate([q_gain, k_gain])[:, None], (2 * ATTN_W, ROW_TILE))
        q, kt, v, cp = _in_call(s_ctx, s_x, la, mod[l], norm1_g[l][None], wm, wqk, gain, cos_t, sin_t)

        lam_p = jnp.zeros((SUBLANES, LANES), F32)
        lam_p = lam_p.at[0, :QK_DIM].set(lambda_q1[l]).at[1, :QK_DIM].set(lambda_k1[l])
        lam_p = lam_p.at[2, :QK_DIM].set(lambda_q2[l]).at[3, :QK_DIM].set(lambda_k2[l])
        sg = jnp.tile(subln_g[l], 2)[None]
        score_bound = jnp.full((1, LANES), QK_DIM * BOUND_SLACK, F32) * (
            jnp.max(jnp.abs(q_gain)) * jnp.max(jnp.abs(k_gain)))
        oa = _attn_call(lam_p, score_bound, q, kt, v, sg, lam_init, first_tile)

        pw = jax.scipy.linalg.block_diag(*[pool_w[l, g] for g in range(len(POOL_WINDOWS))]).astype(BF16)
        wr = jnp.zeros((d, LANES), F32).at[:, :N_EXPERTS].set(router_w[l])
        wrh, wrl = _split_bf16(wr)
        br = jnp.full((1, LANES), NEG_BIG, F32).at[0, :N_EXPERTS].set(router_b[l])
        s, hu, te, tw = _out_call(s_ctx, s_x, oa, cp, conv_w[l], pw, pool_scale[l][None],
                                  w_out[l].astype(BF16), mod[l], norm2_g[l][None], wrh, wrl, br,
                                  seq_len, first_tile)

        dest, counts = _rank_call(te.reshape(t_all, TOP_K))
        cnt = counts[0, :N_EXPERTS]
        padded = (cnt + EXPERT_ROWS - 1) // EXPERT_ROWS * EXPERT_ROWS
        pad_end = jnp.cumsum(padded)
        n_used = (pad_end[-1] // EXPERT_ROWS).astype(I32)
        blk = jnp.minimum(jnp.arange(nb, dtype=I32), n_used - 1) * EXPERT_ROWS
        block_e = jnp.sum((pad_end[None, :] <= blk[:, None]).astype(I32), axis=1)
        block_e = jnp.minimum(block_e, N_EXPERTS - 1)
        dest_flat = dest.reshape(t_all * TOP_K)

        pad0 = jnp.concatenate([pad_end - padded + cnt, pad_end[-1:]])
        npad = jnp.concatenate([padded - cnt, n_rows - pad_end[-1:]])
        xs = _disp_call(pad0, npad, dest_flat, hu.reshape(t_all, SUBLANES, LANES), n_rows)
        ys = _exp_call(block_e, n_used.reshape(1), xs.reshape(n_rows * SUBLANES, LANES), l,
                       w_gate_up, b_gate_up[l], w_down, b_down[l])
        s = _comb_call(dest_flat, s, mod[l], tw, ys.reshape(n_rows, SUBLANES, LANES), first_tile)
        s_ctx = s_x = s
    return s
```

```python
import functools
import math

import jax
import jax.numpy as jnp
from jax import lax
from jax.experimental import pallas as pl
from jax.experimental.pallas import tpu as pltpu

F32 = jnp.float32
BF16 = jnp.bfloat16
I32 = jnp.int32
U32 = jnp.uint32

GRID_W = 64
CTX_LEN = 256
HEAD_DIM = 64
QK_DIM = 32
N_HEADS = 8
ATTN_W = 512
CONV_W = 256
POOL_W = 256
POOL_WINDOWS = (2, 4, 8, 16)
POOL_GC = 64
ROPE_BASE = 10000.0
ROPE_PAIRS = 8
N_EXPERTS = 32
TOP_K = 4
N_MOD = 6
EPS = 1e-6
SWIGLU_ALPHA = 1.702
SWIGLU_LIMIT = 7.0
LOG2E = 1.4426950408889634

LANES = 128
SUBLANES = 8
ROW_TILE = 256
STEP_SAMPLES = 2
N_PAIRS = N_HEADS // 2
POOL_HALO = 8
EXPERT_ROWS = 512
RANK_TILE = 512
ZERO_RUN = 64
KEY_UNROLL = 16
VMEM_LIMIT = 48 * 1024 * 1024
ATTN_VMEM_LIMIT = 40 * 1024 * 1024
EXPERT_VMEM_LIMIT = 58 * 1024 * 1024
NEG_BIG = -1e30
BOUND_SLACK = 1.0 + 2.0 ** -6
UNDERFLOW_GUARD = 2.0 ** -100


def _split_bf16(a):
    hi = a.astype(BF16)
    lo = (a - hi.astype(F32)).astype(BF16)
    return hi, lo


def _dot(a, b):
    return jnp.dot(a, b, preferred_element_type=F32)


def _mod_kernel(c_ref, w_ref, b_ref, o_ref):
    cv = c_ref[...]
    s = cv * jax.nn.sigmoid(cv)
    s_hi, s_lo = _split_bf16(s)
    w_hi, w_lo = _split_bf16(w_ref[0])
    o_ref[0] = _dot(s_hi, w_hi) + _dot(s_lo, w_hi) + _dot(s_hi, w_lo) + b_ref[0]


def _mod_call(c_all, w_mod, b_mod):
    depth, d, n = w_mod.shape
    r = c_all.shape[0]
    tn = 1536
    return pl.pallas_call(
        _mod_kernel,
        grid=(depth, n // tn),
        in_specs=[
            pl.BlockSpec((r, d), lambda l, j: (0, 0)),
            pl.BlockSpec((1, d, tn), lambda l, j: (l, 0, j)),
            pl.BlockSpec((1, 1, tn), lambda l, j: (l, 0, j)),
        ],
        out_specs=pl.BlockSpec((1, r, tn), lambda l, j: (l, 0, j)),
        out_shape=jax.ShapeDtypeStruct((depth, r, n), F32),
        compiler_params=pltpu.CompilerParams(vmem_limit_bytes=VMEM_LIMIT),
        name="mod",
    )(c_all, w_mod, b_mod.reshape(depth, 1, n))


def _rms_mod(x, g, shift, scale):
    y = x * lax.rsqrt(jnp.mean(x * x, axis=-1, keepdims=True) + EPS) * g
    return y * (1.0 + scale) + shift


def _stream_tile(c_ref, x_ref, j, s=0):
    return jnp.where(j == 0, c_ref[s], x_ref[s])


def _stream_specs(d, first_tile, x_is_latent, samples=1):
    shift = first_tile - (1 if x_is_latent else 0)
    return [pl.BlockSpec((samples, ROW_TILE, d), lambda i, j: (i, 0, 0)),
            pl.BlockSpec((samples, ROW_TILE, d), lambda i, j: (i, jnp.maximum(j + shift, 0), 0))]


def _in_kernel(c_ref, x_ref, mod_ref, g1_ref, wm_ref, wqk_ref, gain_ref, cos_ref, sin_ref,
               q_ref, kt_ref, v_ref, cp_ref):
    for s in range(x_ref.shape[0]):
        _in_one_sample(s, c_ref, x_ref, mod_ref, g1_ref, wm_ref, wqk_ref, gain_ref, cos_ref, sin_ref,
                       q_ref, kt_ref, v_ref, cp_ref)


def _in_one_sample(s, c_ref, x_ref, mod_ref, g1_ref, wm_ref, wqk_ref, gain_ref, cos_ref, sin_ref,
                   q_ref, kt_ref, v_ref, cp_ref):
    tm = x_ref.shape[1]
    xin = _stream_tile(c_ref, x_ref, pl.program_id(1), s)
    h = _rms_mod(xin, g1_ref[...], mod_ref[s, 0:1, :], mod_ref[s, 1:2, :])
    pm = _dot(h.astype(BF16), wm_ref[...])
    for p in range(N_PAIRS):
        v_ref[s, p] = pm[:, p * LANES:(p + 1) * LANES].astype(BF16)
    o = ATTN_W
    cp_ref[s, :, 0:CONV_W] = pm[:, o:o + CONV_W]
    cp_ref[s, :, CONV_W:2 * CONV_W] = pm[:, o + CONV_W:o + 2 * CONV_W] * pm[:, o + 2 * CONV_W:o + 3 * CONV_W]
    cp_ref[s, :, 2 * CONV_W:] = pm[:, o + 3 * CONV_W:]

    ht = h.T.astype(BF16)
    qkt = _dot(wqk_ref[...], ht)
    ng = 2 * ATTN_W // QK_DIM
    t = qkt.reshape(ng, 4, ROPE_PAIRS, tm)
    ss = jnp.sum(jnp.sum(t * t, axis=2, keepdims=True), axis=1, keepdims=True)
    tn = t * lax.rsqrt(ss * (1.0 / QK_DIM) + EPS) * gain_ref[...].reshape(ng, 4, ROPE_PAIRS, tm)
    cs = cos_ref[...].reshape(2, ROPE_PAIRS, tm)
    sn = sin_ref[...].reshape(2, ROPE_PAIRS, tm)
    parts = []
    for a in range(2):
        t1 = tn[:, 2 * a]
        t2 = tn[:, 2 * a + 1]
        parts.append(t1 * cs[a] - t2 * sn[a])
        parts.append(t2 * cs[a] + t1 * sn[a])
    rot = jnp.stack(parts, axis=1).reshape(2 * ATTN_W, tm)
    qt = rot[:ATTN_W].T
    for p in range(N_PAIRS):
        q_ref[s, p] = qt[:, p * LANES:(p + 1) * LANES].astype(BF16)
        kt_ref[s, 0, p] = rot[ATTN_W + p * LANES:ATTN_W + (p + 1) * LANES].astype(BF16)


def _in_call(s_ctx, s_x, la, mod, g1, wm, wqk, gain, cos_t, sin_t):
    b, _, d = s_x.shape
    nt = la // ROW_TILE
    nmain = wm.shape[1]
    sps = STEP_SAMPLES
    assert b % sps == 0

    def mod_idx(i, j):
        return (jnp.where(j == 0, b // sps, i), 0, 0)

    return pl.pallas_call(
        _in_kernel,
        grid=(b // sps, nt),
        in_specs=_stream_specs(d, 0, s_x.shape[1] < la, sps) + [
            pl.BlockSpec((sps, N_MOD, d), mod_idx),
            pl.BlockSpec((1, d), lambda i, j: (0, 0)),
            pl.BlockSpec((d, nmain), lambda i, j: (0, 0)),
            pl.BlockSpec((2 * ATTN_W, d), lambda i, j: (0, 0)),
            pl.BlockSpec((2 * ATTN_W, ROW_TILE), lambda i, j: (0, 0)),
            pl.BlockSpec((2 * ROPE_PAIRS, ROW_TILE), lambda i, j: (0, j)),
            pl.BlockSpec((2 * ROPE_PAIRS, ROW_TILE), lambda i, j: (0, j)),
        ],
        out_specs=[
            pl.BlockSpec((sps, N_PAIRS, ROW_TILE, LANES), lambda i, j: (i, 0, j, 0)),
            pl.BlockSpec((sps, 1, N_PAIRS, LANES, ROW_TILE), lambda i, j: (i, j, 0, 0, 0)),
            pl.BlockSpec((sps, N_PAIRS, ROW_TILE, LANES), lambda i, j: (i, 0, j, 0)),
            pl.BlockSpec((sps, ROW_TILE, 3 * CONV_W), lambda i, j: (i, j, 0)),
        ],
        out_shape=[
            jax.ShapeDtypeStruct((b, N_PAIRS, la, LANES), BF16),
            jax.ShapeDtypeStruct((b, nt, N_PAIRS, LANES, ROW_TILE), BF16),
            jax.ShapeDtypeStruct((b, N_PAIRS, la, LANES), BF16),
            jax.ShapeDtypeStruct((b, la, 3 * CONV_W), F32),
        ],
        compiler_params=pltpu.CompilerParams(vmem_limit_bytes=VMEM_LIMIT),
        name="inproj",
    )(s_ctx, s_x, mod, g1, wm, wqk, gain, cos_t, sin_t)


def _attn_kernel(lam_ref, mb_ref, q_ref, kt_ref, v_ref, sg_ref, o_ref, mb_scr, l_scr, acc_scr, *,
                 lam_init, first_tile):
    tq = q_ref.shape[2]
    nkc_all = kt_ref.shape[1]
    j = pl.program_id(1) + first_tile
    n_groups = jnp.where(j == 0, 0, (nkc_all - 1) // KEY_UNROLL)
    n_chunks = jnp.where(j == 0, 1, nkc_all)
    lp = lam_ref[...]
    lam = (jnp.exp(jnp.sum(lp[0:1] * lp[1:2], axis=-1, keepdims=True))
           - jnp.exp(jnp.sum(lp[2:3] * lp[3:4], axis=-1, keepdims=True)) + lam_init)
    lane = lax.broadcasted_iota(I32, (1, LANES), 1)
    low = lane < HEAD_DIM
    groups = [(lane >= g * QK_DIM) & (lane < (g + 1) * QK_DIM) for g in range(4)]
    bound = mb_ref[...]

    def stacked_queries(p):
        qp = q_ref[0, p]
        return jnp.concatenate([jnp.where(groups[g], qp, jnp.zeros_like(qp)) for g in range(4)], axis=0)

    def exp_pv(p, qs, kc, mb):
        sc = _dot(qs, kt_ref[0, kc, p])
        e0 = jnp.exp2(sc[:, :LANES] - mb)
        e1 = jnp.exp2(sc[:, LANES:] - mb)
        r0 = pl.multiple_of(kc * ROW_TILE, ROW_TILE)
        e = jnp.concatenate([e0.astype(BF16), e1.astype(BF16)], axis=1)
        return e0 + e1, _dot(e, v_ref[0, p, pl.ds(r0, ROW_TILE), :])

    def finish(p, lsum):
        on = acc_scr[...] / lsum
        heads = [on[(2 * hh) * tq:(2 * hh + 1) * tq] - lam * on[(2 * hh + 1) * tq:(2 * hh + 2) * tq]
                 for hh in range(2)]
        o = jnp.where(low, heads[0], heads[1])
        o2 = o * o
        ss_lo = jnp.sum(jnp.where(low, o2, 0.0), axis=-1, keepdims=True)
        ss_hi = jnp.sum(jnp.where(low, 0.0, o2), axis=-1, keepdims=True)
        inv = lax.rsqrt(jnp.where(low, ss_lo, ss_hi) * (1.0 / HEAD_DIM) + EPS)
        o_ref[0, p] = (o * inv * sg_ref[...] * (1.0 - lam_init)).astype(BF16)

    lmin = None
    for p in range(N_PAIRS):
        qs = stacked_queries(p)

        def sweep(g, _, p=p, qs=qs):
            parts = [exp_pv(p, qs, 1 + g * KEY_UNROLL + u, bound) for u in range(KEY_UNROLL)]
            while len(parts) > 1:
                parts = [(a[0] + b[0], a[1] + b[1]) for a, b in zip(parts[::2], parts[1::2])]
            l_scr[...] = l_scr[...] + parts[0][0]
            acc_scr[...] = acc_scr[...] + parts[0][1]
            return 0

        l_scr[...], acc_scr[...] = exp_pv(p, qs, 0, bound)
        lax.fori_loop(0, n_groups, sweep, 0)
        lsum = jnp.sum(l_scr[...], axis=-1, keepdims=True)
        lmin = lsum if lmin is None else jnp.minimum(lmin, lsum)
        finish(p, lsum)

    @pl.when(jnp.logical_not(jnp.min(lmin) >= UNDERFLOW_GUARD))
    def _():
        for p in range(N_PAIRS):
            qs = stacked_queries(p)

            def row_max(kc, m, p=p, qs=qs):
                sc = _dot(qs, kt_ref[0, kc, p])
                return jnp.maximum(m, jnp.maximum(sc[:, :LANES], sc[:, LANES:]))

            m = lax.fori_loop(0, n_chunks, row_max, jnp.full((4 * tq, LANES), -jnp.inf, F32))
            mb_scr[...] = jnp.broadcast_to(jnp.max(m, axis=-1, keepdims=True), mb_scr.shape)
            l_scr[...] = jnp.zeros(l_scr.shape, F32)
            acc_scr[...] = jnp.zeros(acc_scr.shape, F32)

            def redo(kc, _, p=p, qs=qs):
                dl, dacc = exp_pv(p, qs, kc, mb_scr[...])
                l_scr[...] = l_scr[...] + dl
                acc_scr[...] = acc_scr[...] + dacc
                return 0

            lax.fori_loop(0, n_chunks, redo, 0)
            finish(p, jnp.sum(l_scr[...], axis=-1, keepdims=True))


def _attn_call(lam_p, score_bound, q, kt, v, sg, lam_init, first_tile):
    b, _, la, _ = q.shape
    nt = la // ROW_TILE
    assert (nt - 1) % KEY_UNROLL == 0
    off = first_tile
    return pl.pallas_call(
        functools.partial(_attn_kernel, lam_init=lam_init, first_tile=off),
        grid=(b, nt - off),
        in_specs=[
            pl.BlockSpec((SUBLANES, LANES), lambda i, j: (0, 0)),
            pl.BlockSpec((1, LANES), lambda i, j: (0, 0)),
            pl.BlockSpec((1, N_PAIRS, ROW_TILE, LANES), lambda i, j: (i, 0, j + off, 0)),
            pl.BlockSpec((1, nt, N_PAIRS, LANES, ROW_TILE), lambda i, j: (i, 0, 0, 0, 0)),
            pl.BlockSpec((1, N_PAIRS, la, LANES), lambda i, j: (i, 0, 0, 0)),
            pl.BlockSpec((1, LANES), lambda i, j: (0, 0)),
        ],
        out_specs=pl.BlockSpec((1, N_PAIRS, ROW_TILE, LANES), lambda i, j: (i, 0, j, 0)),
        out_shape=jax.ShapeDtypeStruct((b, N_PAIRS, la - off * ROW_TILE, LANES), BF16),
        scratch_shapes=[pltpu.VMEM((4 * ROW_TILE, LANES), F32),
                        pltpu.VMEM((4 * ROW_TILE, LANES), F32),
                        pltpu.VMEM((4 * ROW_TILE, LANES), F32)],
        compiler_params=pltpu.CompilerParams(vmem_limit_bytes=ATTN_VMEM_LIMIT),
        name="attn",
    )(lam_p, score_bound, q, kt, v, sg)


def _store_row_tiles(ref, val):
    rows = val.shape[0]
    for c in range(SUBLANES):
        ref[pl.ds(c, rows, stride=SUBLANES), :] = val[:, c * LANES:(c + 1) * LANES]


def _load_row_tiles(ref, first_row, rows, c):
    return ref[pl.ds(first_row * SUBLANES + c, rows, stride=SUBLANES), :]


def _shift_rows(a, k):
    return pltpu.roll(a, k % a.shape[0], axis=0)


def _out_kernel(c_ref, x_ref, oa_ref, cp_ref, cpp_ref, cpn_ref, cw_ref, pw_ref, ps_ref, wo_ref,
                mod_ref, g2_ref, wrh_ref, wrl_ref, br_ref,
                xo_ref, hu_ref, te_ref, tw_ref, *, seq_len, first_tile):
    tm = x_ref.shape[1]
    j = pl.program_id(1) + first_tile
    nt = pl.num_programs(1) + first_tile
    halo = POOL_HALO
    has_prev = j >= 2
    has_next = (j >= 1) & (j < nt - 1)
    prev = jnp.where(has_prev, cpp_ref[0], 0.0)
    nxt = jnp.where(has_next, cpn_ref[0], 0.0)
    ext = jnp.concatenate([prev, cp_ref[0], nxt], axis=0)

    z = ext[:, CONV_W:2 * CONV_W]
    conv = (cw_ref[0:1, :] * _shift_rows(z, 1) + cw_ref[1:2, :] * z + cw_ref[2:3, :] * _shift_rows(z, -1))
    o_conv = ext[halo:halo + tm, 0:CONV_W] * conv[halo:halo + tm]

    u = ext[:, 2 * CONV_W:]
    a2 = u + _shift_rows(u, 1)
    a4 = _shift_rows(a2, -1) + _shift_rows(a2, 1)
    a8 = _shift_rows(a4, -2) + _shift_rows(a4, 2)
    a16 = _shift_rows(a8, -4) + _shift_rows(a8, 4)
    lane = lax.broadcasted_iota(I32, (tm, POOL_W), 1)
    grp = lane // POOL_GC
    wsum = jnp.where(grp == 0, a2[halo:halo + tm],
                     jnp.where(grp == 1, a4[halo:halo + tm],
                               jnp.where(grp == 2, a8[halo:halo + tm], a16[halo:halo + tm])))
    half = jnp.where(grp == 0, 1, jnp.where(grp == 1, 2, jnp.where(grp == 2, 4, 8)))
    row = lax.broadcasted_iota(I32, (tm, POOL_W), 0)
    pos = jnp.where(j == 0, row, (j - 1) * tm + row)
    n_seq = jnp.where(j == 0, CTX_LEN, seq_len)
    cnt = jnp.minimum(pos + half, n_seq) - jnp.maximum(pos - half, 0)
    dlt = wsum / cnt.astype(F32) - u[halo:halo + tm]
    o_pool = _dot(dlt.astype(BF16), pw_ref[...]) * ps_ref[...]

    mix = _dot(o_conv.astype(BF16), wo_ref[ATTN_W:ATTN_W + CONV_W, :])
    mix = mix + _dot(o_pool.astype(BF16), wo_ref[ATTN_W + CONV_W:, :])
    for p in range(N_PAIRS):
        mix = mix + _dot(oa_ref[0, p], wo_ref[p * LANES:(p + 1) * LANES, :])
    x = _stream_tile(c_ref, x_ref, j) + mod_ref[0, 2:3, :] * mix
    xo_ref[0] = x

    h2 = _rms_mod(x, g2_ref[...], mod_ref[0, 3:4, :], mod_ref[0, 4:5, :])
    _store_row_tiles(hu_ref.at[0], h2)

    h_hi, h_lo = _split_bf16(h2)
    logits = (_dot(h_hi, wrh_ref[...]) + _dot(h_lo, wrh_ref[...]) + _dot(h_hi, wrl_ref[...])
              + br_ref[...])
    lanef = lax.broadcasted_iota(I32, (tm, LANES), 1).astype(F32)
    work = logits
    tops = []
    for k in range(TOP_K):
        mk = jnp.max(work, axis=-1, keepdims=True)
        ik = jnp.min(jnp.where(work == mk, lanef, float(LANES)), axis=-1, keepdims=True)
        te_ref[0, :, k:k + 1] = ik.astype(I32)
        work = jnp.where(lanef == ik, -jnp.inf, work)
        tops.append(mk)
    es = [jnp.exp(mk - tops[0]) for mk in tops]
    den = es[0] + es[1] + es[2] + es[3]
    for k in range(TOP_K):
        tw_ref[0, :, k:k + 1] = es[k] / den


def _out_call(s_ctx, s_x, oa, cp, cw, pw, ps, wo, mod, g2, wrh, wrl, br, seq_len, first_tile):
    b, _, d = s_x.shape
    la = CTX_LEN + seq_len
    nt = la // ROW_TILE
    off = first_tile
    lo = la - off * ROW_TILE
    hb = ROW_TILE // POOL_HALO
    nhb = la // POOL_HALO

    def mod_idx(i, j):
        return (jnp.where(j + off == 0, b, i), 0, 0)

    const2 = lambda i, j: (0, 0)
    return pl.pallas_call(
        functools.partial(_out_kernel, seq_len=seq_len, first_tile=off),
        grid=(b, nt - off),
        in_specs=_stream_specs(d, off, s_x.shape[1] < la) + [
            pl.BlockSpec((1, N_PAIRS, ROW_TILE, LANES), lambda i, j: (i, 0, j, 0)),
            pl.BlockSpec((1, ROW_TILE, 3 * CONV_W), lambda i, j: (i, j + off, 0)),
            pl.BlockSpec((1, POOL_HALO, 3 * CONV_W), lambda i, j: (i, jnp.maximum((j + off) * hb - 1, 0), 0)),
            pl.BlockSpec((1, POOL_HALO, 3 * CONV_W),
                         lambda i, j: (i, jnp.minimum((j + off + 1) * hb, nhb - 1), 0)),
            pl.BlockSpec((3, CONV_W), const2),
            pl.BlockSpec((POOL_W, POOL_W), const2),
            pl.BlockSpec((1, POOL_W), const2),
            pl.BlockSpec((d, d), const2),
            pl.BlockSpec((1, N_MOD, d), mod_idx),
            pl.BlockSpec((1, d), const2),
            pl.BlockSpec((d, LANES), const2),
            pl.BlockSpec((d, LANES), const2),
            pl.BlockSpec((1, LANES), const2),
        ],
        out_specs=[
            pl.BlockSpec((1, ROW_TILE, d), lambda i, j: (i, j, 0)),
            pl.BlockSpec((1, ROW_TILE * SUBLANES, LANES), lambda i, j: (i, j, 0)),
            pl.BlockSpec((1, ROW_TILE, TOP_K), lambda i, j: (i, j, 0)),
            pl.BlockSpec((1, ROW_TILE, TOP_K), lambda i, j: (i, j, 0)),
        ],
        out_shape=[
            jax.ShapeDtypeStruct((b, lo, d), F32),
            jax.ShapeDtypeStruct((b, lo * SUBLANES, LANES), F32),
            jax.ShapeDtypeStruct((b, lo, TOP_K), I32),
            jax.ShapeDtypeStruct((b, lo, TOP_K), F32),
        ],
        compiler_params=pltpu.CompilerParams(vmem_limit_bytes=VMEM_LIMIT),
        name="outproj",
    )(s_ctx, s_x, oa, cp, cp, cp, cw, pw, ps, wo, mod, g2, wrh, wrl, br)


def _rank_kernel(te_ref, tri_ref, dest_ref, cnt_ref, carry):
    ph = pl.program_id(0)
    i = pl.program_id(1)
    tr = te_ref.shape[0]
    lane1 = lax.broadcasted_iota(I32, (SUBLANES, LANES), 1)

    @pl.when((ph == 0) & (i == 0))
    def _():
        carry[...] = jnp.zeros_like(carry)

    @pl.when((ph == 1) & (i == 0))
    def _():
        cnt = carry[...]
        cnt_ref[...] = cnt.astype(I32)
        padded = jnp.ceil(cnt * (1.0 / EXPERT_ROWS)) * EXPERT_ROWS
        incl = padded
        for sh in (1, 2, 4, 8, 16):
            incl = incl + jnp.where(lane1 >= sh, pltpu.roll(incl, sh, axis=1), 0.0)
        carry[...] = incl - padded

    e = te_ref[...]
    lane = lax.broadcasted_iota(I32, (tr, LANES), 1)
    ohs = [(e[:, k:k + 1] == lane).astype(F32) for k in range(TOP_K)]
    m = ohs[0] + ohs[1] + ohs[2] + ohs[3]

    @pl.when(ph == 0)
    def _():
        dest_ref[...] = jnp.zeros(dest_ref.shape, I32)

    @pl.when(ph == 1)
    def _():
        base = _dot(tri_ref[...], m.astype(BF16)) + carry[0:1, :]
        for k in range(TOP_K):
            dest_ref[:, k:k + 1] = jnp.sum(ohs[k] * base, axis=-1, keepdims=True).astype(I32)

    carry[...] = carry[...] + jnp.sum(m, axis=0, keepdims=True)


def _rank_call(te):
    t = te.shape[0]
    nt = t // RANK_TILE
    r = lax.broadcasted_iota(I32, (RANK_TILE, RANK_TILE), 0)
    c = lax.broadcasted_iota(I32, (RANK_TILE, RANK_TILE), 1)
    tri = (c < r).astype(BF16)
    return pl.pallas_call(
        _rank_kernel,
        grid=(2, nt),
        in_specs=[
            pl.BlockSpec((RANK_TILE, TOP_K), lambda ph, i: (i, 0)),
            pl.BlockSpec((RANK_TILE, RANK_TILE), lambda ph, i: (0, 0)),
        ],
        out_specs=[
            pl.BlockSpec((RANK_TILE, TOP_K), lambda ph, i: (ph * i, 0)),
            pl.BlockSpec((SUBLANES, LANES), lambda ph, i: (0, 0)),
        ],
        out_shape=[
            jax.ShapeDtypeStruct((t, TOP_K), I32),
            jax.ShapeDtypeStruct((SUBLANES, LANES), I32),
        ],
        scratch_shapes=[pltpu.VMEM((SUBLANES, LANES), F32)],
        compiler_params=pltpu.CompilerParams(dimension_semantics=("arbitrary", "arbitrary")),
        name="rank",
    )(te, tri)


def _disp_kernel(pad0_ref, npad_ref, dest_ref, hu_ref, xs_ref, zero, sem, zsem):
    def row_copy(r, k):
        return pltpu.make_async_copy(hu_ref.at[r], xs_ref.at[dest_ref[r * TOP_K + k]], sem)

    def start(r, _):
        for k in range(TOP_K):
            row_copy(r, k).start(priority=k % 2)
        return 0

    lax.fori_loop(0, ROW_TILE, start, 0)
    for _ in range(TOP_K):
        pltpu.make_async_copy(hu_ref, hu_ref, sem).wait()

    @pl.when(pl.program_id(0) == pl.num_programs(0) - 1)
    def _():
        zero[...] = jnp.zeros(zero.shape, F32)
        run = zero.shape[0]

        def zero_range(e, _):
            n_runs = npad_ref[e] // run
            n_rows = npad_ref[e] - n_runs * run
            row0 = pad0_ref[e] + n_runs * run

            def run_copy(c):
                return pltpu.make_async_copy(zero, xs_ref.at[pl.ds(pad0_ref[e] + c * run, run)], zsem)

            def row_copy0(r):
                return pltpu.make_async_copy(zero.at[0], xs_ref.at[row0 + r], zsem)

            def loop(n, make, wait):
                def body(c, carry):
                    if wait:
                        make(c).wait()
                    else:
                        make(c).start()
                    return carry
                lax.fori_loop(0, n, body, 0)

            loop(n_runs, run_copy, False)
            loop(n_rows, row_copy0, False)
            loop(n_runs, run_copy, True)
            loop(n_rows, row_copy0, True)
            return 0

        lax.fori_loop(0, pad0_ref.shape[0], zero_range, 0)


def _disp_call(pad0, npad, dest_flat, hu, n_rows):
    t = hu.shape[0]
    grid_spec = pltpu.PrefetchScalarGridSpec(
        num_scalar_prefetch=2,
        grid=(t // ROW_TILE,),
        in_specs=[
            pl.BlockSpec((ROW_TILE * TOP_K,), lambda i, p0, n: (i,), memory_space=pltpu.SMEM),
            pl.BlockSpec((ROW_TILE, SUBLANES, LANES), lambda i, p0, n: (i, 0, 0)),
        ],
        out_specs=pl.BlockSpec(memory_space=pl.ANY),
        scratch_shapes=[pltpu.VMEM((ZERO_RUN, SUBLANES, LANES), F32), pltpu.SemaphoreType.DMA(()),
                        pltpu.SemaphoreType.DMA(())],
    )
    return pl.pallas_call(
        _disp_kernel,
        grid_spec=grid_spec,
        out_shape=jax.ShapeDtypeStruct((n_rows, SUBLANES, LANES), F32),
        compiler_params=pltpu.CompilerParams(dimension_semantics=("arbitrary",)),
        name="dispatch",
    )(pad0, npad, dest_flat, hu)


def _exp_kernel(be_ref, nu_ref, xs_ref, wgu_ref, bgu_ref, wdn_ref, bdn_ref, ys_ref, wgu_b, wdn_b):
    i = pl.program_id(0)
    ff = wdn_ref.shape[1]
    rows = xs_ref.shape[0] // SUBLANES

    @pl.when((i == 0) | (be_ref[i] != be_ref[jnp.maximum(i - 1, 0)]))
    def _():
        wgu_b[...] = wgu_ref[0].astype(BF16)
        wdn_b[...] = wdn_ref[0].astype(BF16)

    @pl.when(i < nu_ref[0])
    def _():
        xb = jnp.concatenate([_load_row_tiles(xs_ref, 0, rows, c).astype(BF16) for c in range(SUBLANES)],
                             axis=1)
        gu = _dot(xb, wgu_b[...]) + bgu_ref[0]
        glu = jnp.minimum(gu[:, :ff], SWIGLU_LIMIT)
        lin = jnp.clip(gu[:, ff:], -SWIGLU_LIMIT, SWIGLU_LIMIT)
        act = glu * jax.nn.sigmoid(SWIGLU_ALPHA * glu) * (lin + 1.0)
        _store_row_tiles(ys_ref, _dot(act.astype(BF16), wdn_b[...]) + bdn_ref[0])

    @pl.when(i >= nu_ref[0])
    def _():
        ys_ref[...] = jnp.zeros_like(ys_ref)


def _exp_call(block_e, n_used, xs, layer, wgu, bgu, wdn, bdn):
    n_rows = xs.shape[0] // SUBLANES
    _, _, d, ff2 = wgu.shape
    ff = ff2 // 2
    nb = n_rows // EXPERT_ROWS
    grid_spec = pltpu.PrefetchScalarGridSpec(
        num_scalar_prefetch=2,
        grid=(nb,),
        in_specs=[
            pl.BlockSpec((EXPERT_ROWS * SUBLANES, LANES), lambda i, be, nu: (jnp.minimum(i, nu[0] - 1), 0)),
            pl.BlockSpec((None, 1, d, ff2), lambda i, be, nu: (layer, be[i], 0, 0)),
            pl.BlockSpec((1, 1, ff2), lambda i, be, nu: (be[i], 0, 0)),
            pl.BlockSpec((None, 1, ff, d), lambda i, be, nu: (layer, be[i], 0, 0)),
            pl.BlockSpec((1, 1, d), lambda i, be, nu: (be[i], 0, 0)),
        ],
        out_specs=pl.BlockSpec((EXPERT_ROWS * SUBLANES, LANES), lambda i, be, nu: (i, 0)),
        scratch_shapes=[pltpu.VMEM((d, ff2), BF16), pltpu.VMEM((ff, d), BF16)],
    )
    return pl.pallas_call(
        _exp_kernel,
        grid_spec=grid_spec,
        out_shape=jax.ShapeDtypeStruct((n_rows * SUBLANES, LANES), F32),
        compiler_params=pltpu.CompilerParams(dimension_semantics=("arbitrary",),
                                             vmem_limit_bytes=EXPERT_VMEM_LIMIT),
        name="experts",
    )(block_e, n_used, xs, wgu, bgu.reshape(-1, 1, ff2), wdn, bdn.reshape(-1, 1, d))


def _comb_kernel(dest_ref, x_ref, mod_ref, tw_ref, ys_ref, o_ref, buf, sem):
    def row_copy(r, k):
        r0 = pl.multiple_of((k * ROW_TILE + r) * SUBLANES, SUBLANES)
        return pltpu.make_async_copy(ys_ref.at[dest_ref[r * TOP_K + k]], buf.at[pl.ds(r0, SUBLANES)], sem)

    def start(r, _):
        for k in range(TOP_K):
            row_copy(r, k).start(priority=k % 2)
        return 0

    lax.fori_loop(0, ROW_TILE, start, 0)
    pltpu.make_async_copy(buf, buf, sem).wait()
    w = [tw_ref[0, :, k:k + 1] for k in range(TOP_K)]
    for c in range(SUBLANES):
        acc = _load_row_tiles(buf, 0, ROW_TILE, c) * w[0]
        for k in range(1, TOP_K):
            acc = acc + _load_row_tiles(buf, k * ROW_TILE, ROW_TILE, c) * w[k]
        cols = slice(c * LANES, (c + 1) * LANES)
        o_ref[0, :, cols] = x_ref[0, :, cols] + mod_ref[0, 5:6, cols] * acc


def _comb_call(dest_flat, s, mod, tw, ys, first_tile):
    b, rows, d = s.shape
    nt = rows // ROW_TILE

    def mod_idx(i, j):
        return (jnp.where(j + first_tile == 0, b, i), 0, 0)

    return pl.pallas_call(
        _comb_kernel,
        grid=(b, nt),
        in_specs=[
            pl.BlockSpec((ROW_TILE * TOP_K,), lambda i, j: (i * nt + j,), memory_space=pltpu.SMEM),
            pl.BlockSpec((1, ROW_TILE, d), lambda i, j: (i, j, 0)),
            pl.BlockSpec((1, N_MOD, d), mod_idx),
            pl.BlockSpec((1, ROW_TILE, TOP_K), lambda i, j: (i, j, 0)),
            pl.BlockSpec(memory_space=pl.ANY),
        ],
        out_specs=pl.BlockSpec((1, ROW_TILE, d), lambda i, j: (i, j, 0)),
        out_shape=jax.ShapeDtypeStruct((b, rows, d), F32),
        scratch_shapes=[pltpu.VMEM((TOP_K * ROW_TILE * SUBLANES, LANES), F32), pltpu.SemaphoreType.DMA(())],
        compiler_params=pltpu.CompilerParams(dimension_semantics=("arbitrary", "arbitrary"),
                                             vmem_limit_bytes=VMEM_LIMIT),
        name="combine",
    )(dest_flat, s, mod, tw, ys)


def _rope_tables(seq_len):
    rows = seq_len // GRID_W
    row = jnp.broadcast_to(jnp.arange(rows, dtype=F32)[:, None], (rows, GRID_W)).reshape(seq_len)
    col = jnp.broadcast_to(jnp.arange(GRID_W, dtype=F32)[None, :], (rows, GRID_W)).reshape(seq_len)
    inv = ROPE_BASE ** (-jnp.arange(ROPE_PAIRS, dtype=F32) / ROPE_PAIRS)
    ang = jnp.stack([row, col], axis=0)[:, None, :] * inv[None, :, None]
    cos_l = jnp.cos(ang).reshape(2 * ROPE_PAIRS, seq_len)
    sin_l = jnp.sin(ang).reshape(2 * ROPE_PAIRS, seq_len)
    cos_t = jnp.concatenate([jnp.ones((2 * ROPE_PAIRS, CTX_LEN), F32), cos_l], axis=1)
    sin_t = jnp.concatenate([jnp.zeros((2 * ROPE_PAIRS, CTX_LEN), F32), sin_l], axis=1)
    return cos_t, sin_t


def kernel(x, c, ctx, c_ctx, w_mod, b_mod, norm1_g, norm2_g, w_in, q_norm_g, k_norm_g,
           lambda_q1, lambda_k1, lambda_q2, lambda_k2, subln_g, conv_w, pool_w, pool_scale, w_out,
           router_w, router_b, w_gate_up, b_gate_up, w_down, b_down):
    b, seq_len, d = x.shape
    assert d == SUBLANES * LANES
    depth = w_mod.shape[0]
    la = CTX_LEN + seq_len

    rows = -(-(b + STEP_SAMPLES) // SUBLANES) * SUBLANES
    c_all = jnp.zeros((rows, d), F32).at[:b].set(c).at[b:b + STEP_SAMPLES].set(c_ctx)
    mod = _mod_call(c_all, w_mod, b_mod).reshape(depth, rows, N_MOD, d)

    cos_t, sin_t = _rope_tables(seq_len)
    s_ctx, s_x = ctx, x

    for l in range(depth):
        first_tile = 1 if l + 1 == depth else 0
        t_all = b * (la - first_tile * ROW_TILE)
        n_rows = -(-(t_all * TOP_K + N_EXPERTS * (EXPERT_ROWS - 1)) // EXPERT_ROWS) * EXPERT_ROWS
        nb = n_rows // EXPERT_ROWS
        lam_init = 0.8 - 0.6 * math.exp(-0.3 * l)
        wl = w_in[l]
        wm = wl[:, 2 * ATTN_W:].astype(BF16)
        wqk = wl[:, :2 * ATTN_W].T.astype(BF16)
        q_gain = jnp.tile(q_norm_g[l], ATTN_W // QK_DIM) * (LOG2E * QK_DIM ** -0.5)
        k_gain = jnp.tile(k_norm_g[l], ATTN_W // QK_DIM)
        gain = jnp.broadcast_to(jnp.concatenate([q_gain, k_gain])[:, None], (2 * ATTN_W, ROW_TILE))
        q, kt, v, cp = _in_call(s_ctx, s_x, la, mod[l], norm1_g[l][None], wm, wqk, gain, cos_t, sin_t)

        lam_p = jnp.zeros((SUBLANES, LANES), F32)
        lam_p = lam_p.at[0, :QK_DIM].set(lambda_q1[l]).at[1, :QK_DIM].set(lambda_k1[l])
        lam_p = lam_p.at[2, :QK_DIM].set(lambda_q2[l]).at[3, :QK_DIM].set(lambda_k2[l])
        sg = jnp.tile(subln_g[l], 2)[None]
        score_bound = jnp.full((1, LANES), QK_DIM * BOUND_SLACK, F32) * (
            jnp.max(jnp.abs(q_gain)) * jnp.max(jnp.abs(k_gain)))
        oa = _attn_call(lam_p, score_bound, q, kt, v, sg, lam_init, first_tile)

        pw = jax.scipy.linalg.block_diag(*[pool_w[l, g] for g in range(len(POOL_WINDOWS))]).astype(BF16)
        wr = jnp.zeros((d, LANES), F32).at[:, :N_EXPERTS].set(router_w[l])
        wrh, wrl = _split_bf16(wr)
        br = jnp.full((1, LANES), NEG_BIG, F32).at[0, :N_EXPERTS].set(router_b[l])
        s, hu, te, tw = _out_call(s_ctx, s_x, oa, cp, conv_w[l], pw, pool_scale[l][None],
                                  w_out[l].astype(BF16), mod[l], norm2_g[l][None], wrh, wrl, br,
                                  seq_len, first_tile)

        dest, counts = _rank_call(te.reshape(t_all, TOP_K))
        cnt = counts[0, :N_EXPERTS]
        padded = (cnt + EXPERT_ROWS - 1) // EXPERT_ROWS * EXPERT_ROWS
        pad_end = jnp.cumsum(padded)
        n_used = (pad_end[-1] // EXPERT_ROWS).astype(I32)
        blk = jnp.minimum(jnp.arange(nb, dtype=I32), n_used - 1) * EXPERT_ROWS
        block_e = jnp.sum((pad_end[None, :] <= blk[:, None]).astype(I32), axis=1)
        block_e = jnp.minimum(block_e, N_EXPERTS - 1)
        dest_flat = dest.reshape(t_all * TOP_K)

        pad0 = jnp.concatenate([pad_end - padded + cnt, pad_end[-1:]])
        npad = jnp.concatenate([padded - cnt, n_rows - pad_end[-1:]])
        xs = _disp_call(pad0, npad, dest_flat, hu.reshape(t_all, SUBLANES, LANES), n_rows)
        ys = _exp_call(block_e, n_used.reshape(1), xs.reshape(n_rows * SUBLANES, LANES), l,
                       w_gate_up, b_gate_up[l], w_down, b_down[l])
        s = _comb_call(dest_flat, s, mod[l], tw, ys.reshape(n_rows, SUBLANES, LANES), first_tile)
        s_ctx = s_x = s
    return s
```

```python
import functools
import math

import jax
import jax.numpy as jnp
from jax import lax
from jax.experimental import pallas as pl
from jax.experimental.pallas import tpu as pltpu

F32 = jnp.float32
BF16 = jnp.bfloat16
I32 = jnp.int32
U32 = jnp.uint32

GRID_W = 64
CTX_LEN = 256
HEAD_DIM = 64
QK_DIM = 32
N_HEADS = 8
ATTN_W = 512
CONV_W = 256
POOL_W = 256
POOL_WINDOWS = (2, 4, 8, 16)
POOL_GC = 64
ROPE_BASE = 10000.0
ROPE_PAIRS = 8
N_EXPERTS = 32
TOP_K = 4
N_MOD = 6
EPS = 1e-6
SWIGLU_ALPHA = 1.702
SWIGLU_LIMIT = 7.0
LOG2E = 1.4426950408889634

LANES = 128
SUBLANES = 8
ROW_TILE = 256
STEP_SAMPLES = 2
N_PAIRS = N_HEADS // 2
POOL_HALO = 8
EXPERT_ROWS = 512
RANK_TILE = 512
ZERO_RUN = 64
KEY_UNROLL = 16
VMEM_LIMIT = 48 * 1024 * 1024
ATTN_VMEM_LIMIT = 40 * 1024 * 1024
EXPERT_VMEM_LIMIT = 58 * 1024 * 1024
NEG_BIG = -1e30
BOUND_SLACK = 1.0 + 2.0 ** -6
UNDERFLOW_GUARD = 2.0 ** -100


def _split_bf16(a):
    hi = a.astype(BF16)
    lo = (a - hi.astype(F32)).astype(BF16)
    return hi, lo


def _dot(a, b):
    return jnp.dot(a, b, preferred_element_type=F32)


def _mod_kernel(c_ref, w_ref, b_ref, o_ref):
    cv = c_ref[...]
    s = cv * jax.nn.sigmoid(cv)
    s_hi, s_lo = _split_bf16(s)
    w_hi, w_lo = _split_bf16(w_ref[0])
    o_ref[0] = _dot(s_hi, w_hi) + _dot(s_lo, w_hi) + _dot(s_hi, w_lo) + b_ref[0]


def _mod_call(c_all, w_mod, b_mod):
    depth, d, n = w_mod.shape
    r = c_all.shape[0]
    tn = 1536
    return pl.pallas_call(
        _mod_kernel,
        grid=(depth, n // tn),
        in_specs=[
            pl.BlockSpec((r, d), lambda l, j: (0, 0)),
            pl.BlockSpec((1, d, tn), lambda l, j: (l, 0, j)),
            pl.BlockSpec((1, 1, tn), lambda l, j: (l, 0, j)),
        ],
        out_specs=pl.BlockSpec((1, r, tn), lambda l, j: (l, 0, j)),
        out_shape=jax.ShapeDtypeStruct((depth, r, n), F32),
        compiler_params=pltpu.CompilerParams(vmem_limit_bytes=VMEM_LIMIT),
        name="mod",
    )(c_all, w_mod, b_mod.reshape(depth, 1, n))


def _rms_mod(x, g, shift, scale):
    y = x * lax.rsqrt(jnp.mean(x * x, axis=-1, keepdims=True) + EPS) * g
    return y * (1.0 + scale) + shift


def _stream_tile(c_ref, x_ref, j, s=0):
    return jnp.where(j == 0, c_ref[s], x_ref[s])


def _stream_specs(d, first_tile, x_is_latent, samples=1):
    shift = first_tile - (1 if x_is_latent else 0)
    return [pl.BlockSpec((samples, ROW_TILE, d), lambda i, j: (i, 0, 0)),
            pl.BlockSpec((samples, ROW_TILE, d), lambda i, j: (i, jnp.maximum(j + shift, 0), 0))]


def _in_kernel(c_ref, x_ref, mod_ref, g1_ref, wm_ref, wqk_ref, gain_ref, cos_ref, sin_ref,
               q_ref, kt_ref, v_ref, cp_ref):
    for s in range(x_ref.shape[0]):
        _in_one_sample(s, c_ref, x_ref, mod_ref, g1_ref, wm_ref, wqk_ref, gain_ref, cos_ref, sin_ref,
                       q_ref, kt_ref, v_ref, cp_ref)


def _in_one_sample(s, c_ref, x_ref, mod_ref, g1_ref, wm_ref, wqk_ref, gain_ref, cos_ref, sin_ref,
                   q_ref, kt_ref, v_ref, cp_ref):
    tm = x_ref.shape[1]
    xin = _stream_tile(c_ref, x_ref, pl.program_id(1), s)
    h = _rms_mod(xin, g1_ref[...], mod_ref[s, 0:1, :], mod_ref[s, 1:2, :])
    pm = _dot(h.astype(BF16), wm_ref[...])
    for p in range(N_PAIRS):
        v_ref[s, p] = pm[:, p * LANES:(p + 1) * LANES].astype(BF16)
    o = ATTN_W
    cp_ref[s, :, 0:CONV_W] = pm[:, o:o + CONV_W]
    cp_ref[s, :, CONV_W:2 * CONV_W] = pm[:, o + CONV_W:o + 2 * CONV_W] * pm[:, o + 2 * CONV_W:o + 3 * CONV_W]
    cp_ref[s, :, 2 * CONV_W:] = pm[:, o + 3 * CONV_W:]

    ht = h.T.astype(BF16)
    qkt = _dot(wqk_ref[...], ht)
    ng = 2 * ATTN_W // QK_DIM
    t = qkt.reshape(ng, 4, ROPE_PAIRS, tm)
    ss = jnp.sum(jnp.sum(t * t, axis=2, keepdims=True), axis=1, keepdims=True)
    tn = t * lax.rsqrt(ss * (1.0 / QK_DIM) + EPS) * gain_ref[...].reshape(ng, 4, ROPE_PAIRS, tm)
    cs = cos_ref[...].reshape(2, ROPE_PAIRS, tm)
    sn = sin_ref[...].reshape(2, ROPE_PAIRS, tm)
    parts = []
    for a in range(2):
        t1 = tn[:, 2 * a]
        t2 = tn[:, 2 * a + 1]
        parts.append(t1 * cs[a] - t2 * sn[a])
        parts.append(t2 * cs[a] + t1 * sn[a])
    rot = jnp.stack(parts, axis=1).reshape(2 * ATTN_W, tm)
    qt = rot[:ATTN_W].T
    for p in range(N_PAIRS):
        q_ref[s, p] = qt[:, p * LANES:(p + 1) * LANES].astype(BF16)
        kt_ref[s, 0, p] = rot[ATTN_W + p * LANES:ATTN_W + (p + 1) * LANES].astype(BF16)


def _in_call(s_ctx, s_x, la, mod, g1, wm, wqk, gain, cos_t, sin_t):
    b, _, d = s_x.shape
    nt = la // ROW_TILE
    nmain = wm.shape[1]
    sps = STEP_SAMPLES
    assert b % sps == 0

    def mod_idx(i, j):
        return (jnp.where(j == 0, b // sps, i), 0, 0)

    return pl.pallas_call(
        _in_kernel,
        grid=(b // sps, nt),
        in_specs=_stream_specs(d, 0, s_x.shape[1] < la, sps) + [
            pl.BlockSpec((sps, N_MOD, d), mod_idx),
            pl.BlockSpec((1, d), lambda i, j: (0, 0)),
            pl.BlockSpec((d, nmain), lambda i, j: (0, 0)),
            pl.BlockSpec((2 * ATTN_W, d), lambda i, j: (0, 0)),
            pl.BlockSpec((2 * ATTN_W, ROW_TILE), lambda i, j: (0, 0)),
            pl.BlockSpec((2 * ROPE_PAIRS, ROW_TILE), lambda i, j: (0, j)),
            pl.BlockSpec((2 * ROPE_PAIRS, ROW_TILE), lambda i, j: (0, j)),
        ],
        out_specs=[
            pl.BlockSpec((sps, N_PAIRS, ROW_TILE, LANES), lambda i, j: (i, 0, j, 0)),
            pl.BlockSpec((sps, 1, N_PAIRS, LANES, ROW_TILE), lambda i, j: (i, j, 0, 0, 0)),
            pl.BlockSpec((sps, N_PAIRS, ROW_TILE, LANES), lambda i, j: (i, 0, j, 0)),
            pl.BlockSpec((sps, ROW_TILE, 3 * CONV_W), lambda i, j: (i, j, 0)),
        ],
        out_shape=[
            jax.ShapeDtypeStruct((b, N_PAIRS, la, LANES), BF16),
            jax.ShapeDtypeStruct((b, nt, N_PAIRS, LANES, ROW_TILE), BF16),
            jax.ShapeDtypeStruct((b, N_PAIRS, la, LANES), BF16),
            jax.ShapeDtypeStruct((b, la, 3 * CONV_W), F32),
        ],
        compiler_params=pltpu.CompilerParams(vmem_limit_bytes=VMEM_LIMIT),
        name="inproj",
    )(s_ctx, s_x, mod, g1, wm, wqk, gain, cos_t, sin_t)


def _attn_kernel(lam_ref, mb_ref, q_ref, kt_ref, v_ref, sg_ref, o_ref, mb_scr, l_scr, acc_scr, *,
                 lam_init, first_tile):
    tq = q_ref.shape[2]
    nkc_all = kt_ref.shape[1]
    j = pl.program_id(1) + first_tile
    n_groups = jnp.where(j == 0, 0, (nkc_all - 1) // KEY_UNROLL)
    n_chunks = jnp.where(j == 0, 1, nkc_all)
    lp = lam_ref[...]
    lam = (jnp.exp(jnp.sum(lp[0:1] * lp[1:2], axis=-1, keepdims=True))
           - jnp.exp(jnp.sum(lp[2:3] * lp[3:4], axis=-1, keepdims=True)) + lam_init)
    lane = lax.broadcasted_iota(I32, (1, LANES), 1)
    low = lane < HEAD_DIM
    groups = [(lane >= g * QK_DIM) & (lane < (g + 1) * QK_DIM) for g in range(4)]
    bound = mb_ref[...]

    def stacked_queries(p):
        qp = q_ref[0, p]
        return jnp.concatenate([jnp.where(groups[g], qp, jnp.zeros_like(qp)) for g in range(4)], axis=0)

    def exp_pv(p, qs, kc, mb):
        sc = _dot(qs, kt_ref[0, kc, p])
        e0 = jnp.exp2(sc[:, :LANES] - mb)
        e1 = jnp.exp2(sc[:, LANES:] - mb)
        r0 = pl.multiple_of(kc * ROW_TILE, ROW_TILE)
        e = jnp.concatenate([e0.astype(BF16), e1.astype(BF16)], axis=1)
        return e0 + e1, _dot(e, v_ref[0, p, pl.ds(r0, ROW_TILE), :])

    def finish(p, lsum):
        on = acc_scr[...] / lsum
        heads = [on[(2 * hh) * tq:(2 * hh + 1) * tq] - lam * on[(2 * hh + 1) * tq:(2 * hh + 2) * tq]
                 for hh in range(2)]
        o = jnp.where(low, heads[0], heads[1])
        o2 = o * o
        ss_lo = jnp.sum(jnp.where(low, o2, 0.0), axis=-1, keepdims=True)
        ss_hi = jnp.sum(jnp.where(low, 0.0, o2), axis=-1, keepdims=True)
        inv = lax.rsqrt(jnp.where(low, ss_lo, ss_hi) * (1.0 / HEAD_DIM) + EPS)
        o_ref[0, p] = (o * inv * sg_ref[...] * (1.0 - lam_init)).astype(BF16)

    lmin = None
    for p in range(N_PAIRS):
        qs = stacked_queries(p)

        def sweep(g, _, p=p, qs=qs):
            parts = [exp_pv(p, qs, 1 + g * KEY_UNROLL + u, bound) for u in range(KEY_UNROLL)]
            while len(parts) > 1:
                parts = [(a[0] + b[0], a[1] + b[1]) for a, b in zip(parts[::2], parts[1::2])]
            l_scr[...] = l_scr[...] + parts[0][0]
            acc_scr[...] = acc_scr[...] + parts[0][1]
            return 0

        l_scr[...], acc_scr[...] = exp_pv(p, qs, 0, bound)
        lax.fori_loop(0, n_groups, sweep, 0)
        lsum = jnp.sum(l_scr[...], axis=-1, keepdims=True)
        lmin = lsum if lmin is None else jnp.minimum(lmin, lsum)
        finish(p, lsum)

    @pl.when(jnp.logical_not(jnp.min(lmin) >= UNDERFLOW_GUARD))
    def _():
        for p in range(N_PAIRS):
            qs = stacked_queries(p)

            def row_max(kc, m, p=p, qs=qs):
                sc = _dot(qs, kt_ref[0, kc, p])
                return jnp.maximum(m, jnp.maximum(sc[:, :LANES], sc[:, LANES:]))

            m = lax.fori_loop(0, n_chunks, row_max, jnp.full((4 * tq, LANES), -jnp.inf, F32))
            mb_scr[...] = jnp.broadcast_to(jnp.max(m, axis=-1, keepdims=True), mb_scr.shape)
            l_scr[...] = jnp.zeros(l_scr.shape, F32)
            acc_scr[...] = jnp.zeros(acc_scr.shape, F32)

            def redo(kc, _, p=p, qs=qs):
                dl, dacc = exp_pv(p, qs, kc, mb_scr[...])
                l_scr[...] = l_scr[...] + dl
                acc_scr[...] = acc_scr[...] + dacc
                return 0

            lax.fori_loop(0, n_chunks, redo, 0)
            finish(p, jnp.sum(l_scr[...], axis=-1, keepdims=True))


def _attn_call(lam_p, score_bound, q, kt, v, sg, lam_init, first_tile):
    b, _, la, _ = q.shape
    nt = la // ROW_TILE
    assert (nt - 1) % KEY_UNROLL == 0
    off = first_tile
    return pl.pallas_call(
        functools.partial(_attn_kernel, lam_init=lam_init, first_tile=off),
        grid=(b, nt - off),
        in_specs=[
            pl.BlockSpec((SUBLANES, LANES), lambda i, j: (0, 0)),
            pl.BlockSpec((1, LANES), lambda i, j: (0, 0)),
            pl.BlockSpec((1, N_PAIRS, ROW_TILE, LANES), lambda i, j: (i, 0, j + off, 0)),
            pl.BlockSpec((1, nt, N_PAIRS, LANES, ROW_TILE), lambda i, j: (i, 0, 0, 0, 0)),
            pl.BlockSpec((1, N_PAIRS, la, LANES), lambda i, j: (i, 0, 0, 0)),
            pl.BlockSpec((1, LANES), lambda i, j: (0, 0)),
        ],
        out_specs=pl.BlockSpec((1, N_PAIRS, ROW_TILE, LANES), lambda i, j: (i, 0, j, 0)),
        out_shape=jax.ShapeDtypeStruct((b, N_PAIRS, la - off * ROW_TILE, LANES), BF16),
        scratch_shapes=[pltpu.VMEM((4 * ROW_TILE, LANES), F32),
                        pltpu.VMEM((4 * ROW_TILE, LANES), F32),
                        pltpu.VMEM((4 * ROW_TILE, LANES), F32)],
        compiler_params=pltpu.CompilerParams(vmem_limit_bytes=ATTN_VMEM_LIMIT),
        name="attn",
    )(lam_p, score_bound, q, kt, v, sg)


def _store_row_tiles(ref, val):
    rows = val.shape[0]
    for c in range(SUBLANES):
        ref[pl.ds(c, rows, stride=SUBLANES), :] = val[:, c * LANES:(c + 1) * LANES]


def _load_row_tiles(ref, first_row, rows, c):
    return ref[pl.ds(first_row * SUBLANES + c, rows, stride=SUBLANES), :]


def _shift_rows(a, k):
    return pltpu.roll(a, k % a.shape[0], axis=0)


def _out_kernel(*refs, seq_len, first_tile):
    for s in range(refs[1].shape[0]):
        _out_one_sample(s, *refs, seq_len=seq_len, first_tile=first_tile)


def _out_one_sample(s, c_ref, x_ref, oa_ref, cp_ref, cpp_ref, cpn_ref, cw_ref, pw_ref, ps_ref, wo_ref,
                    mod_ref, g2_ref, wrh_ref, wrl_ref, br_ref,
                    xo_ref, hu_ref, te_ref, tw_ref, *, seq_len, first_tile):
    tm = x_ref.shape[1]
    j = pl.program_id(1) + first_tile
    nt = pl.num_programs(1) + first_tile
    halo = POOL_HALO
    has_prev = j >= 2
    has_next = (j >= 1) & (j < nt - 1)
    prev = jnp.where(has_prev, cpp_ref[s], 0.0)
    nxt = jnp.where(has_next, cpn_ref[s], 0.0)
    ext = jnp.concatenate([prev, cp_ref[s], nxt], axis=0)

    z = ext[:, CONV_W:2 * CONV_W]
    conv = (cw_ref[0:1, :] * _shift_rows(z, 1) + cw_ref[1:2, :] * z + cw_ref[2:3, :] * _shift_rows(z, -1))
    o_conv = ext[halo:halo + tm, 0:CONV_W] * conv[halo:halo + tm]

    u = ext[:, 2 * CONV_W:]
    a2 = u + _shift_rows(u, 1)
    a4 = _shift_rows(a2, -1) + _shift_rows(a2, 1)
    a8 = _shift_rows(a4, -2) + _shift_rows(a4, 2)
    a16 = _shift_rows(a8, -4) + _shift_rows(a8, 4)
    lane = lax.broadcasted_iota(I32, (tm, POOL_W), 1)
    grp = lane // POOL_GC
    wsum = jnp.where(grp == 0, a2[halo:halo + tm],
                     jnp.where(grp == 1, a4[halo:halo + tm],
                               jnp.where(grp == 2, a8[halo:halo + tm], a16[halo:halo + tm])))
    half = jnp.where(grp == 0, 1, jnp.where(grp == 1, 2, jnp.where(grp == 2, 4, 8)))
    row = lax.broadcasted_iota(I32, (tm, POOL_W), 0)
    pos = jnp.where(j == 0, row, (j - 1) * tm + row)
    n_seq = jnp.where(j == 0, CTX_LEN, seq_len)
    cnt = jnp.minimum(pos + half, n_seq) - jnp.maximum(pos - half, 0)
    dlt = wsum / cnt.astype(F32) - u[halo:halo + tm]
    o_pool = _dot(dlt.astype(BF16), pw_ref[...]) * ps_ref[...]

    mix = _dot(o_conv.astype(BF16), wo_ref[ATTN_W:ATTN_W + CONV_W, :])
    mix = mix + _dot(o_pool.astype(BF16), wo_ref[ATTN_W + CONV_W:, :])
    for p in range(N_PAIRS):
        mix = mix + _dot(oa_ref[s, p], wo_ref[p * LANES:(p + 1) * LANES, :])
    x = _stream_tile(c_ref, x_ref, j, s) + mod_ref[s, 2:3, :] * mix
    xo_ref[s] = x

    h2 = _rms_mod(x, g2_ref[...], mod_ref[s, 3:4, :], mod_ref[s, 4:5, :])
    _store_row_tiles(hu_ref.at[s], h2)

    h_hi, h_lo = _split_bf16(h2)
    logits = (_dot(h_hi, wrh_ref[...]) + _dot(h_lo, wrh_ref[...]) + _dot(h_hi, wrl_ref[...])
              + br_ref[...])
    lanef = lax.broadcasted_iota(I32, (tm, LANES), 1).astype(F32)
    work = logits
    tops = []
    for k in range(TOP_K):
        mk = jnp.max(work, axis=-1, keepdims=True)
        ik = jnp.min(jnp.where(work == mk, lanef, float(LANES)), axis=-1, keepdims=True)
        te_ref[s, :, k:k + 1] = ik.astype(I32)
        work = jnp.where(lanef == ik, -jnp.inf, work)
        tops.append(mk)
    es = [jnp.exp(mk - tops[0]) for mk in tops]
    den = es[0] + es[1] + es[2] + es[3]
    for k in range(TOP_K):
        tw_ref[s, :, k:k + 1] = es[k] / den


def _out_call(s_ctx, s_x, oa, cp, cw, pw, ps, wo, mod, g2, wrh, wrl, br, seq_len, first_tile):
    b, _, d = s_x.shape
    la = CTX_LEN + seq_len
    nt = la // ROW_TILE
    off = first_tile
    lo = la - off * ROW_TILE
    hb = ROW_TILE // POOL_HALO
    nhb = la // POOL_HALO
    sps = STEP_SAMPLES
    assert b % sps == 0

    def mod_idx(i, j):
        return (jnp.where(j + off == 0, b // sps, i), 0, 0)

    const2 = lambda i, j: (0, 0)
    return pl.pallas_call(
        functools.partial(_out_kernel, seq_len=seq_len, first_tile=off),
        grid=(b // sps, nt - off),
        in_specs=_stream_specs(d, off, s_x.shape[1] < la, sps) + [
            pl.BlockSpec((sps, N_PAIRS, ROW_TILE, LANES), lambda i, j: (i, 0, j, 0)),
            pl.BlockSpec((sps, ROW_TILE, 3 * CONV_W), lambda i, j: (i, j + off, 0)),
            pl.BlockSpec((sps, POOL_HALO, 3 * CONV_W), lambda i, j: (i, jnp.maximum((j + off) * hb - 1, 0), 0)),
            pl.BlockSpec((sps, POOL_HALO, 3 * CONV_W),
                         lambda i, j: (i, jnp.minimum((j + off + 1) * hb, nhb - 1), 0)),
            pl.BlockSpec((3, CONV_W), const2),
            pl.BlockSpec((POOL_W, POOL_W), const2),
            pl.BlockSpec((1, POOL_W), const2),
            pl.BlockSpec((d, d), const2),
            pl.BlockSpec((sps, N_MOD, d), mod_idx),
            pl.BlockSpec((1, d), const2),
            pl.BlockSpec((d, LANES), const2),
            pl.BlockSpec((d, LANES), const2),
            pl.BlockSpec((1, LANES), const2),
        ],
        out_specs=[
            pl.BlockSpec((sps, ROW_TILE, d), lambda i, j: (i, j, 0)),
            pl.BlockSpec((sps, ROW_TILE * SUBLANES, LANES), lambda i, j: (i, j, 0)),
            pl.BlockSpec((sps, ROW_TILE, TOP_K), lambda i, j: (i, j, 0)),
            pl.BlockSpec((sps, ROW_TILE, TOP_K), lambda i, j: (i, j, 0)),
        ],
        out_shape=[
            jax.ShapeDtypeStruct((b, lo, d), F32),
            jax.ShapeDtypeStruct((b, lo * SUBLANES, LANES), F32),
            jax.ShapeDtypeStruct((b, lo, TOP_K), I32),
            jax.ShapeDtypeStruct((b, lo, TOP_K), F32),
        ],
        compiler_params=pltpu.CompilerParams(vmem_limit_bytes=VMEM_LIMIT),
        name="outproj",
    )(s_ctx, s_x, oa, cp, cp, cp, cw, pw, ps, wo, mod, g2, wrh, wrl, br)


def _rank_kernel(te_ref, tri_ref, dest_ref, cnt_ref, carry):
    ph = pl.program_id(0)
    i = pl.program_id(1)
    tr = te_ref.shape[0]
    lane1 = lax.broadcasted_iota(I32, (SUBLANES, LANES), 1)

    @pl.when((ph == 0) & (i == 0))
    def _():
        carry[...] = jnp.zeros_like(carry)

    @pl.when((ph == 1) & (i == 0))
    def _():
        cnt = carry[...]
        cnt_ref[...] = cnt.astype(I32)
        padded = jnp.ceil(cnt * (1.0 / EXPERT_ROWS)) * EXPERT_ROWS
        incl = padded
        for sh in (1, 2, 4, 8, 16):
            incl = incl + jnp.where(lane1 >= sh, pltpu.roll(incl, sh, axis=1), 0.0)
        carry[...] = incl - padded

    e = te_ref[...]
    lane = lax.broadcasted_iota(I32, (tr, LANES), 1)
    ohs = [(e[:, k:k + 1] == lane).astype(F32) for k in range(TOP_K)]
    m = ohs[0] + ohs[1] + ohs[2] + ohs[3]

    @pl.when(ph == 0)
    def _():
        dest_ref[...] = jnp.zeros(dest_ref.shape, I32)

    @pl.when(ph == 1)
    def _():
        base = _dot(tri_ref[...], m.astype(BF16)) + carry[0:1, :]
        for k in range(TOP_K):
            dest_ref[:, k:k + 1] = jnp.sum(ohs[k] * base, axis=-1, keepdims=True).astype(I32)

    carry[...] = carry[...] + jnp.sum(m, axis=0, keepdims=True)


def _rank_call(te):
    t = te.shape[0]
    nt = t // RANK_TILE
    r = lax.broadcasted_iota(I32, (RANK_TILE, RANK_TILE), 0)
    c = lax.broadcasted_iota(I32, (RANK_TILE, RANK_TILE), 1)
    tri = (c < r).astype(BF16)
    return pl.pallas_call(
        _rank_kernel,
        grid=(2, nt),
        in_specs=[
            pl.BlockSpec((RANK_TILE, TOP_K), lambda ph, i: (i, 0)),
            pl.BlockSpec((RANK_TILE, RANK_TILE), lambda ph, i: (0, 0)),
        ],
        out_specs=[
            pl.BlockSpec((RANK_TILE, TOP_K), lambda ph, i: (ph * i, 0)),
            pl.BlockSpec((SUBLANES, LANES), lambda ph, i: (0, 0)),
        ],
        out_shape=[
            jax.ShapeDtypeStruct((t, TOP_K), I32),
            jax.ShapeDtypeStruct((SUBLANES, LANES), I32),
        ],
        scratch_shapes=[pltpu.VMEM((SUBLANES, LANES), F32)],
        compiler_params=pltpu.CompilerParams(dimension_semantics=("arbitrary", "arbitrary")),
        name="rank",
    )(te, tri)


def _disp_kernel(pad0_ref, npad_ref, dest_ref, hu_ref, xs_ref, zero, sem, zsem):
    def row_copy(r, k):
        return pltpu.make_async_copy(hu_ref.at[r], xs_ref.at[dest_ref[r * TOP_K + k]], sem)

    def start(r, _):
        for k in range(TOP_K):
            row_copy(r, k).start(priority=k % 2)
        return 0

    lax.fori_loop(0, ROW_TILE, start, 0)
    for _ in range(TOP_K):
        pltpu.make_async_copy(hu_ref, hu_ref, sem).wait()

    @pl.when(pl.program_id(0) == pl.num_programs(0) - 1)
    def _():
        zero[...] = jnp.zeros(zero.shape, F32)
        run = zero.shape[0]

        def zero_range(e, _):
            n_runs = npad_ref[e] // run
            n_rows = npad_ref[e] - n_runs * run
            row0 = pad0_ref[e] + n_runs * run

            def run_copy(c):
                return pltpu.make_async_copy(zero, xs_ref.at[pl.ds(pad0_ref[e] + c * run, run)], zsem)

            def row_copy0(r):
                return pltpu.make_async_copy(zero.at[0], xs_ref.at[row0 + r], zsem)

            def loop(n, make, wait):
                def body(c, carry):
                    if wait:
                        make(c).wait()
                    else:
                        make(c).start()
                    return carry
                lax.fori_loop(0, n, body, 0)

            loop(n_runs, run_copy, False)
            loop(n_rows, row_copy0, False)
            loop(n_runs, run_copy, True)
            loop(n_rows, row_copy0, True)
            return 0

        lax.fori_loop(0, pad0_ref.shape[0], zero_range, 0)


def _disp_call(pad0, npad, dest_flat, hu, n_rows):
    t = hu.shape[0]
    grid_spec = pltpu.PrefetchScalarGridSpec(
        num_scalar_prefetch=2,
        grid=(t // ROW_TILE,),
        in_specs=[
            pl.BlockSpec((ROW_TILE * TOP_K,), lambda i, p0, n: (i,), memory_space=pltpu.SMEM),
            pl.BlockSpec((ROW_TILE, SUBLANES, LANES), lambda i, p0, n: (i, 0, 0)),
        ],
        out_specs=pl.BlockSpec(memory_space=pl.ANY),
        scratch_shapes=[pltpu.VMEM((ZERO_RUN, SUBLANES, LANES), F32), pltpu.SemaphoreType.DMA(()),
                        pltpu.SemaphoreType.DMA(())],
    )
    return pl.pallas_call(
        _disp_kernel,
        grid_spec=grid_spec,
        out_shape=jax.ShapeDtypeStruct((n_rows, SUBLANES, LANES), F32),
        compiler_params=pltpu.CompilerParams(dimension_semantics=("arbitrary",)),
        name="dispatch",
    )(pad0, npad, dest_flat, hu)


def _exp_kernel(be_ref, nu_ref, xs_ref, wgu_ref, bgu_ref, wdn_ref, bdn_ref, ys_ref, wgu_b, wdn_b):
    i = pl.program_id(0)
    ff = wdn_ref.shape[1]
    rows = xs_ref.shape[0] // SUBLANES

    @pl.when((i == 0) | (be_ref[i] != be_ref[jnp.maximum(i - 1, 0)]))
    def _():
        wgu_b[...] = wgu_ref[0].astype(BF16)
        wdn_b[...] = wdn_ref[0].astype(BF16)

    @pl.when(i < nu_ref[0])
    def _():
        xb = jnp.concatenate([_load_row_tiles(xs_ref, 0, rows, c).astype(BF16) for c in range(SUBLANES)],
                             axis=1)
        gu = _dot(xb, wgu_b[...]) + bgu_ref[0]
        glu = jnp.minimum(gu[:, :ff], SWIGLU_LIMIT)
        lin = jnp.clip(gu[:, ff:], -SWIGLU_LIMIT, SWIGLU_LIMIT)
        act = glu * jax.nn.sigmoid(SWIGLU_ALPHA * glu) * (lin + 1.0)
        _store_row_tiles(ys_ref, _dot(act.astype(BF16), wdn_b[...]) + bdn_ref[0])

    @pl.when(i >= nu_ref[0])
    def _():
        ys_ref[...] = jnp.zeros_like(ys_ref)


def _exp_call(block_e, n_used, xs, layer, wgu, bgu, wdn, bdn):
    n_rows = xs.shape[0] // SUBLANES
    _, _, d, ff2 = wgu.shape
    ff = ff2 // 2
    nb = n_rows // EXPERT_ROWS
    grid_spec = pltpu.PrefetchScalarGridSpec(
        num_scalar_prefetch=2,
        grid=(nb,),
        in_specs=[
            pl.BlockSpec((EXPERT_ROWS * SUBLANES, LANES), lambda i, be, nu: (jnp.minimum(i, nu[0] - 1), 0)),
            pl.BlockSpec((None, 1, d, ff2), lambda i, be, nu: (layer, be[i], 0, 0)),
            pl.BlockSpec((1, 1, ff2), lambda i, be, nu: (be[i], 0, 0)),
            pl.BlockSpec((None, 1, ff, d), lambda i, be, nu: (layer, be[i], 0, 0)),
            pl.BlockSpec((1, 1, d), lambda i, be, nu: (be[i], 0, 0)),
        ],
        out_specs=pl.BlockSpec((EXPERT_ROWS * SUBLANES, LANES), lambda i, be, nu: (i, 0)),
        scratch_shapes=[pltpu.VMEM((d, ff2), BF16), pltpu.VMEM((ff, d), BF16)],
    )
    return pl.pallas_call(
        _exp_kernel,
        grid_spec=grid_spec,
        out_shape=jax.ShapeDtypeStruct((n_rows * SUBLANES, LANES), F32),
        compiler_params=pltpu.CompilerParams(dimension_semantics=("arbitrary",),
                                             vmem_limit_bytes=EXPERT_VMEM_LIMIT),
        name="experts",
    )(block_e, n_used, xs, wgu, bgu.reshape(-1, 1, ff2), wdn, bdn.reshape(-1, 1, d))


def _comb_kernel(dest_ref, x_ref, mod_ref, tw_ref, ys_ref, o_ref, buf, sem):
    def row_copy(r, k):
        r0 = pl.multiple_of((k * ROW_TILE + r) * SUBLANES, SUBLANES)
        return pltpu.make_async_copy(ys_ref.at[dest_ref[r * TOP_K + k]], buf.at[pl.ds(r0, SUBLANES)], sem)

    def start(r, _):
        for k in range(TOP_K):
            row_copy(r, k).start(priority=k % 2)
        return 0

    lax.fori_loop(0, ROW_TILE, start, 0)
    pltpu.make_async_copy(buf, buf, sem).wait()
    w = [tw_ref[0, :, k:k + 1] for k in range(TOP_K)]
    for c in range(SUBLANES):
        acc = _load_row_tiles(buf, 0, ROW_TILE, c) * w[0]
        for k in range(1, TOP_K):
            acc = acc + _load_row_tiles(buf, k * ROW_TILE, ROW_TILE, c) * w[k]
        cols = slice(c * LANES, (c + 1) * LANES)
        o_ref[0, :, cols] = x_ref[0, :, cols] + mod_ref[0, 5:6, cols] * acc


def _comb_call(dest_flat, s, mod, tw, ys, first_tile):
    b, rows, d = s.shape
    nt = rows // ROW_TILE

    def mod_idx(i, j):
        return (jnp.where(j + first_tile == 0, b, i), 0, 0)

    return pl.pallas_call(
        _comb_kernel,
        grid=(b, nt),
        in_specs=[
            pl.BlockSpec((ROW_TILE * TOP_K,), lambda i, j: (i * nt + j,), memory_space=pltpu.SMEM),
            pl.BlockSpec((1, ROW_TILE, d), lambda i, j: (i, j, 0)),
            pl.BlockSpec((1, N_MOD, d), mod_idx),
            pl.BlockSpec((1, ROW_TILE, TOP_K), lambda i, j: (i, j, 0)),
            pl.BlockSpec(memory_space=pl.ANY),
        ],
        out_specs=pl.BlockSpec((1, ROW_TILE, d), lambda i, j: (i, j, 0)),
        out_shape=jax.ShapeDtypeStruct((b, rows, d), F32),
        scratch_shapes=[pltpu.VMEM((TOP_K * ROW_TILE * SUBLANES, LANES), F32), pltpu.SemaphoreType.DMA(())],
        compiler_params=pltpu.CompilerParams(dimension_semantics=("arbitrary", "arbitrary"),
                                             vmem_limit_bytes=VMEM_LIMIT),
        name="combine",
    )(dest_flat, s, mod, tw, ys)


def _rope_tables(seq_len):
    rows = seq_len // GRID_W
    row = jnp.broadcast_to(jnp.arange(rows, dtype=F32)[:, None], (rows, GRID_W)).reshape(seq_len)
    col = jnp.broadcast_to(jnp.arange(GRID_W, dtype=F32)[None, :], (rows, GRID_W)).reshape(seq_len)
    inv = ROPE_BASE ** (-jnp.arange(ROPE_PAIRS, dtype=F32) / ROPE_PAIRS)
    ang = jnp.stack([row, col], axis=0)[:, None, :] * inv[None, :, None]
    cos_l = jnp.cos(ang).reshape(2 * ROPE_PAIRS, seq_len)
    sin_l = jnp.sin(ang).reshape(2 * ROPE_PAIRS, seq_len)
    cos_t = jnp.concatenate([jnp.ones((2 * ROPE_PAIRS, CTX_LEN), F32), cos_l], axis=1)
    sin_t = jnp.concatenate([jnp.zeros((2 * ROPE_PAIRS, CTX_LEN), F32), sin_l], axis=1)
    return cos_t, sin_t


def kernel(x, c, ctx, c_ctx, w_mod, b_mod, norm1_g, norm2_g, w_in, q_norm_g, k_norm_g,
           lambda_q1, lambda_k1, lambda_q2, lambda_k2, subln_g, conv_w, pool_w, pool_scale, w_out,
           router_w, router_b, w_gate_up, b_gate_up, w_down, b_down):
    b, seq_len, d = x.shape
    assert d == SUBLANES * LANES
    depth = w_mod.shape[0]
    la = CTX_LEN + seq_len

    rows = -(-(b + STEP_SAMPLES) // SUBLANES) * SUBLANES
    c_all = jnp.zeros((rows, d), F32).at[:b].set(c).at[b:b + STEP_SAMPLES].set(c_ctx)
    mod = _mod_call(c_all, w_mod, b_mod).reshape(depth, rows, N_MOD, d)

    cos_t, sin_t = _rope_tables(seq_len)
    s_ctx, s_x = ctx, x

    for l in range(depth):
        first_tile = 1 if l + 1 == depth else 0
        t_all = b * (la - first_tile * ROW_TILE)
        n_rows = -(-(t_all * TOP_K + N_EXPERTS * (EXPERT_ROWS - 1)) // EXPERT_ROWS) * EXPERT_ROWS
        nb = n_rows // EXPERT_ROWS
        lam_init = 0.8 - 0.6 * math.exp(-0.3 * l)
        wl = w_in[l]
        wm = wl[:, 2 * ATTN_W:].astype(BF16)
        wqk = wl[:, :2 * ATTN_W].T.astype(BF16)
        q_gain = jnp.tile(q_norm_g[l], ATTN_W // QK_DIM) * (LOG2E * QK_DIM ** -0.5)
        k_gain = jnp.tile(k_norm_g[l], ATTN_W // QK_DIM)
        gain = jnp.broadcast_to(jnp.concatenate([q_gain, k_gain])[:, None], (2 * ATTN_W, ROW_TILE))
        q, kt, v, cp = _in_call(s_ctx, s_x, la, mod[l], norm1_g[l][None], wm, wqk, gain, cos_t, sin_t)

        lam_p = jnp.zeros((SUBLANES, LANES), F32)
        lam_p = lam_p.at[0, :QK_DIM].set(lambda_q1[l]).at[1, :QK_DIM].set(lambda_k1[l])
        lam_p = lam_p.at[2, :QK_DIM].set(lambda_q2[l]).at[3, :QK_DIM].set(lambda_k2[l])
        sg = jnp.tile(subln_g[l], 2)[None]
        score_bound = jnp.full((1, LANES), QK_DIM * BOUND_SLACK, F32) * (
            jnp.max(jnp.abs(q_gain)) * jnp.max(jnp.abs(k_gain)))
        oa = _attn_call(lam_p, score_bound, q, kt, v, sg, lam_init, first_tile)

        pw = jax.scipy.linalg.block_diag(*[pool_w[l, g] for g in range(len(POOL_WINDOWS))]).astype(BF16)
        wr = jnp.zeros((d, LANES), F32).at[:, :N_EXPERTS].set(router_w[l])
        wrh, wrl = _split_bf16(wr)
        br = jnp.full((1, LANES), NEG_BIG, F32).at[0, :N_EXPERTS].set(router_b[l])
        s, hu, te, tw = _out_call(s_ctx, s_x, oa, cp, conv_w[l], pw, pool_scale[l][None],
                                  w_out[l].astype(BF16), mod[l], norm2_g[l][None], wrh, wrl, br,
                                  seq_len, first_tile)

        dest, counts = _rank_call(te.reshape(t_all, TOP_K))
        cnt = counts[0, :N_EXPERTS]
        padded = (cnt + EXPERT_ROWS - 1) // EXPERT_ROWS * EXPERT_ROWS
        pad_end = jnp.cumsum(padded)
        n_used = (pad_end[-1] // EXPERT_ROWS).astype(I32)
        blk = jnp.minimum(jnp.arange(nb, dtype=I32), n_used - 1) * EXPERT_ROWS
        block_e = jnp.sum((pad_end[None, :] <= blk[:, None]).astype(I32), axis=1)
        block_e = jnp.minimum(block_e, N_EXPERTS - 1)
        dest_flat = dest.reshape(t_all * TOP_K)

        pad0 = jnp.concatenate([pad_end - padded + cnt, pad_end[-1:]])
        npad = jnp.concatenate([padded - cnt, n_rows - pad_end[-1:]])
        xs = _disp_call(pad0, npad, dest_flat, hu.reshape(t_all, SUBLANES, LANES), n_rows)
        ys = _exp_call(block_e, n_used.reshape(1), xs.reshape(n_rows * SUBLANES, LANES), l,
                       w_gate_up, b_gate_up[l], w_down, b_down[l])
        s = _comb_call(dest_flat, s, mod[l], tw, ys.reshape(n_rows, SUBLANES, LANES), first_tile)
        s_ctx = s_x = s
    return s
```

```python
import functools
import math

import jax
import jax.numpy as jnp
from jax import lax
from jax.experimental import pallas as pl
from jax.experimental.pallas import tpu as pltpu

F32 = jnp.float32
BF16 = jnp.bfloat16
I32 = jnp.int32
U32 = jnp.uint32

GRID_W = 64
CTX_LEN = 256
HEAD_DIM = 64
QK_DIM = 32
N_HEADS = 8
ATTN_W = 512
CONV_W = 256
POOL_W = 256
POOL_WINDOWS = (2, 4, 8, 16)
POOL_GC = 64
ROPE_BASE = 10000.0
ROPE_PAIRS = 8
N_EXPERTS = 32
TOP_K = 4
N_MOD = 6
EPS = 1e-6
SWIGLU_ALPHA = 1.702
SWIGLU_LIMIT = 7.0
LOG2E = 1.4426950408889634

LANES = 128
SUBLANES = 8
ROW_TILE = 256
STEP_SAMPLES = 4
N_PAIRS = N_HEADS // 2
POOL_HALO = 8
EXPERT_ROWS = 512
RANK_TILE = 1024
ZERO_RUN = 64
KEY_UNROLL = 16
VMEM_LIMIT = 48 * 1024 * 1024
ATTN_VMEM_LIMIT = 40 * 1024 * 1024
EXPERT_VMEM_LIMIT = 58 * 1024 * 1024
NEG_BIG = -1e30
BOUND_SLACK = 1.0 + 2.0 ** -6
UNDERFLOW_GUARD = 2.0 ** -100


def _split_bf16(a):
    hi = a.astype(BF16)
    lo = (a - hi.astype(F32)).astype(BF16)
    return hi, lo


def _dot(a, b):
    return jnp.dot(a, b, preferred_element_type=F32)


def _mod_kernel(c_ref, w_ref, b_ref, o_ref):
    cv = c_ref[...]
    s = cv * jax.nn.sigmoid(cv)
    s_hi, s_lo = _split_bf16(s)
    w_hi, w_lo = _split_bf16(w_ref[0])
    o_ref[0] = _dot(s_hi, w_hi) + _dot(s_lo, w_hi) + _dot(s_hi, w_lo) + b_ref[0]


def _mod_call(c_all, w_mod, b_mod):
    depth, d, n = w_mod.shape
    r = c_all.shape[0]
    tn = 1536
    return pl.pallas_call(
        _mod_kernel,
        grid=(depth, n // tn),
        in_specs=[
            pl.BlockSpec((r, d), lambda l, j: (0, 0)),
            pl.BlockSpec((1, d, tn), lambda l, j: (l, 0, j)),
            pl.BlockSpec((1, 1, tn), lambda l, j: (l, 0, j)),
        ],
        out_specs=pl.BlockSpec((1, r, tn), lambda l, j: (l, 0, j)),
        out_shape=jax.ShapeDtypeStruct((depth, r, n), F32),
        compiler_params=pltpu.CompilerParams(vmem_limit_bytes=VMEM_LIMIT),
        name="mod",
    )(c_all, w_mod, b_mod.reshape(depth, 1, n))


def _rms_mod(x, g, shift, scale):
    y = x * lax.rsqrt(jnp.mean(x * x, axis=-1, keepdims=True) + EPS) * g
    return y * (1.0 + scale) + shift


def _stream_tile(c_ref, x_ref, j, s=0):
    return jnp.where(j == 0, c_ref[s], x_ref[s])


def _stream_specs(d, first_tile, x_is_latent, samples=1):
    shift = first_tile - (1 if x_is_latent else 0)
    return [pl.BlockSpec((samples, ROW_TILE, d), lambda i, j: (i, 0, 0)),
            pl.BlockSpec((samples, ROW_TILE, d), lambda i, j: (i, jnp.maximum(j + shift, 0), 0))]


def _in_kernel(c_ref, x_ref, mod_ref, g1_ref, wm_ref, wqk_ref, gain_ref, cos_ref, sin_ref,
               q_ref, kt_ref, v_ref, cp_ref):
    for s in range(x_ref.shape[0]):
        _in_one_sample(s, c_ref, x_ref, mod_ref, g1_ref, wm_ref, wqk_ref, gain_ref, cos_ref, sin_ref,
                       q_ref, kt_ref, v_ref, cp_ref)


def _in_one_sample(s, c_ref, x_ref, mod_ref, g1_ref, wm_ref, wqk_ref, gain_ref, cos_ref, sin_ref,
                   q_ref, kt_ref, v_ref, cp_ref):
    tm = x_ref.shape[1]
    xin = _stream_tile(c_ref, x_ref, pl.program_id(1), s)
    h = _rms_mod(xin, g1_ref[...], mod_ref[s, 0:1, :], mod_ref[s, 1:2, :])
    pm = _dot(h.astype(BF16), wm_ref[...])
    for p in range(N_PAIRS):
        v_ref[s, p] = pm[:, p * LANES:(p + 1) * LANES].astype(BF16)
    o = ATTN_W
    cp_ref[s, :, 0:CONV_W] = pm[:, o:o + CONV_W]
    cp_ref[s, :, CONV_W:2 * CONV_W] = pm[:, o + CONV_W:o + 2 * CONV_W] * pm[:, o + 2 * CONV_W:o + 3 * CONV_W]
    cp_ref[s, :, 2 * CONV_W:] = pm[:, o + 3 * CONV_W:]

    ht = h.T.astype(BF16)
    qkt = _dot(wqk_ref[...], ht)
    ng = 2 * ATTN_W // QK_DIM
    t = qkt.reshape(ng, 4, ROPE_PAIRS, tm)
    ss = jnp.sum(jnp.sum(t * t, axis=2, keepdims=True), axis=1, keepdims=True)
    tn = t * lax.rsqrt(ss * (1.0 / QK_DIM) + EPS) * gain_ref[...].reshape(ng, 4, ROPE_PAIRS, tm)
    cs = cos_ref[...].reshape(2, ROPE_PAIRS, tm)
    sn = sin_ref[...].reshape(2, ROPE_PAIRS, tm)
    parts = []
    for a in range(2):
        t1 = tn[:, 2 * a]
        t2 = tn[:, 2 * a + 1]
        parts.append(t1 * cs[a] - t2 * sn[a])
        parts.append(t2 * cs[a] + t1 * sn[a])
    rot = jnp.stack(parts, axis=1).reshape(2 * ATTN_W, tm)
    qt = rot[:ATTN_W].T
    for p in range(N_PAIRS):
        q_ref[s, p] = qt[:, p * LANES:(p + 1) * LANES].astype(BF16)
        kt_ref[s, 0, p] = rot[ATTN_W + p * LANES:ATTN_W + (p + 1) * LANES].astype(BF16)


def _in_call(s_ctx, s_x, la, mod, g1, wm, wqk, gain, cos_t, sin_t):
    b, _, d = s_x.shape
    nt = la // ROW_TILE
    nmain = wm.shape[1]
    sps = STEP_SAMPLES
    assert b % sps == 0

    def mod_idx(i, j):
        return (jnp.where(j == 0, b // sps, i), 0, 0)

    return pl.pallas_call(
        _in_kernel,
        grid=(b // sps, nt),
        in_specs=_stream_specs(d, 0, s_x.shape[1] < la, sps) + [
            pl.BlockSpec((sps, N_MOD, d), mod_idx),
            pl.BlockSpec((1, d), lambda i, j: (0, 0)),
            pl.BlockSpec((d, nmain), lambda i, j: (0, 0)),
            pl.BlockSpec((2 * ATTN_W, d), lambda i, j: (0, 0)),
            pl.BlockSpec((2 * ATTN_W, ROW_TILE), lambda i, j: (0, 0)),
            pl.BlockSpec((2 * ROPE_PAIRS, ROW_TILE), lambda i, j: (0, j)),
            pl.BlockSpec((2 * ROPE_PAIRS, ROW_TILE), lambda i, j: (0, j)),
        ],
        out_specs=[
            pl.BlockSpec((sps, N_PAIRS, ROW_TILE, LANES), lambda i, j: (i, 0, j, 0)),
            pl.BlockSpec((sps, 1, N_PAIRS, LANES, ROW_TILE), lambda i, j: (i, j, 0, 0, 0)),
            pl.BlockSpec((sps, N_PAIRS, ROW_TILE, LANES), lambda i, j: (i, 0, j, 0)),
            pl.BlockSpec((sps, ROW_TILE, 3 * CONV_W), lambda i, j: (i, j, 0)),
        ],
        out_shape=[
            jax.ShapeDtypeStruct((b, N_PAIRS, la, LANES), BF16),
            jax.ShapeDtypeStruct((b, nt, N_PAIRS, LANES, ROW_TILE), BF16),
            jax.ShapeDtypeStruct((b, N_PAIRS, la, LANES), BF16),
            jax.ShapeDtypeStruct((b, la, 3 * CONV_W), F32),
        ],
        compiler_params=pltpu.CompilerParams(vmem_limit_bytes=VMEM_LIMIT),
        name="inproj",
    )(s_ctx, s_x, mod, g1, wm, wqk, gain, cos_t, sin_t)


def _attn_kernel(lam_ref, mb_ref, q_ref, kt_ref, v_ref, sg_ref, o_ref, mb_scr, l_scr, acc_scr, *,
                 lam_init, first_tile):
    tq = q_ref.shape[2]
    nkc_all = kt_ref.shape[1]
    j = pl.program_id(1) + first_tile
    n_groups = jnp.where(j == 0, 0, (nkc_all - 1) // KEY_UNROLL)
    n_chunks = jnp.where(j == 0, 1, nkc_all)
    lp = lam_ref[...]
    lam = (jnp.exp(jnp.sum(lp[0:1] * lp[1:2], axis=-1, keepdims=True))
           - jnp.exp(jnp.sum(lp[2:3] * lp[3:4], axis=-1, keepdims=True)) + lam_init)
    lane = lax.broadcasted_iota(I32, (1, LANES), 1)
    low = lane < HEAD_DIM
    groups = [(lane >= g * QK_DIM) & (lane < (g + 1) * QK_DIM) for g in range(4)]
    bound = mb_ref[...]

    def stacked_queries(p):
        qp = q_ref[0, p]
        return jnp.concatenate([jnp.where(groups[g], qp, jnp.zeros_like(qp)) for g in range(4)], axis=0)

    def exp_pv(p, qs, kc, mb):
        sc = _dot(qs, kt_ref[0, kc, p])
        e0 = jnp.exp2(sc[:, :LANES] - mb)
        e1 = jnp.exp2(sc[:, LANES:] - mb)
        r0 = pl.multiple_of(kc * ROW_TILE, ROW_TILE)
        e = jnp.concatenate([e0.astype(BF16), e1.astype(BF16)], axis=1)
        return e0 + e1, _dot(e, v_ref[0, p, pl.ds(r0, ROW_TILE), :])

    def finish(p, lsum):
        on = acc_scr[...] / lsum
        heads = [on[(2 * hh) * tq:(2 * hh + 1) * tq] - lam * on[(2 * hh + 1) * tq:(2 * hh + 2) * tq]
                 for hh in range(2)]
        o = jnp.where(low, heads[0], heads[1])
        o2 = o * o
        ss_lo = jnp.sum(jnp.where(low, o2, 0.0), axis=-1, keepdims=True)
        ss_hi = jnp.sum(jnp.where(low, 0.0, o2), axis=-1, keepdims=True)
        inv = lax.rsqrt(jnp.where(low, ss_lo, ss_hi) * (1.0 / HEAD_DIM) + EPS)
        o_ref[0, p] = (o * inv * sg_ref[...] * (1.0 - lam_init)).astype(BF16)

    lmin = None
    for p in range(N_PAIRS):
        qs = stacked_queries(p)

        def sweep(g, _, p=p, qs=qs):
            parts = [exp_pv(p, qs, 1 + g * KEY_UNROLL + u, bound) for u in range(KEY_UNROLL)]
            while len(parts) > 1:
                parts = [(a[0] + b[0], a[1] + b[1]) for a, b in zip(parts[::2], parts[1::2])]
            l_scr[...] = l_scr[...] + parts[0][0]
            acc_scr[...] = acc_scr[...] + parts[0][1]
            return 0

        l_scr[...], acc_scr[...] = exp_pv(p, qs, 0, bound)
        lax.fori_loop(0, n_groups, sweep, 0)
        lsum = jnp.sum(l_scr[...], axis=-1, keepdims=True)
        lmin = lsum if lmin is None else jnp.minimum(lmin, lsum)
        finish(p, lsum)

    @pl.when(jnp.logical_not(jnp.min(lmin) >= UNDERFLOW_GUARD))
    def _():
        for p in range(N_PAIRS):
            qs = stacked_queries(p)

            def row_max(kc, m, p=p, qs=qs):
                sc = _dot(qs, kt_ref[0, kc, p])
                return jnp.maximum(m, jnp.maximum(sc[:, :LANES], sc[:, LANES:]))

            m = lax.fori_loop(0, n_chunks, row_max, jnp.full((4 * tq, LANES), -jnp.inf, F32))
            mb_scr[...] = jnp.broadcast_to(jnp.max(m, axis=-1, keepdims=True), mb_scr.shape)
            l_scr[...] = jnp.zeros(l_scr.shape, F32)
            acc_scr[...] = jnp.zeros(acc_scr.shape, F32)

            def redo(kc, _, p=p, qs=qs):
                dl, dacc = exp_pv(p, qs, kc, mb_scr[...])
                l_scr[...] = l_scr[...] + dl
                acc_scr[...] = acc_scr[...] + dacc
                return 0

            lax.fori_loop(0, n_chunks, redo, 0)
            finish(p, jnp.sum(l_scr[...], axis=-1, keepdims=True))


def _attn_call(lam_p, score_bound, q, kt, v, sg, lam_init, first_tile):
    b, _, la, _ = q.shape
    nt = la // ROW_TILE
    assert (nt - 1) % KEY_UNROLL == 0
    off = first_tile
    return pl.pallas_call(
        functools.partial(_attn_kernel, lam_init=lam_init, first_tile=off),
        grid=(b, nt - off),
        in_specs=[
            pl.BlockSpec((SUBLANES, LANES), lambda i, j: (0, 0)),
            pl.BlockSpec((1, LANES), lambda i, j: (0, 0)),
            pl.BlockSpec((1, N_PAIRS, ROW_TILE, LANES), lambda i, j: (i, 0, j + off, 0)),
            pl.BlockSpec((1, nt, N_PAIRS, LANES, ROW_TILE), lambda i, j: (i, 0, 0, 0, 0)),
            pl.BlockSpec((1, N_PAIRS, la, LANES), lambda i, j: (i, 0, 0, 0)),
            pl.BlockSpec((1, LANES), lambda i, j: (0, 0)),
        ],
        out_specs=pl.BlockSpec((1, N_PAIRS, ROW_TILE, LANES), lambda i, j: (i, 0, j, 0)),
        out_shape=jax.ShapeDtypeStruct((b, N_PAIRS, la - off * ROW_TILE, LANES), BF16),
        scratch_shapes=[pltpu.VMEM((4 * ROW_TILE, LANES), F32),
                        pltpu.VMEM((4 * ROW_TILE, LANES), F32),
                        pltpu.VMEM((4 * ROW_TILE, LANES), F32)],
        compiler_params=pltpu.CompilerParams(vmem_limit_bytes=ATTN_VMEM_LIMIT),
        name="attn",
    )(lam_p, score_bound, q, kt, v, sg)


def _store_row_tiles(ref, val):
    rows = val.shape[0]
    for c in range(SUBLANES):
        ref[pl.ds(c, rows, stride=SUBLANES), :] = val[:, c * LANES:(c + 1) * LANES]


def _load_row_tiles(ref, first_row, rows, c):
    return ref[pl.ds(first_row * SUBLANES + c, rows, stride=SUBLANES), :]


def _shift_rows(a, k):
    return pltpu.roll(a, k % a.shape[0], axis=0)


def _out_kernel(*refs, seq_len, first_tile):
    for s in range(refs[1].shape[0]):
        _out_one_sample(s, *refs, seq_len=seq_len, first_tile=first_tile)


def _out_one_sample(s, c_ref, x_ref, oa_ref, cp_ref, cpp_ref, cpn_ref, cw_ref, pw_ref, ps_ref, wo_ref,
                    mod_ref, g2_ref, wrh_ref, wrl_ref, br_ref,
                    xo_ref, hu_ref, te_ref, tw_ref, *, seq_len, first_tile):
    tm = x_ref.shape[1]
    j = pl.program_id(1) + first_tile
    nt = pl.num_programs(1) + first_tile
    halo = POOL_HALO
    has_prev = j >= 2
    has_next = (j >= 1) & (j < nt - 1)
    prev = jnp.where(has_prev, cpp_ref[s], 0.0)
    nxt = jnp.where(has_next, cpn_ref[s], 0.0)
    ext = jnp.concatenate([prev, cp_ref[s], nxt], axis=0)

    z = ext[:, CONV_W:2 * CONV_W]
    conv = (cw_ref[0:1, :] * _shift_rows(z, 1) + cw_ref[1:2, :] * z + cw_ref[2:3, :] * _shift_rows(z, -1))
    o_conv = ext[halo:halo + tm, 0:CONV_W] * conv[halo:halo + tm]

    u = ext[:, 2 * CONV_W:]
    a2 = u + _shift_rows(u, 1)
    a4 = _shift_rows(a2, -1) + _shift_rows(a2, 1)
    a8 = _shift_rows(a4, -2) + _shift_rows(a4, 2)
    a16 = _shift_rows(a8, -4) + _shift_rows(a8, 4)
    lane = lax.broadcasted_iota(I32, (tm, POOL_W), 1)
    grp = lane // POOL_GC
    wsum = jnp.where(grp == 0, a2[halo:halo + tm],
                     jnp.where(grp == 1, a4[halo:halo + tm],
                               jnp.where(grp == 2, a8[halo:halo + tm], a16[halo:halo + tm])))
    half = jnp.where(grp == 0, 1, jnp.where(grp == 1, 2, jnp.where(grp == 2, 4, 8)))
    row = lax.broadcasted_iota(I32, (tm, POOL_W), 0)
    pos = jnp.where(j == 0, row, (j - 1) * tm + row)
    n_seq = jnp.where(j == 0, CTX_LEN, seq_len)
    cnt = jnp.minimum(pos + half, n_seq) - jnp.maximum(pos - half, 0)
    dlt = wsum / cnt.astype(F32) - u[halo:halo + tm]
    o_pool = _dot(dlt.astype(BF16), pw_ref[...]) * ps_ref[...]

    mix = _dot(o_conv.astype(BF16), wo_ref[ATTN_W:ATTN_W + CONV_W, :])
    mix = mix + _dot(o_pool.astype(BF16), wo_ref[ATTN_W + CONV_W:, :])
    for p in range(N_PAIRS):
        mix = mix + _dot(oa_ref[s, p], wo_ref[p * LANES:(p + 1) * LANES, :])
    x = _stream_tile(c_ref, x_ref, j, s) + mod_ref[s, 2:3, :] * mix
    xo_ref[s] = x

    h2 = _rms_mod(x, g2_ref[...], mod_ref[s, 3:4, :], mod_ref[s, 4:5, :])
    _store_row_tiles(hu_ref.at[s], h2)

    h_hi, h_lo = _split_bf16(h2)
    logits = (_dot(h_hi, wrh_ref[...]) + _dot(h_lo, wrh_ref[...]) + _dot(h_hi, wrl_ref[...])
              + br_ref[...])
    lanef = lax.broadcasted_iota(I32, (tm, LANES), 1).astype(F32)
    work = logits
    tops = []
    for k in range(TOP_K):
        mk = jnp.max(work, axis=-1, keepdims=True)
        ik = jnp.min(jnp.where(work == mk, lanef, float(LANES)), axis=-1, keepdims=True)
        te_ref[s, :, k:k + 1] = ik.astype(I32)
        work = jnp.where(lanef == ik, -jnp.inf, work)
        tops.append(mk)
    es = [jnp.exp(mk - tops[0]) for mk in tops]
    den = es[0] + es[1] + es[2] + es[3]
    for k in range(TOP_K):
        tw_ref[s, :, k:k + 1] = es[k] / den


def _out_call(s_ctx, s_x, oa, cp, cw, pw, ps, wo, mod, g2, wrh, wrl, br, seq_len, first_tile):
    b, _, d = s_x.shape
    la = CTX_LEN + seq_len
    nt = la // ROW_TILE
    off = first_tile
    lo = la - off * ROW_TILE
    hb = ROW_TILE // POOL_HALO
    nhb = la // POOL_HALO
    sps = STEP_SAMPLES
    assert b % sps == 0

    def mod_idx(i, j):
        return (jnp.where(j + off == 0, b // sps, i), 0, 0)

    const2 = lambda i, j: (0, 0)
    return pl.pallas_call(
        functools.partial(_out_kernel, seq_len=seq_len, first_tile=off),
        grid=(b // sps, nt - off),
        in_specs=_stream_specs(d, off, s_x.shape[1] < la, sps) + [
            pl.BlockSpec((sps, N_PAIRS, ROW_TILE, LANES), lambda i, j: (i, 0, j, 0)),
            pl.BlockSpec((sps, ROW_TILE, 3 * CONV_W), lambda i, j: (i, j + off, 0)),
            pl.BlockSpec((sps, POOL_HALO, 3 * CONV_W), lambda i, j: (i, jnp.maximum((j + off) * hb - 1, 0), 0)),
            pl.BlockSpec((sps, POOL_HALO, 3 * CONV_W),
                         lambda i, j: (i, jnp.minimum((j + off + 1) * hb, nhb - 1), 0)),
            pl.BlockSpec((3, CONV_W), const2),
            pl.BlockSpec((POOL_W, POOL_W), const2),
            pl.BlockSpec((1, POOL_W), const2),
            pl.BlockSpec((d, d), const2),
            pl.BlockSpec((sps, N_MOD, d), mod_idx),
            pl.BlockSpec((1, d), const2),
            pl.BlockSpec((d, LANES), const2),
            pl.BlockSpec((d, LANES), const2),
            pl.BlockSpec((1, LANES), const2),
        ],
        out_specs=[
            pl.BlockSpec((sps, ROW_TILE, d), lambda i, j: (i, j, 0)),
            pl.BlockSpec((sps, ROW_TILE * SUBLANES, LANES), lambda i, j: (i, j, 0)),
            pl.BlockSpec((sps, ROW_TILE, TOP_K), lambda i, j: (i, j, 0)),
            pl.BlockSpec((sps, ROW_TILE, TOP_K), lambda i, j: (i, j, 0)),
        ],
        out_shape=[
            jax.ShapeDtypeStruct((b, lo, d), F32),
            jax.ShapeDtypeStruct((b, lo * SUBLANES, LANES), F32),
            jax.ShapeDtypeStruct((b, lo, TOP_K), I32),
            jax.ShapeDtypeStruct((b, lo, TOP_K), F32),
        ],
        compiler_params=pltpu.CompilerParams(vmem_limit_bytes=VMEM_LIMIT),
        name="outproj",
    )(s_ctx, s_x, oa, cp, cp, cp, cw, pw, ps, wo, mod, g2, wrh, wrl, br)


def _rank_kernel(te_ref, tri_ref, dest_ref, cnt_ref, carry):
    ph = pl.program_id(0)
    i = pl.program_id(1)
    tr = te_ref.shape[0]
    lane1 = lax.broadcasted_iota(I32, (SUBLANES, LANES), 1)

    @pl.when((ph == 0) & (i == 0))
    def _():
        carry[...] = jnp.zeros_like(carry)

    @pl.when((ph == 1) & (i == 0))
    def _():
        cnt = carry[...]
        cnt_ref[...] = cnt.astype(I32)
        padded = jnp.ceil(cnt * (1.0 / EXPERT_ROWS)) * EXPERT_ROWS
        incl = padded
        for sh in (1, 2, 4, 8, 16):
            incl = incl + jnp.where(lane1 >= sh, pltpu.roll(incl, sh, axis=1), 0.0)
        carry[...] = incl - padded

    e = te_ref[...]
    lane = lax.broadcasted_iota(I32, (tr, LANES), 1)
    ohs = [(e[:, k:k + 1] == lane).astype(F32) for k in range(TOP_K)]
    m = ohs[0] + ohs[1] + ohs[2] + ohs[3]

    @pl.when(ph == 0)
    def _():
        dest_ref[...] = jnp.zeros(dest_ref.shape, I32)

    @pl.when(ph == 1)
    def _():
        base = _dot(tri_ref[...], m.astype(BF16)) + carry[0:1, :]
        for k in range(TOP_K):
            dest_ref[:, k:k + 1] = jnp.sum(ohs[k] * base, axis=-1, keepdims=True).astype(I32)

    carry[...] = carry[...] + jnp.sum(m, axis=0, keepdims=True)


def _rank_call(te):
    t = te.shape[0]
    nt = t // RANK_TILE
    r = lax.broadcasted_iota(I32, (RANK_TILE, RANK_TILE), 0)
    c = lax.broadcasted_iota(I32, (RANK_TILE, RANK_TILE), 1)
    tri = (c < r).astype(BF16)
    return pl.pallas_call(
        _rank_kernel,
        grid=(2, nt),
        in_specs=[
            pl.BlockSpec((RANK_TILE, TOP_K), lambda ph, i: (i, 0)),
            pl.BlockSpec((RANK_TILE, RANK_TILE), lambda ph, i: (0, 0)),
        ],
        out_specs=[
            pl.BlockSpec((RANK_TILE, TOP_K), lambda ph, i: (ph * i, 0)),
            pl.BlockSpec((SUBLANES, LANES), lambda ph, i: (0, 0)),
        ],
        out_shape=[
            jax.ShapeDtypeStruct((t, TOP_K), I32),
            jax.ShapeDtypeStruct((SUBLANES, LANES), I32),
        ],
        scratch_shapes=[pltpu.VMEM((SUBLANES, LANES), F32)],
        compiler_params=pltpu.CompilerParams(dimension_semantics=("arbitrary", "arbitrary")),
        name="rank",
    )(te, tri)


def _disp_kernel(pad0_ref, npad_ref, dest_ref, hu_ref, xs_ref, zero, sem, zsem):
    def row_copy(r, k):
        return pltpu.make_async_copy(hu_ref.at[r], xs_ref.at[dest_ref[r * TOP_K + k]], sem)

    def start(r, _):
        for k in range(TOP_K):
            row_copy(r, k).start(priority=k % 2)
        return 0

    lax.fori_loop(0, ROW_TILE, start, 0)
    for _ in range(TOP_K):
        pltpu.make_async_copy(hu_ref, hu_ref, sem).wait()

    @pl.when(pl.program_id(0) == pl.num_programs(0) - 1)
    def _():
        zero[...] = jnp.zeros(zero.shape, F32)
        run = zero.shape[0]

        def zero_range(e, _):
            n_runs = npad_ref[e] // run
            n_rows = npad_ref[e] - n_runs * run
            row0 = pad0_ref[e] + n_runs * run

            def run_copy(c):
                return pltpu.make_async_copy(zero, xs_ref.at[pl.ds(pad0_ref[e] + c * run, run)], zsem)

            def row_copy0(r):
                return pltpu.make_async_copy(zero.at[0], xs_ref.at[row0 + r], zsem)

            def loop(n, make, wait):
                def body(c, carry):
                    if wait:
                        make(c).wait()
                    else:
                        make(c).start()
                    return carry
                lax.fori_loop(0, n, body, 0)

            loop(n_runs, run_copy, False)
            loop(n_rows, row_copy0, False)
            loop(n_runs, run_copy, True)
            loop(n_rows, row_copy0, True)
            return 0

        lax.fori_loop(0, pad0_ref.shape[0], zero_range, 0)


def _disp_call(pad0, npad, dest_flat, hu, n_rows):
    t = hu.shape[0]
    grid_spec = pltpu.PrefetchScalarGridSpec(
        num_scalar_prefetch=2,
        grid=(t // ROW_TILE,),
        in_specs=[
            pl.BlockSpec((ROW_TILE * TOP_K,), lambda i, p0, n: (i,), memory_space=pltpu.SMEM),
            pl.BlockSpec((ROW_TILE, SUBLANES, LANES), lambda i, p0, n: (i, 0, 0)),
        ],
        out_specs=pl.BlockSpec(memory_space=pl.ANY),
        scratch_shapes=[pltpu.VMEM((ZERO_RUN, SUBLANES, LANES), F32), pltpu.SemaphoreType.DMA(()),
                        pltpu.SemaphoreType.DMA(())],
    )
    return pl.pallas_call(
        _disp_kernel,
        grid_spec=grid_spec,
        out_shape=jax.ShapeDtypeStruct((n_rows, SUBLANES, LANES), F32),
        compiler_params=pltpu.CompilerParams(dimension_semantics=("arbitrary",)),
        name="dispatch",
    )(pad0, npad, dest_flat, hu)


def _exp_kernel(be_ref, nu_ref, xs_ref, wgu_ref, bgu_ref, wdn_ref, bdn_ref, ys_ref, wgu_b, wdn_b):
    i = pl.program_id(0)
    ff = wdn_ref.shape[1]
    rows = xs_ref.shape[0] // SUBLANES

    @pl.when((i == 0) | (be_ref[i] != be_ref[jnp.maximum(i - 1, 0)]))
    def _():
        wgu_b[...] = wgu_ref[0].astype(BF16)
        wdn_b[...] = wdn_ref[0].astype(BF16)

    @pl.when(i < nu_ref[0])
    def _():
        xb = jnp.concatenate([_load_row_tiles(xs_ref, 0, rows, c).astype(BF16) for c in range(SUBLANES)],
                             axis=1)
        gu = _dot(xb, wgu_b[...]) + bgu_ref[0]
        glu = jnp.minimum(gu[:, :ff], SWIGLU_LIMIT)
        lin = jnp.clip(gu[:, ff:], -SWIGLU_LIMIT, SWIGLU_LIMIT)
        act = glu * jax.nn.sigmoid(SWIGLU_ALPHA * glu) * (lin + 1.0)
        _store_row_tiles(ys_ref, _dot(act.astype(BF16), wdn_b[...]) + bdn_ref[0])

    @pl.when(i >= nu_ref[0])
    def _():
        ys_ref[...] = jnp.zeros_like(ys_ref)


def _exp_call(block_e, n_used, xs, layer, wgu, bgu, wdn, bdn):
    n_rows = xs.shape[0] // SUBLANES
    _, _, d, ff2 = wgu.shape
    ff = ff2 // 2
    nb = n_rows // EXPERT_ROWS
    grid_spec = pltpu.PrefetchScalarGridSpec(
        num_scalar_prefetch=2,
        grid=(nb,),
        in_specs=[
            pl.BlockSpec((EXPERT_ROWS * SUBLANES, LANES), lambda i, be, nu: (jnp.minimum(i, nu[0] - 1), 0)),
            pl.BlockSpec((None, 1, d, ff2), lambda i, be, nu: (layer, be[i], 0, 0)),
            pl.BlockSpec((1, 1, ff2), lambda i, be, nu: (be[i], 0, 0)),
            pl.BlockSpec((None, 1, ff, d), lambda i, be, nu: (layer, be[i], 0, 0)),
            pl.BlockSpec((1, 1, d), lambda i, be, nu: (be[i], 0, 0)),
        ],
        out_specs=pl.BlockSpec((EXPERT_ROWS * SUBLANES, LANES), lambda i, be, nu: (i, 0)),
        scratch_shapes=[pltpu.VMEM((d, ff2), BF16), pltpu.VMEM((ff, d), BF16)],
    )
    return pl.pallas_call(
        _exp_kernel,
        grid_spec=grid_spec,
        out_shape=jax.ShapeDtypeStruct((n_rows * SUBLANES, LANES), F32),
        compiler_params=pltpu.CompilerParams(dimension_semantics=("arbitrary",),
                                             vmem_limit_bytes=EXPERT_VMEM_LIMIT),
        name="experts",
    )(block_e, n_used, xs, wgu, bgu.reshape(-1, 1, ff2), wdn, bdn.reshape(-1, 1, d))


def _comb_kernel(dest_ref, x_ref, mod_ref, tw_ref, ys_ref, o_ref, buf, sem):
    def row_copy(r, k):
        r0 = pl.multiple_of((k * ROW_TILE + r) * SUBLANES, SUBLANES)
        return pltpu.make_async_copy(ys_ref.at[dest_ref[r * TOP_K + k]], buf.at[pl.ds(r0, SUBLANES)], sem)

    def start(r, _):
        for k in range(TOP_K):
            row_copy(r, k).start(priority=k % 2)
        return 0

    lax.fori_loop(0, ROW_TILE, start, 0)
    pltpu.make_async_copy(buf, buf, sem).wait()
    w = [tw_ref[0, :, k:k + 1] for k in range(TOP_K)]
    for c in range(SUBLANES):
        acc = _load_row_tiles(buf, 0, ROW_TILE, c) * w[0]
        for k in range(1, TOP_K):
            acc = acc + _load_row_tiles(buf, k * ROW_TILE, ROW_TILE, c) * w[k]
        cols = slice(c * LANES, (c + 1) * LANES)
        o_ref[0, :, cols] = x_ref[0, :, cols] + mod_ref[0, 5:6, cols] * acc


def _comb_call(dest_flat, s, mod, tw, ys, first_tile):
    b, rows, d = s.shape
    nt = rows // ROW_TILE

    def mod_idx(i, j):
        return (jnp.where(j + first_tile == 0, b, i), 0, 0)

    return pl.pallas_call(
        _comb_kernel,
        grid=(b, nt),
        in_specs=[
            pl.BlockSpec((ROW_TILE * TOP_K,), lambda i, j: (i * nt + j,), memory_space=pltpu.SMEM),
            pl.BlockSpec((1, ROW_TILE, d), lambda i, j: (i, j, 0)),
            pl.BlockSpec((1, N_MOD, d), mod_idx),
            pl.BlockSpec((1, ROW_TILE, TOP_K), lambda i, j: (i, j, 0)),
            pl.BlockSpec(memory_space=pl.ANY),
        ],
        out_specs=pl.BlockSpec((1, ROW_TILE, d), lambda i, j: (i, j, 0)),
        out_shape=jax.ShapeDtypeStruct((b, rows, d), F32),
        scratch_shapes=[pltpu.VMEM((TOP_K * ROW_TILE * SUBLANES, LANES), F32), pltpu.SemaphoreType.DMA(())],
        compiler_params=pltpu.CompilerParams(dimension_semantics=("arbitrary", "arbitrary"),
                                             vmem_limit_bytes=VMEM_LIMIT),
        name="combine",
    )(dest_flat, s, mod, tw, ys)


def _rope_tables(seq_len):
    rows = seq_len // GRID_W
    row = jnp.broadcast_to(jnp.arange(rows, dtype=F32)[:, None], (rows, GRID_W)).reshape(seq_len)
    col = jnp.broadcast_to(jnp.arange(GRID_W, dtype=F32)[None, :], (rows, GRID_W)).reshape(seq_len)
    inv = ROPE_BASE ** (-jnp.arange(ROPE_PAIRS, dtype=F32) / ROPE_PAIRS)
    ang = jnp.stack([row, col], axis=0)[:, None, :] * inv[None, :, None]
    cos_l = jnp.cos(ang).reshape(2 * ROPE_PAIRS, seq_len)
    sin_l = jnp.sin(ang).reshape(2 * ROPE_PAIRS, seq_len)
    cos_t = jnp.concatenate([jnp.ones((2 * ROPE_PAIRS, CTX_LEN), F32), cos_l], axis=1)
    sin_t = jnp.concatenate([jnp.zeros((2 * ROPE_PAIRS, CTX_LEN), F32), sin_l], axis=1)
    return cos_t, sin_t


def kernel(x, c, ctx, c_ctx, w_mod, b_mod, norm1_g, norm2_g, w_in, q_norm_g, k_norm_g,
           lambda_q1, lambda_k1, lambda_q2, lambda_k2, subln_g, conv_w, pool_w, pool_scale, w_out,
           router_w, router_b, w_gate_up, b_gate_up, w_down, b_down):
    b, seq_len, d = x.shape
    assert d == SUBLANES * LANES
    depth = w_mod.shape[0]
    la = CTX_LEN + seq_len

    rows = -(-(b + STEP_SAMPLES) // SUBLANES) * SUBLANES
    c_all = jnp.zeros((rows, d), F32).at[:b].set(c).at[b:b + STEP_SAMPLES].set(c_ctx)
    mod = _mod_call(c_all, w_mod, b_mod).reshape(depth, rows, N_MOD, d)

    cos_t, sin_t = _rope_tables(seq_len)
    s_ctx, s_x = ctx, x

    for l in range(depth):
        first_tile = 1 if l + 1 == depth else 0
        t_all = b * (la - first_tile * ROW_TILE)
        n_rows = -(-(t_all * TOP_K + N_EXPERTS * (EXPERT_ROWS - 1)) // EXPERT_ROWS) * EXPERT_ROWS
        nb = n_rows // EXPERT_ROWS
        lam_init = 0.8 - 0.6 * math.exp(-0.3 * l)
        wl = w_in[l]
        wm = wl[:, 2 * ATTN_W:].astype(BF16)
        wqk = wl[:, :2 * ATTN_W].T.astype(BF16)
        q_gain = jnp.tile(q_norm_g[l], ATTN_W // QK_DIM) * (LOG2E * QK_DIM ** -0.5)
        k_gain = jnp.tile(k_norm_g[l], ATTN_W // QK_DIM)
        gain = jnp.broadcast_to(jnp.concatenate([q_gain, k_gain])[:, None], (2 * ATTN_W, ROW_TILE))
        q, kt, v, cp = _in_call(s_ctx, s_x, la, mod[l], norm1_g[l][None], wm, wqk, gain, cos_t, sin_t)

        lam_p = jnp.zeros((SUBLANES, LANES), F32)
        lam_p = lam_p.at[0, :QK_DIM].set(lambda_q1[l]).at[1, :QK_DIM].set(lambda_k1[l])
        lam_p = lam_p.at[2, :QK_DIM].set(lambda_q2[l]).at[3, :QK_DIM].set(lambda_k2[l])
        sg = jnp.tile(subln_g[l], 2)[None]
        score_bound = jnp.full((1, LANES), QK_DIM * BOUND_SLACK, F32) * (
            jnp.max(jnp.abs(q_gain)) * jnp.max(jnp.abs(k_gain)))
        oa = _attn_call(lam_p, score_bound, q, kt, v, sg, lam_init, first_tile)

        pw = jax.scipy.linalg.block_diag(*[pool_w[l, g] for g in range(len(POOL_WINDOWS))]).astype(BF16)
        wr = jnp.zeros((d, LANES), F32).at[:, :N_EXPERTS].set(router_w[l])
        wrh, wrl = _split_bf16(wr)
        br = jnp.full((1, LANES), NEG_BIG, F32).at[0, :N_EXPERTS].set(router_b[l])
        s, hu, te, tw = _out_call(s_ctx, s_x, oa, cp, conv_w[l], pw, pool_scale[l][None],
                                  w_out[l].astype(BF16), mod[l], norm2_g[l][None], wrh, wrl, br,
                                  seq_len, first_tile)

        dest, counts = _rank_call(te.reshape(t_all, TOP_K))
        cnt = counts[0, :N_EXPERTS]
        padded = (cnt + EXPERT_ROWS - 1) // EXPERT_ROWS * EXPERT_ROWS
        pad_end = jnp.cumsum(padded)
        n_used = (pad_end[-1] // EXPERT_ROWS).astype(I32)
        blk = jnp.minimum(jnp.arange(nb, dtype=I32), n_used - 1) * EXPERT_ROWS
        block_e = jnp.sum((pad_end[None, :] <= blk[:, None]).astype(I32), axis=1)
        block_e = jnp.minimum(block_e, N_EXPERTS - 1)
        dest_flat = dest.reshape(t_all * TOP_K)

        pad0 = jnp.concatenate([pad_end - padded + cnt, pad_end[-1:]])
        npad = jnp.concatenate([padded - cnt, n_rows - pad_end[-1:]])
        xs = _disp_call(pad0, npad, dest_flat, hu.reshape(t_all, SUBLANES, LANES), n_rows)
        ys = _exp_call(block_e, n_used.reshape(1), xs.reshape(n_rows * SUBLANES, LANES), l,
                       w_gate_up, b_gate_up[l], w_down, b_down[l])
        s = _comb_call(dest_flat, s, mod[l], tw, ys.reshape(n_rows, SUBLANES, LANES), first_tile)
        s_ctx = s_x = s
    return s
```

```python
import functools
import math

import jax
import jax.numpy as jnp
from jax import lax
from jax.experimental import pallas as pl
from jax.experimental.pallas import tpu as pltpu

F32 = jnp.float32
BF16 = jnp.bfloat16
I32 = jnp.int32
U32 = jnp.uint32

GRID_W = 64
CTX_LEN = 256
HEAD_DIM = 64
QK_DIM = 32
N_HEADS = 8
ATTN_W = 512
CONV_W = 256
POOL_W = 256
POOL_WINDOWS = (2, 4, 8, 16)
POOL_GC = 64
ROPE_BASE = 10000.0
ROPE_PAIRS = 8
N_EXPERTS = 32
TOP_K = 4
N_MOD = 6
EPS = 1e-6
SWIGLU_ALPHA = 1.702
SWIGLU_LIMIT = 7.0
LOG2E = 1.4426950408889634

LANES = 128
SUBLANES = 8
ROW_TILE = 256
STEP_SAMPLES = 4
N_PAIRS = N_HEADS // 2
POOL_HALO = 8
EXPERT_ROWS = 512
RANK_TILE = 1024
ZERO_RUN = 64
KEY_UNROLL = 16
VMEM_LIMIT = 48 * 1024 * 1024
ATTN_VMEM_LIMIT = 40 * 1024 * 1024
EXPERT_VMEM_LIMIT = 58 * 1024 * 1024
NEG_BIG = -1e30
BOUND_SLACK = 1.0 + 2.0 ** -6
UNDERFLOW_GUARD = 2.0 ** -100


def _split_bf16(a):
    hi = a.astype(BF16)
    lo = (a - hi.astype(F32)).astype(BF16)
    return hi, lo


def _dot(a, b):
    return jnp.dot(a, b, preferred_element_type=F32)


def _mod_kernel(c_ref, w_ref, b_ref, o_ref):
    cv = c_ref[...]
    s = cv * jax.nn.sigmoid(cv)
    s_hi, s_lo = _split_bf16(s)
    w_hi, w_lo = _split_bf16(w_ref[0])
    o_ref[0] = _dot(s_hi, w_hi) + _dot(s_lo, w_hi) + _dot(s_hi, w_lo) + b_ref[0]


def _mod_call(c_all, w_mod, b_mod):
    depth, d, n = w_mod.shape
    r = c_all.shape[0]
    tn = 1536
    return pl.pallas_call(
        _mod_kernel,
        grid=(depth, n // tn),
        in_specs=[
            pl.BlockSpec((r, d), lambda l, j: (0, 0)),
            pl.BlockSpec((1, d, tn), lambda l, j: (l, 0, j)),
            pl.BlockSpec((1, 1, tn), lambda l, j: (l, 0, j)),
        ],
        out_specs=pl.BlockSpec((1, r, tn), lambda l, j: (l, 0, j)),
        out_shape=jax.ShapeDtypeStruct((depth, r, n), F32),
        compiler_params=pltpu.CompilerParams(vmem_limit_bytes=VMEM_LIMIT),
        name="mod",
    )(c_all, w_mod, b_mod.reshape(depth, 1, n))


def _rms_mod(x, g, shift, scale):
    y = x * lax.rsqrt(jnp.mean(x * x, axis=-1, keepdims=True) + EPS) * g
    return y * (1.0 + scale) + shift


def _stream_tile(c_ref, x_ref, j, s=0):
    return jnp.where(j == 0, c_ref[s], x_ref[s])


def _stream_specs(d, first_tile, x_is_latent, samples=1):
    shift = first_tile - (1 if x_is_latent else 0)
    return [pl.BlockSpec((samples, ROW_TILE, d), lambda i, j: (i, 0, 0)),
            pl.BlockSpec((samples, ROW_TILE, d), lambda i, j: (i, jnp.maximum(j + shift, 0), 0))]


def _in_kernel(c_ref, x_ref, mod_ref, g1_ref, wm_ref, wqk_ref, gain_ref, cos_ref, sin_ref,
               q_ref, kt_ref, v_ref, cp_ref):
    for s in range(x_ref.shape[0]):
        _in_one_sample(s, c_ref, x_ref, mod_ref, g1_ref, wm_ref, wqk_ref, gain_ref, cos_ref, sin_ref,
                       q_ref, kt_ref, v_ref, cp_ref)


def _in_one_sample(s, c_ref, x_ref, mod_ref, g1_ref, wm_ref, wqk_ref, gain_ref, cos_ref, sin_ref,
                   q_ref, kt_ref, v_ref, cp_ref):
    tm = x_ref.shape[1]
    xin = _stream_tile(c_ref, x_ref, pl.program_id(1), s)
    h = _rms_mod(xin, g1_ref[...], mod_ref[s, 0:1, :], mod_ref[s, 1:2, :])
    pm = _dot(h.astype(BF16), wm_ref[...])
    for p in range(N_PAIRS):
        v_ref[s, p] = pm[:, p * LANES:(p + 1) * LANES].astype(BF16)
    o = ATTN_W
    cp_ref[s, :, 0:CONV_W] = pm[:, o:o + CONV_W]
    cp_ref[s, :, CONV_W:2 * CONV_W] = pm[:, o + CONV_W:o + 2 * CONV_W] * pm[:, o + 2 * CONV_W:o + 3 * CONV_W]
    cp_ref[s, :, 2 * CONV_W:] = pm[:, o + 3 * CONV_W:]

    ht = h.T.astype(BF16)
    qkt = _dot(wqk_ref[...], ht)
    ng = 2 * ATTN_W // QK_DIM
    t = qkt.reshape(ng, 4, ROPE_PAIRS, tm)
    ss = jnp.sum(jnp.sum(t * t, axis=2, keepdims=True), axis=1, keepdims=True)
    tn = t * lax.rsqrt(ss * (1.0 / QK_DIM) + EPS) * gain_ref[...].reshape(ng, 4, ROPE_PAIRS, tm)
    cs = cos_ref[...].reshape(2, ROPE_PAIRS, tm)
    sn = sin_ref[...].reshape(2, ROPE_PAIRS, tm)
    parts = []
    for a in range(2):
        t1 = tn[:, 2 * a]
        t2 = tn[:, 2 * a + 1]
        parts.append(t1 * cs[a] - t2 * sn[a])
        parts.append(t2 * cs[a] + t1 * sn[a])
    rot = jnp.stack(parts, axis=1).reshape(2 * ATTN_W, tm)
    qt = rot[:ATTN_W].T
    for p in range(N_PAIRS):
        q_ref[s, p] = qt[:, p * LANES:(p + 1) * LANES].astype(BF16)
        kt_ref[s, 0, p] = rot[ATTN_W + p * LANES:ATTN_W + (p + 1) * LANES].astype(BF16)


def _in_call(s_ctx, s_x, la, mod, g1, wm, wqk, gain, cos_t, sin_t):
    b, _, d = s_x.shape
    nt = la // ROW_TILE
    nmain = wm.shape[1]
    sps = STEP_SAMPLES
    assert b % sps == 0

    def mod_idx(i, j):
        return (jnp.where(j == 0, b // sps, i), 0, 0)

    return pl.pallas_call(
        _in_kernel,
        grid=(b // sps, nt),
        in_specs=_stream_specs(d, 0, s_x.shape[1] < la, sps) + [
            pl.BlockSpec((sps, N_MOD, d), mod_idx),
            pl.BlockSpec((1, d), lambda i, j: (0, 0)),
            pl.BlockSpec((d, nmain), lambda i, j: (0, 0)),
            pl.BlockSpec((2 * ATTN_W, d), lambda i, j: (0, 0)),
            pl.BlockSpec((2 * ATTN_W, ROW_TILE), lambda i, j: (0, 0)),
            pl.BlockSpec((2 * ROPE_PAIRS, ROW_TILE), lambda i, j: (0, j)),
            pl.BlockSpec((2 * ROPE_PAIRS, ROW_TILE), lambda i, j: (0, j)),
        ],
        out_specs=[
            pl.BlockSpec((sps, N_PAIRS, ROW_TILE, LANES), lambda i, j: (i, 0, j, 0)),
            pl.BlockSpec((sps, 1, N_PAIRS, LANES, ROW_TILE), lambda i, j: (i, j, 0, 0, 0)),
            pl.BlockSpec((sps, N_PAIRS, ROW_TILE, LANES), lambda i, j: (i, 0, j, 0)),
            pl.BlockSpec((sps, ROW_TILE, 3 * CONV_W), lambda i, j: (i, j, 0)),
        ],
        out_shape=[
            jax.ShapeDtypeStruct((b, N_PAIRS, la, LANES), BF16),
            jax.ShapeDtypeStruct((b, nt, N_PAIRS, LANES, ROW_TILE), BF16),
            jax.ShapeDtypeStruct((b, N_PAIRS, la, LANES), BF16),
            jax.ShapeDtypeStruct((b, la, 3 * CONV_W), F32),
        ],
        compiler_params=pltpu.CompilerParams(vmem_limit_bytes=VMEM_LIMIT),
        name="inproj",
    )(s_ctx, s_x, mod, g1, wm, wqk, gain, cos_t, sin_t)


def _attn_kernel(lam_ref, mb_ref, q_ref, kt_ref, v_ref, sg_ref, o_ref, mb_scr, l_scr, acc_scr, *,
                 lam_init, first_tile):
    tq = q_ref.shape[2]
    nkc_all = kt_ref.shape[1]
    j = pl.program_id(1) + first_tile
    n_groups = jnp.where(j == 0, 0, (nkc_all - 1) // KEY_UNROLL)
    n_chunks = jnp.where(j == 0, 1, nkc_all)
    lp = lam_ref[...]
    lam = (jnp.exp(jnp.sum(lp[0:1] * lp[1:2], axis=-1, keepdims=True))
           - jnp.exp(jnp.sum(lp[2:3] * lp[3:4], axis=-1, keepdims=True)) + lam_init)
    lane = lax.broadcasted_iota(I32, (1, LANES), 1)
    low = lane < HEAD_DIM
    groups = [(lane >= g * QK_DIM) & (lane < (g + 1) * QK_DIM) for g in range(4)]
    bound = mb_ref[...]

    def stacked_queries(p):
        qp = q_ref[0, p]
        return jnp.concatenate([jnp.where(groups[g], qp, jnp.zeros_like(qp)) for g in range(4)], axis=0)

    def exp_pv(p, qs, kc, mb):
        sc = _dot(qs, kt_ref[0, kc, p])
        e0 = jnp.exp2(sc[:, :LANES] - mb)
        e1 = jnp.exp2(sc[:, LANES:] - mb)
        r0 = pl.multiple_of(kc * ROW_TILE, ROW_TILE)
        e = jnp.concatenate([e0.astype(BF16), e1.astype(BF16)], axis=1)
        return e0 + e1, _dot(e, v_ref[0, p, pl.ds(r0, ROW_TILE), :])

    def finish(p, lsum):
        on = acc_scr[...] / lsum
        heads = [on[(2 * hh) * tq:(2 * hh + 1) * tq] - lam * on[(2 * hh + 1) * tq:(2 * hh + 2) * tq]
                 for hh in range(2)]
        o = jnp.where(low, heads[0], heads[1])
        o2 = o * o
        ss_lo = jnp.sum(jnp.where(low, o2, 0.0), axis=-1, keepdims=True)
        ss_hi = jnp.sum(jnp.where(low, 0.0, o2), axis=-1, keepdims=True)
        inv = lax.rsqrt(jnp.where(low, ss_lo, ss_hi) * (1.0 / HEAD_DIM) + EPS)
        o_ref[0, p] = (o * inv * sg_ref[...] * (1.0 - lam_init)).astype(BF16)

    lmin = None
    for p in range(N_PAIRS):
        qs = stacked_queries(p)

        def sweep(g, _, p=p, qs=qs):
            parts = [exp_pv(p, qs, 1 + g * KEY_UNROLL + u, bound) for u in range(KEY_UNROLL)]
            while len(parts) > 1:
                parts = [(a[0] + b[0], a[1] + b[1]) for a, b in zip(parts[::2], parts[1::2])]
            l_scr[...] = l_scr[...] + parts[0][0]
            acc_scr[...] = acc_scr[...] + parts[0][1]
            return 0

        l_scr[...], acc_scr[...] = exp_pv(p, qs, 0, bound)
        lax.fori_loop(0, n_groups, sweep, 0)
        lsum = jnp.sum(l_scr[...], axis=-1, keepdims=True)
        lmin = lsum if lmin is None else jnp.minimum(lmin, lsum)
        finish(p, lsum)

    @pl.when(jnp.logical_not(jnp.min(lmin) >= UNDERFLOW_GUARD))
    def _():
        for p in range(N_PAIRS):
            qs = stacked_queries(p)

            def row_max(kc, m, p=p, qs=qs):
                sc = _dot(qs, kt_ref[0, kc, p])
                return jnp.maximum(m, jnp.maximum(sc[:, :LANES], sc[:, LANES:]))

            m = lax.fori_loop(0, n_chunks, row_max, jnp.full((4 * tq, LANES), -jnp.inf, F32))
            mb_scr[...] = jnp.broadcast_to(jnp.max(m, axis=-1, keepdims=True), mb_scr.shape)
            l_scr[...] = jnp.zeros(l_scr.shape, F32)
            acc_scr[...] = jnp.zeros(acc_scr.shape, F32)

            def redo(kc, _, p=p, qs=qs):
                dl, dacc = exp_pv(p, qs, kc, mb_scr[...])
                l_scr[...] = l_scr[...] + dl
                acc_scr[...] = acc_scr[...] + dacc
                return 0

            lax.fori_loop(0, n_chunks, redo, 0)
            finish(p, jnp.sum(l_scr[...], axis=-1, keepdims=True))


def _attn_call(lam_p, score_bound, q, kt, v, sg, lam_init, first_tile):
    b, _, la, _ = q.shape
    nt = la // ROW_TILE
    assert (nt - 1) % KEY_UNROLL == 0
    off = first_tile
    return pl.pallas_call(
        functools.partial(_attn_kernel, lam_init=lam_init, first_tile=off),
        grid=(b, nt - off),
        in_specs=[
            pl.BlockSpec((SUBLANES, LANES), lambda i, j: (0, 0)),
            pl.BlockSpec((1, LANES), lambda i, j: (0, 0)),
            pl.BlockSpec((1, N_PAIRS, ROW_TILE, LANES), lambda i, j: (i, 0, j + off, 0)),
            pl.BlockSpec((1, nt, N_PAIRS, LANES, ROW_TILE), lambda i, j: (i, 0, 0, 0, 0)),
            pl.BlockSpec((1, N_PAIRS, la, LANES), lambda i, j: (i, 0, 0, 0)),
            pl.BlockSpec((1, LANES), lambda i, j: (0, 0)),
        ],
        out_specs=pl.BlockSpec((1, N_PAIRS, ROW_TILE, LANES), lambda i, j: (i, 0, j, 0)),
        out_shape=jax.ShapeDtypeStruct((b, N_PAIRS, la - off * ROW_TILE, LANES), BF16),
        scratch_shapes=[pltpu.VMEM((4 * ROW_TILE, LANES), F32),
                        pltpu.VMEM((4 * ROW_TILE, LANES), F32),
                        pltpu.VMEM((4 * ROW_TILE, LANES), F32)],
        compiler_params=pltpu.CompilerParams(vmem_limit_bytes=ATTN_VMEM_LIMIT),
        name="attn",
    )(lam_p, score_bound, q, kt, v, sg)


def _store_row_tiles(ref, val):
    rows = val.shape[0]
    for c in range(SUBLANES):
        ref[pl.ds(c, rows, stride=SUBLANES), :] = val[:, c * LANES:(c + 1) * LANES]


def _load_row_tiles(ref, first_row, rows, c):
    return ref[pl.ds(first_row * SUBLANES + c, rows, stride=SUBLANES), :]


def _shift_rows(a, k):
    return pltpu.roll(a, k % a.shape[0], axis=0)


def _out_kernel(*refs, seq_len, first_tile):
    for s in range(refs[1].shape[0]):
        _out_one_sample(s, *refs, seq_len=seq_len, first_tile=first_tile)


def _out_one_sample(s, c_ref, x_ref, oa_ref, cp_ref, cpp_ref, cpn_ref, cw_ref, pw_ref, ps_ref, wo_ref,
                    mod_ref, g2_ref, wrh_ref, wrl_ref, br_ref,
                    xo_ref, hu_ref, te_ref, tw_ref, *, seq_len, first_tile):
    tm = x_ref.shape[1]
    j = pl.program_id(1) + first_tile
    nt = pl.num_programs(1) + first_tile
    halo = POOL_HALO
    has_prev = j >= 2
    has_next = (j >= 1) & (j < nt - 1)
    prev = jnp.where(has_prev, cpp_ref[s], 0.0)
    nxt = jnp.where(has_next, cpn_ref[s], 0.0)
    ext = jnp.concatenate([prev, cp_ref[s], nxt], axis=0)

    z = ext[:, CONV_W:2 * CONV_W]
    conv = (cw_ref[0:1, :] * _shift_rows(z, 1) + cw_ref[1:2, :] * z + cw_ref[2:3, :] * _shift_rows(z, -1))
    o_conv = ext[halo:halo + tm, 0:CONV_W] * conv[halo:halo + tm]

    u = ext[:, 2 * CONV_W:]
    a2 = u + _shift_rows(u, 1)
    a4 = _shift_rows(a2, -1) + _shift_rows(a2, 1)
    a8 = _shift_rows(a4, -2) + _shift_rows(a4, 2)
    a16 = _shift_rows(a8, -4) + _shift_rows(a8, 4)
    lane = lax.broadcasted_iota(I32, (tm, POOL_W), 1)
    grp = lane // POOL_GC
    wsum = jnp.where(grp == 0, a2[halo:halo + tm],
                     jnp.where(grp == 1, a4[halo:halo + tm],
                               jnp.where(grp == 2, a8[halo:halo + tm], a16[halo:halo + tm])))
    half = jnp.where(grp == 0, 1, jnp.where(grp == 1, 2, jnp.where(grp == 2, 4, 8)))
    row = lax.broadcasted_iota(I32, (tm, POOL_W), 0)
    pos = jnp.where(j == 0, row, (j - 1) * tm + row)
    n_seq = jnp.where(j == 0, CTX_LEN, seq_len)
    cnt = jnp.minimum(pos + half, n_seq) - jnp.maximum(pos - half, 0)
    dlt = wsum / cnt.astype(F32) - u[halo:halo + tm]
    o_pool = _dot(dlt.astype(BF16), pw_ref[...]) * ps_ref[...]

    mix = _dot(o_conv.astype(BF16), wo_ref[ATTN_W:ATTN_W + CONV_W, :])
    mix = mix + _dot(o_pool.astype(BF16), wo_ref[ATTN_W + CONV_W:, :])
    for p in range(N_PAIRS):
        mix = mix + _dot(oa_ref[s, p], wo_ref[p * LANES:(p + 1) * LANES, :])
    x = _stream_tile(c_ref, x_ref, j, s) + mod_ref[s, 2:3, :] * mix
    xo_ref[s] = x

    h2 = _rms_mod(x, g2_ref[...], mod_ref[s, 3:4, :], mod_ref[s, 4:5, :])
    _store_row_tiles(hu_ref.at[s], h2)

    h_hi, h_lo = _split_bf16(h2)
    logits = (_dot(h_hi, wrh_ref[...]) + _dot(h_lo, wrh_ref[...]) + _dot(h_hi, wrl_ref[...])
              + br_ref[...])
    lanef = lax.broadcasted_iota(I32, (tm, LANES), 1).astype(F32)
    work = logits
    tops = []
    for k in range(TOP_K):
        mk = jnp.max(work, axis=-1, keepdims=True)
        ik = jnp.min(jnp.where(work == mk, lanef, float(LANES)), axis=-1, keepdims=True)
        te_ref[s, :, k:k + 1] = ik.astype(I32)
        work = jnp.where(lanef == ik, -jnp.inf, work)
        tops.append(mk)
    es = [jnp.exp(mk - tops[0]) for mk in tops]
    den = es[0] + es[1] + es[2] + es[3]
    for k in range(TOP_K):
        tw_ref[s, :, k:k + 1] = es[k] / den


def _out_call(s_ctx, s_x, oa, cp, cw, pw, ps, wo, mod, g2, wrh, wrl, br, seq_len, first_tile):
    b, _, d = s_x.shape
    la = CTX_LEN + seq_len
    nt = la // ROW_TILE
    off = first_tile
    lo = la - off * ROW_TILE
    hb = ROW_TILE // POOL_HALO
    nhb = la // POOL_HALO
    sps = STEP_SAMPLES
    assert b % sps == 0

    def mod_idx(i, j):
        return (jnp.where(j + off == 0, b // sps, i), 0, 0)

    const2 = lambda i, j: (0, 0)
    return pl.pallas_call(
        functools.partial(_out_kernel, seq_len=seq_len, first_tile=off),
        grid=(b // sps, nt - off),
        in_specs=_stream_specs(d, off, s_x.shape[1] < la, sps) + [
            pl.BlockSpec((sps, N_PAIRS, ROW_TILE, LANES), lambda i, j: (i, 0, j, 0)),
            pl.BlockSpec((sps, ROW_TILE, 3 * CONV_W), lambda i, j: (i, j + off, 0)),
            pl.BlockSpec((sps, POOL_HALO, 3 * CONV_W), lambda i, j: (i, jnp.maximum((j + off) * hb - 1, 0), 0)),
            pl.BlockSpec((sps, POOL_HALO, 3 * CONV_W),
                         lambda i, j: (i, jnp.minimum((j + off + 1) * hb, nhb - 1), 0)),
            pl.BlockSpec((3, CONV_W), const2),
            pl.BlockSpec((POOL_W, POOL_W), const2),
            pl.BlockSpec((1, POOL_W), const2),
            pl.BlockSpec((d, d), const2),
            pl.BlockSpec((sps, N_MOD, d), mod_idx),
            pl.BlockSpec((1, d), const2),
            pl.BlockSpec((d, LANES), const2),
            pl.BlockSpec((d, LANES), const2),
            pl.BlockSpec((1, LANES), const2),
        ],
        out_specs=[
            pl.BlockSpec((sps, ROW_TILE, d), lambda i, j: (i, j, 0)),
            pl.BlockSpec((sps, ROW_TILE * SUBLANES, LANES), lambda i, j: (i, j, 0)),
            pl.BlockSpec((sps, ROW_TILE, TOP_K), lambda i, j: (i, j, 0)),
            pl.BlockSpec((sps, ROW_TILE, TOP_K), lambda i, j: (i, j, 0)),
        ],
        out_shape=[
            jax.ShapeDtypeStruct((b, lo, d), F32),
            jax.ShapeDtypeStruct((b, lo * SUBLANES, LANES), F32),
            jax.ShapeDtypeStruct((b, lo, TOP_K), I32),
            jax.ShapeDtypeStruct((b, lo, TOP_K), F32),
        ],
        compiler_params=pltpu.CompilerParams(vmem_limit_bytes=VMEM_LIMIT),
        name="outproj",
    )(s_ctx, s_x, oa, cp, cp, cp, cw, pw, ps, wo, mod, g2, wrh, wrl, br)


def _rank_kernel(te_ref, tri_ref, dest_ref, cnt_ref, carry):
    ph = pl.program_id(0)
    i = pl.program_id(1)
    tr = te_ref.shape[0]
    lane1 = lax.broadcasted_iota(I32, (SUBLANES, LANES), 1)

    @pl.when((ph == 0) & (i == 0))
    def _():
        carry[...] = jnp.zeros_like(carry)

    @pl.when((ph == 1) & (i == 0))
    def _():
        cnt = carry[...]
        cnt_ref[...] = cnt.astype(I32)
        padded = jnp.ceil(cnt * (1.0 / EXPERT_ROWS)) * EXPERT_ROWS
        incl = padded
        for sh in (1, 2, 4, 8, 16):
            incl = incl + jnp.where(lane1 >= sh, pltpu.roll(incl, sh, axis=1), 0.0)
        carry[...] = incl - padded

    e = te_ref[...]
    lane = lax.broadcasted_iota(I32, (tr, LANES), 1)
    ohs = [(e[:, k:k + 1] == lane).astype(F32) for k in range(TOP_K)]
    m = ohs[0] + ohs[1] + ohs[2] + ohs[3]

    @pl.when(ph == 0)
    def _():
        dest_ref[...] = jnp.zeros(dest_ref.shape, I32)

    @pl.when(ph == 1)
    def _():
        base = _dot(tri_ref[...], m.astype(BF16)) + carry[0:1, :]
        for k in range(TOP_K):
            dest_ref[:, k:k + 1] = jnp.sum(ohs[k] * base, axis=-1, keepdims=True).astype(I32)

    carry[...] = carry[...] + jnp.sum(m, axis=0, keepdims=True)


def _rank_call(te):
    t = te.shape[0]
    nt = t // RANK_TILE
    r = lax.broadcasted_iota(I32, (RANK_TILE, RANK_TILE), 0)
    c = lax.broadcasted_iota(I32, (RANK_TILE, RANK_TILE), 1)
    tri = (c < r).astype(BF16)
    return pl.pallas_call(
        _rank_kernel,
        grid=(2, nt),
        in_specs=[
            pl.BlockSpec((RANK_TILE, TOP_K), lambda ph, i: (i, 0)),
            pl.BlockSpec((RANK_TILE, RANK_TILE), lambda ph, i: (0, 0)),
        ],
        out_specs=[
            pl.BlockSpec((RANK_TILE, TOP_K), lambda ph, i: (ph * i, 0)),
            pl.BlockSpec((SUBLANES, LANES), lambda ph, i: (0, 0)),
        ],
        out_shape=[
            jax.ShapeDtypeStruct((t, TOP_K), I32),
            jax.ShapeDtypeStruct((SUBLANES, LANES), I32),
        ],
        scratch_shapes=[pltpu.VMEM((SUBLANES, LANES), F32)],
        compiler_params=pltpu.CompilerParams(dimension_semantics=("arbitrary", "arbitrary")),
        name="rank",
    )(te, tri)


def _disp_kernel(pad0_ref, npad_ref, dest_ref, hu_ref, xs_ref, zero, sem, zsem):
    def row_copy(r, k):
        return pltpu.make_async_copy(hu_ref.at[r], xs_ref.at[dest_ref[r * TOP_K + k]], sem)

    def start(r, _):
        for k in range(TOP_K):
            row_copy(r, k).start(priority=k % 2)
        return 0

    lax.fori_loop(0, ROW_TILE, start, 0)
    for _ in range(TOP_K):
        pltpu.make_async_copy(hu_ref, hu_ref, sem).wait()

    @pl.when(pl.program_id(0) == pl.num_programs(0) - 1)
    def _():
        zero[...] = jnp.zeros(zero.shape, F32)
        run = zero.shape[0]

        def zero_range(e, _):
            n_runs = npad_ref[e] // run
            n_rows = npad_ref[e] - n_runs * run
            row0 = pad0_ref[e] + n_runs * run

            def run_copy(c):
                return pltpu.make_async_copy(zero, xs_ref.at[pl.ds(pad0_ref[e] + c * run, run)], zsem)

            def row_copy0(r):
                return pltpu.make_async_copy(zero.at[0], xs_ref.at[row0 + r], zsem)

            def loop(n, make, wait):
                def body(c, carry):
                    if wait:
                        make(c).wait()
                    else:
                        make(c).start()
                    return carry
                lax.fori_loop(0, n, body, 0)

            loop(n_runs, run_copy, False)
            loop(n_rows, row_copy0, False)
            loop(n_runs, run_copy, True)
            loop(n_rows, row_copy0, True)
            return 0

        lax.fori_loop(0, pad0_ref.shape[0], zero_range, 0)


def _disp_call(pad0, npad, dest_flat, hu, n_rows):
    t = hu.shape[0]
    grid_spec = pltpu.PrefetchScalarGridSpec(
        num_scalar_prefetch=2,
        grid=(t // ROW_TILE,),
        in_specs=[
            pl.BlockSpec((ROW_TILE * TOP_K,), lambda i, p0, n: (i,), memory_space=pltpu.SMEM),
            pl.BlockSpec((ROW_TILE, SUBLANES, LANES), lambda i, p0, n: (i, 0, 0)),
        ],
        out_specs=pl.BlockSpec(memory_space=pl.ANY),
        scratch_shapes=[pltpu.VMEM((ZERO_RUN, SUBLANES, LANES), F32), pltpu.SemaphoreType.DMA(()),
                        pltpu.SemaphoreType.DMA(())],
    )
    return pl.pallas_call(
        _disp_kernel,
        grid_spec=grid_spec,
        out_shape=jax.ShapeDtypeStruct((n_rows, SUBLANES, LANES), F32),
        compiler_params=pltpu.CompilerParams(dimension_semantics=("arbitrary",)),
        name="dispatch",
    )(pad0, npad, dest_flat, hu)


def _exp_kernel(be_ref, nu_ref, xs_ref, wgu_ref, bgu_ref, wdn_ref, bdn_ref, ys_ref, wgu_b, wdn_b):
    i = pl.program_id(0)
    ff = wdn_ref.shape[1]
    rows = xs_ref.shape[0] // SUBLANES

    @pl.when((i == 0) | (be_ref[i] != be_ref[jnp.maximum(i - 1, 0)]))
    def _():
        wgu_b[...] = wgu_ref[0].astype(BF16)
        wdn_b[...] = wdn_ref[0].astype(BF16)

    @pl.when(i < nu_ref[0])
    def _():
        xb = jnp.concatenate([_load_row_tiles(xs_ref, 0, rows, c).astype(BF16) for c in range(SUBLANES)],
                             axis=1)
        gu = _dot(xb, wgu_b[...]) + bgu_ref[0]
        glu = jnp.minimum(gu[:, :ff], SWIGLU_LIMIT)
        lin = jnp.clip(gu[:, ff:], -SWIGLU_LIMIT, SWIGLU_LIMIT)
        act = glu * jax.nn.sigmoid(SWIGLU_ALPHA * glu) * (lin + 1.0)
        _store_row_tiles(ys_ref, _dot(act.astype(BF16), wdn_b[...]) + bdn_ref[0])

    @pl.when(i >= nu_ref[0])
    def _():
        ys_ref[...] = jnp.zeros_like(ys_ref)


def _exp_call(block_e, n_used, xs, layer, wgu, bgu, wdn, bdn):
    n_rows = xs.shape[0] // SUBLANES
    _, _, d, ff2 = wgu.shape
    ff = ff2 // 2
    nb = n_rows // EXPERT_ROWS
    grid_spec = pltpu.PrefetchScalarGridSpec(
        num_scalar_prefetch=2,
        grid=(nb,),
        in_specs=[
            pl.BlockSpec((EXPERT_ROWS * SUBLANES, LANES), lambda i, be, nu: (jnp.minimum(i, nu[0] - 1), 0)),
            pl.BlockSpec((None, 1, d, ff2), lambda i, be, nu: (layer, be[i], 0, 0)),
            pl.BlockSpec((1, 1, ff2), lambda i, be, nu: (be[i], 0, 0)),
            pl.BlockSpec((None, 1, ff, d), lambda i, be, nu: (layer, be[i], 0, 0)),
            pl.BlockSpec((1, 1, d), lambda i, be, nu: (be[i], 0, 0)),
        ],
        out_specs=pl.BlockSpec((EXPERT_ROWS * SUBLANES, LANES), lambda i, be, nu: (i, 0)),
        scratch_shapes=[pltpu.VMEM((d, ff2), BF16), pltpu.VMEM((ff, d), BF16)],
    )
    return pl.pallas_call(
        _exp_kernel,
        grid_spec=grid_spec,
        out_shape=jax.ShapeDtypeStruct((n_rows * SUBLANES, LANES), F32),
        compiler_params=pltpu.CompilerParams(dimension_semantics=("arbitrary",),
                                             vmem_limit_bytes=EXPERT_VMEM_LIMIT),
        name="experts",
    )(block_e, n_used, xs, wgu, bgu.reshape(-1, 1, ff2), wdn, bdn.reshape(-1, 1, d))


def _comb_kernel(dest_ref, dnext_ref, x_ref, mod_ref, tw_ref, ys_ref, o_ref, buf0, buf1, sem0, sem1):
    t = pl.program_id(0) * pl.num_programs(1) + pl.program_id(1)
    n_steps = pl.num_programs(0) * pl.num_programs(1)

    def gather(idx_ref, buf, sem):
        def start(r, _):
            for k in range(TOP_K):
                r0 = pl.multiple_of((k * ROW_TILE + r) * SUBLANES, SUBLANES)
                pltpu.make_async_copy(ys_ref.at[idx_ref[r * TOP_K + k]], buf.at[pl.ds(r0, SUBLANES)],
                                      sem).start(priority=k % 2)
            return 0

        lax.fori_loop(0, ROW_TILE, start, 0)

    def consume(buf, sem):
        pltpu.make_async_copy(buf, buf, sem).wait()
        w = [tw_ref[0, :, k:k + 1] for k in range(TOP_K)]
        for c in range(SUBLANES):
            acc = _load_row_tiles(buf, 0, ROW_TILE, c) * w[0]
            for k in range(1, TOP_K):
                acc = acc + _load_row_tiles(buf, k * ROW_TILE, ROW_TILE, c) * w[k]
            cols = slice(c * LANES, (c + 1) * LANES)
            o_ref[0, :, cols] = x_ref[0, :, cols] + mod_ref[0, 5:6, cols] * acc

    for parity, (cur, cur_sem, nxt, nxt_sem) in enumerate(((buf0, sem0, buf1, sem1), (buf1, sem1, buf0, sem0))):
        @pl.when(lax.rem(t, 2) == parity)
        def _(cur=cur, cur_sem=cur_sem, nxt=nxt, nxt_sem=nxt_sem):
            @pl.when(t == 0)
            def _():
                gather(dest_ref, cur, cur_sem)

            @pl.when(t + 1 < n_steps)
            def _():
                gather(dnext_ref, nxt, nxt_sem)

            consume(cur, cur_sem)


def _comb_call(dest_flat, s, mod, tw, ys, first_tile):
    b, rows, d = s.shape
    nt = rows // ROW_TILE

    def mod_idx(i, j):
        return (jnp.where(j + first_tile == 0, b, i), 0, 0)

    return pl.pallas_call(
        _comb_kernel,
        grid=(b, nt),
        in_specs=[
            pl.BlockSpec((ROW_TILE * TOP_K,), lambda i, j: (i * nt + j,), memory_space=pltpu.SMEM),
            pl.BlockSpec((ROW_TILE * TOP_K,), lambda i, j: (jnp.minimum(i * nt + j + 1, b * nt - 1),),
                         memory_space=pltpu.SMEM),
            pl.BlockSpec((1, ROW_TILE, d), lambda i, j: (i, j, 0)),
            pl.BlockSpec((1, N_MOD, d), mod_idx),
            pl.BlockSpec((1, ROW_TILE, TOP_K), lambda i, j: (i, j, 0)),
            pl.BlockSpec(memory_space=pl.ANY),
        ],
        out_specs=pl.BlockSpec((1, ROW_TILE, d), lambda i, j: (i, j, 0)),
        out_shape=jax.ShapeDtypeStruct((b, rows, d), F32),
        scratch_shapes=[pltpu.VMEM((TOP_K * ROW_TILE * SUBLANES, LANES), F32),
                        pltpu.VMEM((TOP_K * ROW_TILE * SUBLANES, LANES), F32),
                        pltpu.SemaphoreType.DMA(()), pltpu.SemaphoreType.DMA(())],
        compiler_params=pltpu.CompilerParams(dimension_semantics=("arbitrary", "arbitrary"),
                                             vmem_limit_bytes=VMEM_LIMIT),
        name="combine",
    )(dest_flat, dest_flat, s, mod, tw, ys)


def _rope_tables(seq_len):
    rows = seq_len // GRID_W
    row = jnp.broadcast_to(jnp.arange(rows, dtype=F32)[:, None], (rows, GRID_W)).reshape(seq_len)
    col = jnp.broadcast_to(jnp.arange(GRID_W, dtype=F32)[None, :], (rows, GRID_W)).reshape(seq_len)
    inv = ROPE_BASE ** (-jnp.arange(ROPE_PAIRS, dtype=F32) / ROPE_PAIRS)
    ang = jnp.stack([row, col], axis=0)[:, None, :] * inv[None, :, None]
    cos_l = jnp.cos(ang).reshape(2 * ROPE_PAIRS, seq_len)
    sin_l = jnp.sin(ang).reshape(2 * ROPE_PAIRS, seq_len)
    cos_t = jnp.concatenate([jnp.ones((2 * ROPE_PAIRS, CTX_LEN), F32), cos_l], axis=1)
    sin_t = jnp.concatenate([jnp.zeros((2 * ROPE_PAIRS, CTX_LEN), F32), sin_l], axis=1)
    return cos_t, sin_t


def kernel(x, c, ctx, c_ctx, w_mod, b_mod, norm1_g, norm2_g, w_in, q_norm_g, k_norm_g,
           lambda_q1, lambda_k1, lambda_q2, lambda_k2, subln_g, conv_w, pool_w, pool_scale, w_out,
           router_w, router_b, w_gate_up, b_gate_up, w_down, b_down):
    b, seq_len, d = x.shape
    assert d == SUBLANES * LANES
    depth = w_mod.shape[0]
    la = CTX_LEN + seq_len

    rows = -(-(b + STEP_SAMPLES) // SUBLANES) * SUBLANES
    c_all = jnp.zeros((rows, d), F32).at[:b].set(c).at[b:b + STEP_SAMPLES].set(c_ctx)
    mod = _mod_call(c_all, w_mod, b_mod).reshape(depth, rows, N_MOD, d)

    cos_t, sin_t = _rope_tables(seq_len)
    s_ctx, s_x = ctx, x

    for l in range(depth):
        first_tile = 1 if l + 1 == depth else 0
        t_all = b * (la - first_tile * ROW_TILE)
        n_rows = -(-(t_all * TOP_K + N_EXPERTS * (EXPERT_ROWS - 1)) // EXPERT_ROWS) * EXPERT_ROWS
        nb = n_rows // EXPERT_ROWS
        lam_init = 0.8 - 0.6 * math.exp(-0.3 * l)
        wl = w_in[l]
        wm = wl[:, 2 * ATTN_W:].astype(BF16)
        wqk = wl[:, :2 * ATTN_W].T.astype(BF16)
        q_gain = jnp.tile(q_norm_g[l], ATTN_W // QK_DIM) * (LOG2E * QK_DIM ** -0.5)
        k_gain = jnp.tile(k_norm_g[l], ATTN_W // QK_DIM)
        gain = jnp.broadcast_to(jnp.concatenate([q_gain, k_gain])[:, None], (2 * ATTN_W, ROW_TILE))
        q, kt, v, cp = _in_call(s_ctx, s_x, la, mod[l], norm1_g[l][None], wm, wqk, gain, cos_t, sin_t)

        lam_p = jnp.zeros((SUBLANES, LANES), F32)
        lam_p = lam_p.at[0, :QK_DIM].set(lambda_q1[l]).at[1, :QK_DIM].set(lambda_k1[l])
        lam_p = lam_p.at[2, :QK_DIM].set(lambda_q2[l]).at[3, :QK_DIM].set(lambda_k2[l])
        sg = jnp.tile(subln_g[l], 2)[None]
        score_bound = jnp.full((1, LANES), QK_DIM * BOUND_SLACK, F32) * (
            jnp.max(jnp.abs(q_gain)) * jnp.max(jnp.abs(k_gain)))
        oa = _attn_call(lam_p, score_bound, q, kt, v, sg, lam_init, first_tile)

        pw = jax.scipy.linalg.block_diag(*[pool_w[l, g] for g in range(len(POOL_WINDOWS))]).astype(BF16)
        wr = jnp.zeros((d, LANES), F32).at[:, :N_EXPERTS].set(router_w[l])
        wrh, wrl = _split_bf16(wr)
        br = jnp.full((1, LANES), NEG_BIG, F32).at[0, :N_EXPERTS].set(router_b[l])
        s, hu, te, tw = _out_call(s_ctx, s_x, oa, cp, conv_w[l], pw, pool_scale[l][None],
                                  w_out[l].astype(BF16), mod[l], norm2_g[l][None], wrh, wrl, br,
                                  seq_len, first_tile)

        dest, counts = _rank_call(te.reshape(t_all, TOP_K))
        cnt = counts[0, :N_EXPERTS]
        padded = (cnt + EXPERT_ROWS - 1) // EXPERT_ROWS * EXPERT_ROWS
        pad_end = jnp.cumsum(padded)
        n_used = (pad_end[-1] // EXPERT_ROWS).astype(I32)
        blk = jnp.minimum(jnp.arange(nb, dtype=I32), n_used - 1) * EXPERT_ROWS
        block_e = jnp.sum((pad_end[None, :] <= blk[:, None]).astype(I32), axis=1)
        block_e = jnp.minimum(block_e, N_EXPERTS - 1)
        dest_flat = dest.reshape(t_all * TOP_K)

        pad0 = jnp.concatenate([pad_end - padded + cnt, pad_end[-1:]])
        npad = jnp.concatenate([padded - cnt, n_rows - pad_end[-1:]])
        xs = _disp_call(pad0, npad, dest_flat, hu.reshape(t_all, SUBLANES, LANES), n_rows)
        ys = _exp_call(block_e, n_used.reshape(1), xs.reshape(n_rows * SUBLANES, LANES), l,
                       w_gate_up, b_gate_up[l], w_down, b_down[l])
        s = _comb_call(dest_flat, s, mod[l], tw, ys.reshape(n_rows, SUBLANES, LANES), first_tile)
        s_ctx = s_x = s
    return s
```
